```python
import math
import jax, jax.numpy as jnp
from jax import lax
import numpy as np

D_MODEL = 1024
BATCH = 16
SEQ = 2048
DEPTH = 2

CHUNK = 64
Q_BLOCK = 128
EPS = 1e-6

FOX_HEAD_DIM = 64
FOX_WIDTH = D_MODEL // 4
FOX_HEADS = FOX_WIDTH // FOX_HEAD_DIM
HGRN_KEY_DIM = 128
HGRN_WIDTH = D_MODEL // 2
HGRN_HEADS = HGRN_WIDTH // 128
HGRN_VAL_DIM = HGRN_WIDTH // HGRN_HEADS
HGRN_FORGET_DIM = HGRN_HEADS * HGRN_KEY_DIM
S5_WIDTH = D_MODEL // 4
S5_GROUP = 16
S5_GROUPS = S5_WIDTH // S5_GROUP
S5_STATE = 64

MIX_WIDTH = FOX_WIDTH + HGRN_WIDTH + S5_WIDTH
D_FF = 2816

IN_SIZES = [FOX_WIDTH, FOX_WIDTH, FOX_WIDTH, FOX_HEADS,
            HGRN_FORGET_DIM, HGRN_FORGET_DIM, HGRN_WIDTH, HGRN_WIDTH,
            S5_WIDTH]
IN_WIDTH = sum(IN_SIZES)
IN_SPLITS = [int(v) for v in np.cumsum(IN_SIZES)[:-1]]

kernel_name = "hybrid_fox_hgrn2_s5_macaron"


def rms_norm(x, gain):
    xf = x.astype(jnp.float32)
    y = xf * lax.rsqrt(jnp.mean(xf * xf, axis=-1, keepdims=True) + EPS)
    return (y * gain.astype(jnp.float32)).astype(x.dtype)


def swiglu(x, w_gate, w_up, w_down):
    return (jax.nn.silu(x @ w_gate) * (x @ w_up)) @ w_down


def fox_attention(q, k, v, f_logit):
    L = q.shape[1]
    scale = 1.0 / math.sqrt(q.shape[-1])
    c = jnp.cumsum(jax.nn.log_sigmoid(f_logit.astype(jnp.float32)), axis=1)
    c = c.transpose(0, 2, 1)
    qf = q.astype(jnp.float32).transpose(0, 2, 1, 3)
    kf = k.astype(jnp.float32).transpose(0, 2, 1, 3)
    vf = v.astype(jnp.float32).transpose(0, 2, 1, 3)
    outs = []
    for blk in range(L // Q_BLOCK):
        qs, qe = blk * Q_BLOCK, (blk + 1) * Q_BLOCK
        s = jnp.einsum('bhqd,bhkd->bhqk', qf[:, :, qs:qe], kf[:, :, :qe]) * scale
        s = s + c[:, :, qs:qe, None] - c[:, :, None, :qe]
        mask = jnp.arange(qs, qe)[:, None] >= jnp.arange(qe)[None, :]
        p = jax.nn.softmax(jnp.where(mask, s, -jnp.inf), axis=-1)
        outs.append(jnp.einsum('bhqk,bhkd->bhqd', p, vf[:, :, :qe]))
    return jnp.concatenate(outs, axis=2).transpose(0, 2, 1, 3)


def _to_chunks(t, n_chunks):
    b, _, h, d = t.shape
    return t.reshape(b, n_chunks, CHUNK, h, d).transpose(1, 0, 3, 2, 4)


def hgrn2(q, f_logit, i_in, lb):
    bsz, L, h, kd = q.shape
    vd = i_in.shape[-1]
    n = L // CHUNK
    f = lb + (1.0 - lb) * jax.nn.sigmoid(f_logit.astype(jnp.float32))
    log_f = jnp.log(f)
    key = 1.0 - f
    qc = _to_chunks(q.astype(jnp.float32), n)
    kc = _to_chunks(key, n)
    gc = _to_chunks(log_f, n)
    vc = _to_chunks(i_in.astype(jnp.float32), n)
    mask = (jnp.arange(CHUNK)[:, None] >= jnp.arange(CHUNK)[None, :])[None, None, :, :, None]

    def step(S, inp):
        q_c, k_c, g_c, v_c = inp
        b = jnp.cumsum(g_c, axis=2)
        diff = b[:, :, :, None, :] - b[:, :, None, :, :]
        decay = jnp.exp(jnp.where(mask, diff, -jnp.inf))
        attn = jnp.einsum('bhtk,bhsk,bhtsk->bhts', q_c, k_c, decay)
        o = jnp.einsum('bhts,bhsv->bhtv', attn, v_c)
        o = o + jnp.einsum('bhtk,bhkv->bhtv', q_c * jnp.exp(b), S)
        b_last = b[:, :, -1]
        S_new = jnp.exp(b_last)[..., None] * S + jnp.einsum(
            'bhsk,bhsv->bhkv', k_c * jnp.exp(b_last[:, :, None, :] - b), v_c)
        return S_new, o

    S0 = jnp.zeros((bsz, h, kd, vd), jnp.float32)
    _, o = lax.scan(step, S0, (qc, kc, gc, vc))
    return o.transpose(1, 0, 3, 2, 4).reshape(bsz, L, h, vd)


def _complex_scan_combine(e1, e2):
    a1r, a1i, b1r, b1i = e1
    a2r, a2i, b2r, b2i = e2
    ar = a1r * a2r - a1i * a2i
    ai = a1r * a2i + a1i * a2r
    br = a2r * b1r - a2i * b1i + b2r
    bi = a2r * b1i + a2i * b1r + b2i
    return (ar, ai, br, bi)


def s5_block(u, lam_re, lam_im, log_step, B_re, B_im, C_re, C_im, D, glu_w, glu_b):
    bsz, L, _ = u.shape
    uf = u.astype(jnp.float32).reshape(bsz, L, S5_GROUPS, S5_GROUP)
    lr = lam_re.astype(jnp.float32)
    li = lam_im.astype(jnp.float32)
    dt = jnp.exp(log_step.astype(jnp.float32))[:, None]
    mag = jnp.exp(lr * dt)
    lbar_re = mag * jnp.cos(li * dt)
    lbar_im = mag * jnp.sin(li * dt)
    nr, ni = lbar_re - 1.0, lbar_im
    den = lr * lr + li * li
    coef_re = (nr * lr + ni * li) / den
    coef_im = (ni * lr - nr * li) / den
    Br = B_re.astype(jnp.float32)
    Bi = B_im.astype(jnp.float32)
    bbar_re = coef_re[..., None] * Br - coef_im[..., None] * Bi
    bbar_im = coef_re[..., None] * Bi + coef_im[..., None] * Br
    bu_re = jnp.einsum('blgh,gph->blgp', uf, bbar_re)
    bu_im = jnp.einsum('blgh,gph->blgp', uf, bbar_im)
    a_re = jnp.broadcast_to(lbar_re, bu_re.shape)
    a_im = jnp.broadcast_to(lbar_im, bu_im.shape)
    _, _, x_re, x_im = lax.associative_scan(
        _complex_scan_combine, (a_re, a_im, bu_re, bu_im), axis=1)
    y = (jnp.einsum('blgp,ghp->blgh', x_re, C_re.astype(jnp.float32))
         - jnp.einsum('blgp,ghp->blgh', x_im, C_im.astype(jnp.float32))
         + D.astype(jnp.float32).reshape(S5_GROUPS, S5_GROUP) * uf)
    z = jax.nn.gelu(y.reshape(bsz, L, S5_WIDTH))
    return z * jax.nn.sigmoid(z @ glu_w.astype(jnp.float32) + glu_b.astype(jnp.float32))


def setup_inputs(seed: int = 0) -> dict:
    key = jax.random.key(seed)
    ks = jax.random.split(key, 32)
    f32 = jnp.float32

    def nrm(k, shape, scale):
        return jax.random.normal(k, shape, f32) * scale

    def gain(k, shape):
        return 1.0 + 0.02 * jax.random.normal(k, shape, f32)

    lam_im_base = jnp.pi * jnp.arange(S5_STATE, dtype=f32)
    log_lo, log_hi = math.log(0.001), math.log(0.1)
    return {
        "x": nrm(ks[0], (BATCH, SEQ, D_MODEL), 1.0),
        "ffn1_norm": gain(ks[1], (DEPTH, D_MODEL)),
        "ffn1_w_gate": nrm(ks[2], (DEPTH, D_MODEL, D_FF), D_MODEL ** -0.5),
        "ffn1_w_up": nrm(ks[3], (DEPTH, D_MODEL, D_FF), D_MODEL ** -0.5),
        "ffn1_w_down": nrm(ks[4], (DEPTH, D_FF, D_MODEL), D_FF ** -0.5),
        "mix_norm": gain(ks[5], (DEPTH, D_MODEL)),
        "w_in": nrm(ks[6], (DEPTH, D_MODEL, IN_WIDTH), D_MODEL ** -0.5),
        "fox_f_bias": nrm(ks[7], (DEPTH, FOX_HEADS), 0.1),
        "fox_q_gain": gain(ks[8], (DEPTH, FOX_HEAD_DIM)),
        "fox_k_gain": gain(ks[9], (DEPTH, FOX_HEAD_DIM)),
        "hgrn_lb": nrm(ks[10], (DEPTH, HGRN_FORGET_DIM), 0.1),
        "hgrn_o_gain": gain(ks[11], (DEPTH, HGRN_WIDTH)),
        "s5_lambda_re": -0.5 + nrm(ks[12], (DEPTH, S5_GROUPS, S5_STATE), 0.01),
        "s5_lambda_im": lam_im_base + nrm(ks[13], (DEPTH, S5_GROUPS, S5_STATE), 0.01),
        "s5_log_step": log_lo + (log_hi - log_lo) * jax.random.uniform(ks[14], (DEPTH, S5_GROUPS), f32),
        "s5_B_re": nrm(ks[15], (DEPTH, S5_GROUPS, S5_STATE, S5_GROUP), (2 * S5_GROUP) ** -0.5),
        "s5_B_im": nrm(ks[16], (DEPTH, S5_GROUPS, S5_STATE, S5_GROUP), (2 * S5_GROUP) ** -0.5),
        "s5_C_re": nrm(ks[17], (DEPTH, S5_GROUPS, S5_GROUP, S5_STATE), S5_STATE ** -0.5),
        "s5_C_im": nrm(ks[18], (DEPTH, S5_GROUPS, S5_GROUP, S5_STATE), S5_STATE ** -0.5),
        "s5_D": nrm(ks[19], (DEPTH, S5_WIDTH), 1.0),
        "s5_glu_w": nrm(ks[20], (DEPTH, S5_WIDTH, S5_WIDTH), S5_WIDTH ** -0.5),
        "s5_glu_b": nrm(ks[21], (DEPTH, S5_WIDTH), 0.02),
        "w_out": nrm(ks[22], (DEPTH, MIX_WIDTH, D_MODEL), MIX_WIDTH ** -0.5),
        "ffn2_norm": gain(ks[23], (DEPTH, D_MODEL)),
        "ffn2_w_gate": nrm(ks[24], (DEPTH, D_MODEL, D_FF), D_MODEL ** -0.5),
        "ffn2_w_up": nrm(ks[25], (DEPTH, D_MODEL, D_FF), D_MODEL ** -0.5),
        "ffn2_w_down": nrm(ks[26], (DEPTH, D_FF, D_MODEL), D_FF ** -0.5),
    }


def reference(x, ffn1_norm, ffn1_w_gate, ffn1_w_up, ffn1_w_down, mix_norm, w_in,
              fox_f_bias, fox_q_gain, fox_k_gain, hgrn_lb, hgrn_o_gain,
              s5_lambda_re, s5_lambda_im, s5_log_step, s5_B_re, s5_B_im, s5_C_re, s5_C_im,
              s5_D, s5_glu_w, s5_glu_b, w_out,
              ffn2_norm, ffn2_w_gate, ffn2_w_up, ffn2_w_down):
    bsz, L, _ = x.shape
    lb_all = jnp.cumsum(jax.nn.softmax(hgrn_lb.astype(jnp.float32), axis=0), axis=0)
    lb_all = lb_all - lb_all[0:1]
    for l in range(DEPTH):
        h = rms_norm(x, ffn1_norm[l])
        x = x + 0.5 * swiglu(h, ffn1_w_gate[l], ffn1_w_up[l], ffn1_w_down[l])

        h = rms_norm(x, mix_norm[l])
        p = h @ w_in[l]
        a_q, a_k, a_v, a_f, b_q, b_f, b_i, b_g, c_u = jnp.split(p, IN_SPLITS, axis=-1)

        q = rms_norm(a_q.reshape(bsz, L, FOX_HEADS, FOX_HEAD_DIM), fox_q_gain[l])
        k = rms_norm(a_k.reshape(bsz, L, FOX_HEADS, FOX_HEAD_DIM), fox_k_gain[l])
        v = a_v.reshape(bsz, L, FOX_HEADS, FOX_HEAD_DIM)
        o_a = fox_attention(q, k, v, a_f + fox_f_bias[l]).reshape(bsz, L, FOX_WIDTH)

        lb = lb_all[l].reshape(HGRN_HEADS, HGRN_KEY_DIM)
        o_b = hgrn2(b_q.reshape(bsz, L, HGRN_HEADS, HGRN_KEY_DIM),
                    b_f.reshape(bsz, L, HGRN_HEADS, HGRN_KEY_DIM),
                    b_i.reshape(bsz, L, HGRN_HEADS, HGRN_VAL_DIM), lb)
        o_b = rms_norm(o_b, hgrn_o_gain[l].reshape(HGRN_HEADS, HGRN_VAL_DIM))
        o_b = (o_b * jax.nn.silu(b_g.astype(jnp.float32).reshape(bsz, L, HGRN_HEADS, HGRN_VAL_DIM))
               ).reshape(bsz, L, HGRN_WIDTH)

        o_c = s5_block(c_u, s5_lambda_re[l], s5_lambda_im[l], s5_log_step[l],
                       s5_B_re[l], s5_B_im[l], s5_C_re[l], s5_C_im[l],
                       s5_D[l], s5_glu_w[l], s5_glu_b[l])

        mix = jnp.concatenate([o_a.astype(x.dtype), o_b.astype(x.dtype), o_c.astype(x.dtype)], axis=-1)
        x = x + mix @ w_out[l]

        h = rms_norm(x, ffn2_norm[l])
        x = x + 0.5 * swiglu(h, ffn2_w_gate[l], ffn2_w_up[l], ffn2_w_down[l])
    return x
```

```python
import functools
import math

import jax
import jax.numpy as jnp
from jax import lax
from jax.experimental import pallas as pl
from jax.experimental.pallas import tpu as pltpu

F32 = jnp.float32
BF16 = jnp.bfloat16

EPS = 1e-6
D_MODEL = 1024
D_FF = 2816
FOX_HEADS = 4
FOX_HEAD_DIM = 64
FOX_WIDTH = FOX_HEADS * FOX_HEAD_DIM
HGRN_HEADS = 4
HGRN_DIM = 128
HGRN_WIDTH = HGRN_HEADS * HGRN_DIM
S5_GROUPS = 16
S5_GROUP = 16
S5_STATE = 64
S5_WIDTH = S5_GROUPS * S5_GROUP
S5_CHUNK = 16

V7X_VMEM_BYTES = 64 * 1024 * 1024
V7X_SUBLANES = 8

ATTN_BLOCK = 256
HGRN_CHUNK = 128
FFN_ROWS = 512
PROJ_ROWS = ATTN_BLOCK
OUT_ROWS = 512


def _vmem_limit(estimate_bytes):
    return int(min(estimate_bytes * 5 // 4 + (4 << 20), V7X_VMEM_BYTES - (4 << 20)))


def _rms_norm(x, gain):
    ms = jnp.mean(x * x, axis=-1, keepdims=True)
    return x * lax.rsqrt(ms + EPS) * gain


def _dot(a, b):
    return jnp.dot(a, b, preferred_element_type=F32)


def _dot_nt(a, b):
    return lax.dot_general(a, b, (((1,), (1,)), ((), ())), preferred_element_type=F32)


def _dot_tn(a, b):
    return lax.dot_general(a, b, (((0,), (0,)), ((), ())), preferred_element_type=F32)


def _split3(a):
    a1 = a.astype(BF16)
    r1 = a - a1.astype(F32)
    a2 = r1.astype(BF16)
    a3 = (r1 - a2.astype(F32)).astype(BF16)
    return a1, a2, a3


def _dot_exact_rhs(a, rhs):
    a1, a2, a3 = _split3(a)
    return _dot(a1, rhs) + _dot(a2, rhs) + _dot(a3, rhs)


def _dot_exact_lhs(lhs, a):
    a1, a2, a3 = _split3(a)
    return _dot(lhs, a1) + _dot(lhs, a2) + _dot(lhs, a3)


def _sigmoid(x):
    return 1.0 / (1.0 + jnp.exp(-x))


def _ffn_kernel(x_ref, g_ref, wg_ref, wu_ref, wd_ref, o_ref):
    x = x_ref[...]
    h = _rms_norm(x, g_ref[...]).astype(BF16)
    gate = _dot(h, wg_ref[...])
    up = _dot(h, wu_ref[...])
    act = (gate * _sigmoid(gate) * up).astype(BF16)
    o_ref[...] = x + 0.5 * _dot(act, wd_ref[...])


def _ffn(x2d, gain, wg, wu, wd):
    t, d = x2d.shape
    f = wg.shape[1]
    tm = min(FFN_ROWS, t)
    const = dict(pipeline_mode=pl.Buffered(1))
    est = 3 * d * f * 2 + 4 * tm * d * 4 + tm * f * (4 + 4 + 2 + 4) + tm * d * 4
    return pl.pallas_call(
        _ffn_kernel,
        grid=(t // tm,),
        in_specs=[
            pl.BlockSpec((tm, d), lambda i: (i, 0)),
            pl.BlockSpec((1, d), lambda i: (0, 0)),
            pl.BlockSpec((d, f), lambda i: (0, 0), **const),
            pl.BlockSpec((d, f), lambda i: (0, 0), **const),
            pl.BlockSpec((f, d), lambda i: (0, 0), **const),
        ],
        out_specs=pl.BlockSpec((tm, d), lambda i: (i, 0)),
        out_shape=jax.ShapeDtypeStruct((t, d), F32),
        compiler_params=pltpu.CompilerParams(
            dimension_semantics=("arbitrary",), vmem_limit_bytes=_vmem_limit(est)),
        name="ffn",
    )(x2d, gain, wg, wu, wd)


def _proj_kernel(x_ref, g_ref, wqk_ref, wvt_ref, wft_ref, wh_ref, wu_ref, qg_ref, kg_ref,
                 fb_ref, q_ref, k_ref, vt_ref, c_ref, hg_ref, u_ref, carry_ref):
    tm = x_ref.shape[0]

    @pl.when(pl.program_id(1) == 0)
    def _():
        carry_ref[...] = jnp.zeros_like(carry_ref)

    h = _rms_norm(x_ref[...], g_ref[...]).astype(BF16)

    li = lax.broadcasted_iota(jnp.int32, (FOX_WIDTH, FOX_WIDTH), 0) // FOX_HEAD_DIM
    lj = lax.broadcasted_iota(jnp.int32, (FOX_WIDTH, FOX_WIDTH), 1) // FOX_HEAD_DIM
    group = (li == lj).astype(BF16)

    def head_norm(a, gain):
        ms = _dot_exact_rhs(a * a, group) * (1.0 / FOX_HEAD_DIM)
        return a * lax.rsqrt(ms + EPS) * gain

    qk = _dot(h, wqk_ref[...])
    scale = 1.0 / math.sqrt(FOX_HEAD_DIM)
    q_ref[...] = (head_norm(qk[:, :FOX_WIDTH], qg_ref[...]) * scale).astype(BF16)
    k_ref[...] = head_norm(qk[:, FOX_WIDTH:], kg_ref[...]).astype(BF16)
    vt_ref[...] = _dot_nt(wvt_ref[...], h).astype(BF16)

    fl = _dot_nt(wft_ref[...], h) + fb_ref[...]
    ls = jnp.minimum(fl, 0.0) - jnp.log1p(jnp.exp(-jnp.abs(fl)))
    ri = lax.broadcasted_iota(jnp.int32, (tm, tm), 0)
    ci = lax.broadcasted_iota(jnp.int32, (tm, tm), 1)
    upper = (ri <= ci).astype(BF16)
    c = _dot_exact_rhs(ls, upper) + carry_ref[:, 0:1]
    c_ref[...] = c
    carry_ref[...] = jnp.broadcast_to(c[:, tm - 1:tm], carry_ref.shape)

    hg_ref[...] = _dot(h, wh_ref[...])
    u_ref[...] = _dot(h, wu_ref[...])


def _proj(x, gain, wqk, wvt, wft, wh, wu, qg, kg, fb):
    b, l, d = x.shape
    tm = min(PROJ_ROWS, l)
    nt = l // tm
    hw = wh.shape[1]
    full = lambda shape: pl.BlockSpec(shape, lambda bi, i: tuple(0 for _ in shape))
    est = 2 * (d * (512 + 256 + 8 + hw + 256) * 2) + 2 * tm * d * 4 + 2 * tm * (hw + 256 + 512) * 4 + tm * hw * 4
    return pl.pallas_call(
        _proj_kernel,
        grid=(b, nt),
        in_specs=[
            pl.BlockSpec((None, tm, d), lambda bi, i: (bi, i, 0)),
            full((1, d)), full(wqk.shape), full(wvt.shape), full(wft.shape), full(wh.shape),
            full(wu.shape), full((1, FOX_WIDTH)), full((1, FOX_WIDTH)), full((V7X_SUBLANES, 1)),
        ],
        out_specs=[
            pl.BlockSpec((None, tm, FOX_WIDTH), lambda bi, i: (bi, i, 0)),
            pl.BlockSpec((None, tm, FOX_WIDTH), lambda bi, i: (bi, i, 0)),
            pl.BlockSpec((None, None, FOX_WIDTH, tm), lambda bi, i: (bi, i, 0, 0)),
            pl.BlockSpec((None, V7X_SUBLANES, tm), lambda bi, i: (bi, 0, i)),
            pl.BlockSpec((None, tm, hw), lambda bi, i: (bi, i, 0)),
            pl.BlockSpec((None, tm, S5_WIDTH), lambda bi, i: (bi, i, 0)),
        ],
        out_shape=[
            jax.ShapeDtypeStruct((b, l, FOX_WIDTH), BF16),
            jax.ShapeDtypeStruct((b, l, FOX_WIDTH), BF16),
            jax.ShapeDtypeStruct((b, nt, FOX_WIDTH, tm), BF16),
            jax.ShapeDtypeStruct((b, V7X_SUBLANES, l), F32),
            jax.ShapeDtypeStruct((b, l, hw), F32),
            jax.ShapeDtypeStruct((b, l, S5_WIDTH), F32),
        ],
        scratch_shapes=[pltpu.VMEM((V7X_SUBLANES, 128), F32)],
        compiler_params=pltpu.CompilerParams(
            dimension_semantics=("arbitrary", "arbitrary"), vmem_limit_bytes=_vmem_limit(est)),
        name="proj",
    )(x, gain, wqk, wvt, wft, wh, wu, qg, kg, fb)


def _attn_kernel(q_ref, k_ref, vt_ref, cq_ref, ck_ref, o_ref):
    tq = q_ref.shape[0]
    tk = vt_ref.shape[2]
    i = pl.program_id(1)
    q = q_ref[...]
    lane_head = lax.broadcasted_iota(jnp.int32, q.shape, 1) // FOX_HEAD_DIM
    key_pos = lax.broadcasted_iota(jnp.int32, (tk, tq), 0)
    qry_pos = lax.broadcasted_iota(jnp.int32, (tk, tq), 1) + i * tq
    outs = []
    for hd in range(FOX_HEADS):
        qm = jnp.where(lane_head == hd, q, jnp.zeros_like(q))
        cq = cq_ref[hd:hd + 1, :]

        def body(j, carry, qm=qm, cq=cq, hd=hd):
            m, l, acc = carry
            k = k_ref[pl.ds(j * tk, tk), :]
            ck = ck_ref[pl.ds(j * tk, tk), hd:hd + 1]
            s = _dot_nt(k, qm) + cq - ck
            s = jnp.where(key_pos + j * tk <= qry_pos, s, -jnp.inf)
            m_new = jnp.maximum(m, jnp.max(s, axis=0, keepdims=True))
            alpha = jnp.exp(m - m_new)
            p = jnp.exp(s - m_new)
            l = alpha * l + jnp.sum(p, axis=0, keepdims=True)
            vt = vt_ref[j, hd * FOX_HEAD_DIM:(hd + 1) * FOX_HEAD_DIM, :]
            acc = alpha * acc + _dot(vt, p.astype(BF16))
            return m_new, l, acc

        init = (jnp.full((1, tq), -jnp.inf, F32), jnp.zeros((1, tq), F32),
                jnp.zeros((FOX_HEAD_DIM, tq), F32))
        _, l, acc = lax.fori_loop(0, i + 1, body, init)
        outs.append(acc / l)
    o_ref[...] = jnp.concatenate(outs, axis=0).T.astype(o_ref.dtype)


def _attn(q, k, vt, c_row, c_col):
    b, l, w = q.shape
    nkv, tk = vt.shape[1], vt.shape[3]
    tq = tk
    est = 2 * (l * w * 2 * 2 + l * 128 * 4 + 8 * l * 4) + 4 * tq * w * 4 + 8 * tq * tk * 4
    return pl.pallas_call(
        _attn_kernel,
        grid=(b, l // tq),
        in_specs=[
            pl.BlockSpec((None, tq, w), lambda bi, i: (bi, i, 0)),
            pl.BlockSpec((None, l, w), lambda bi, i: (bi, 0, 0)),
            pl.BlockSpec((None, nkv, w, tk), lambda bi, i: (bi, 0, 0, 0)),
            pl.BlockSpec((None, V7X_SUBLANES, tq), lambda bi, i: (bi, 0, i)),
            pl.BlockSpec((None, l, V7X_SUBLANES), lambda bi, i: (bi, 0, 0)),
        ],
        out_specs=pl.BlockSpec((None, tq, w), lambda bi, i: (bi, i, 0)),
        out_shape=jax.ShapeDtypeStruct((b, l, w), BF16),
        compiler_params=pltpu.CompilerParams(
            dimension_semantics=("arbitrary", "arbitrary"), vmem_limit_bytes=_vmem_limit(est)),
        name="fox_attn",
    )(q, k, vt, c_row, c_col)


def _hgrn_levels(c):
    return [c >> (s + 1) for s in range(int(math.log2(c)))]


def _hgrn_kernel(q_ref, f_ref, i_ref, g_ref, lb_ref, og_ref, o_ref, st_ref, b_ref):
    c = q_ref.shape[0]

    @pl.when(pl.program_id(1) == 0)
    def _():
        st_ref[...] = jnp.zeros_like(st_ref)

    lb = lb_ref[...]
    f = lb + (1.0 - lb) * _sigmoid(f_ref[...])
    key = 1.0 - f
    q = q_ref[...]
    ti = lax.broadcasted_iota(jnp.int32, (c, c), 0)
    si = lax.broadcasted_iota(jnp.int32, (c, c), 1)
    lower = (si <= ti).astype(BF16)
    b = _dot_exact_lhs(lower, jnp.log(f))
    b_ref[...] = b
    row = lax.broadcasted_iota(jnp.int32, b.shape, 0)

    def ref_rows(m):
        if 2 * m >= V7X_SUBLANES:
            parts = [jnp.broadcast_to(b_ref[g0 + m - 1:g0 + m, :], (2 * m, b.shape[1]))
                     for g0 in range(0, c, 2 * m)]
            return jnp.concatenate(parts, axis=0) if len(parts) > 1 else parts[0]
        off = row % (2 * m)
        out = b
        for o in range(2 * m):
            if o != m - 1:
                out = jnp.where(off == o, pltpu.roll(b, (o - (m - 1)) % c, axis=0), out)
        return out

    diff = ti ^ si
    scores = [jnp.zeros((c, c), F32) for _ in range(HGRN_HEADS)]
    for m in [0] + _hgrn_levels(c)[::-1]:
        if m == 0:
            z, y = q.astype(BF16), key.astype(BF16)
            mask = ti == si
        else:
            d = b - ref_rows(m)
            e = jnp.exp(jnp.minimum(d, -d))
            z, y = (q * e).astype(BF16), (key * e).astype(BF16)
            mask = (diff >= m) & (diff < 2 * m) & (si < ti)
        for hd in range(HGRN_HEADS):
            sl = slice(hd * HGRN_DIM, (hd + 1) * HGRN_DIM)
            scores[hd] = jnp.where(mask, _dot_nt(z[:, sl], y[:, sl]), scores[hd])

    b_last = b[c - 1:c, :]
    q_in = (q * jnp.exp(b)).astype(BF16)
    k_out = (key * jnp.exp(b_last - b)).astype(BF16)
    v = i_ref[...].astype(BF16)
    g = g_ref[...]
    gate = g * _sigmoid(g)
    og = og_ref[...]
    decay = jnp.exp(b_last)
    for hd in range(HGRN_HEADS):
        sl = slice(hd * HGRN_DIM, (hd + 1) * HGRN_DIM)
        st = st_ref[hd]
        o = _dot(scores[hd].astype(BF16), v[:, sl]) + _dot_nt(q_in[:, sl], st.astype(BF16))
        st_ref[hd] = st * decay[:, sl] + _dot_tn(v[:, sl], k_out[:, sl])
        o_ref[:, sl] = (_rms_norm(o, og[:, sl]) * gate[:, sl]).astype(o_ref.dtype)


def _hgrn(hg, lb, og):
    b, l, _ = hg.shape
    c = min(HGRN_CHUNK, l)
    w = HGRN_WIDTH
    part = lambda p: pl.BlockSpec((None, c, w), lambda bi, i, p=p: (bi, i, p))
    est = 2 * 5 * c * w * 4 + 16 * c * w * 4 + HGRN_HEADS * HGRN_DIM * HGRN_DIM * 4
    return pl.pallas_call(
        _hgrn_kernel,
        grid=(b, l // c),
        in_specs=[part(0), part(1), part(2), part(3),
                  pl.BlockSpec((1, w), lambda bi, i: (0, 0)),
                  pl.BlockSpec((1, w), lambda bi, i: (0, 0))],
        out_specs=pl.BlockSpec((None, c, w), lambda bi, i: (bi, i, 0)),
        out_shape=jax.ShapeDtypeStruct((b, l, w), BF16),
        scratch_shapes=[pltpu.VMEM((HGRN_HEADS, HGRN_DIM, HGRN_DIM), F32),
                        pltpu.VMEM((c, w), F32)],
        compiler_params=pltpu.CompilerParams(
            dimension_semantics=("arbitrary", "arbitrary"), vmem_limit_bytes=_vmem_limit(est)),
        name="hgrn2",
    )(hg, hg, hg, hg, lb, og)


def _s5_kernel(u_ref, mt_ref, pir_ref, pii_ref, por_ref, poi_ref, ar_ref, ai_ref, y_ref,
               wr_ref, wi_ref, xr_ref, xi_ref, *, rows_per_step):
    u = u_ref[...]
    wr_ref[...] = _dot(u, pir_ref[...])
    wi_ref[...] = _dot(u, pii_ref[...])
    ar = ar_ref[...]
    ai = ai_ref[...]
    nb = rows_per_step

    def step(cidx, carry):
        xr, xi = carry
        rows = pl.ds(pl.multiple_of(cidx * nb, nb), nb)
        xr_ref[rows, :] = xr
        xi_ref[rows, :] = xi
        return (ar * xr - ai * xi + wr_ref[rows, :], ar * xi + ai * xr + wi_ref[rows, :])

    zero = jnp.zeros((nb, S5_STATE), F32)
    lax.fori_loop(0, u.shape[0] // nb, step, (zero, zero))
    y_ref[...] = (_dot(u, mt_ref[...]) + _dot(xr_ref[...].astype(BF16), por_ref[...])
                  + _dot(xi_ref[...].astype(BF16), poi_ref[...]))


def _s5(u_flat, mt, pir, pii, por, poi, ar, ai, rows_per_step):
    g, n, w = u_flat.shape
    per_g = lambda shape: pl.BlockSpec((None,) + shape, lambda gi: (gi,) + tuple(0 for _ in shape))
    est = 2 * n * w * (2 + 4) + 4 * n * 128 * 4 + 4 * w * w * 2
    return pl.pallas_call(
        functools.partial(_s5_kernel, rows_per_step=rows_per_step),
        grid=(g,),
        in_specs=[per_g((n, w)), per_g((w, w)), per_g((w, S5_STATE)), per_g((w, S5_STATE)),
                  per_g((S5_STATE, w)), per_g((S5_STATE, w)), per_g((1, S5_STATE)), per_g((1, S5_STATE))],
        out_specs=per_g((n, w)),
        out_shape=jax.ShapeDtypeStruct((g, n, w), F32),
        scratch_shapes=[pltpu.VMEM((n, S5_STATE), F32) for _ in range(4)],
        compiler_params=pltpu.CompilerParams(
            dimension_semantics=("arbitrary",), vmem_limit_bytes=_vmem_limit(est)),
        name="s5",
    )(u_flat, mt, pir, pii, por, poi, ar, ai)


def _s5_operators(lam_re, lam_im, log_step, b_re, b_im, c_re, c_im):
    lr, li = lam_re.astype(F32), lam_im.astype(F32)
    dt = jnp.exp(log_step.astype(F32))[:, None]
    mag = jnp.exp(lr * dt)
    lbar_re, lbar_im = mag * jnp.cos(li * dt), mag * jnp.sin(li * dt)
    nr, ni = lbar_re - 1.0, lbar_im
    den = lr * lr + li * li
    coef_re = (nr * lr + ni * li) / den
    coef_im = (ni * lr - nr * li) / den
    br, bi = b_re.astype(F32), b_im.astype(F32)
    bbar_re = coef_re[..., None] * br - coef_im[..., None] * bi
    bbar_im = coef_re[..., None] * bi + coef_im[..., None] * br
    cr, ci = c_re.astype(F32), c_im.astype(F32)

    pr, pi = [jnp.ones_like(lbar_re)], [jnp.zeros_like(lbar_re)]
    for _ in range(S5_CHUNK):
        pr, pi = (pr + [pr[-1] * lbar_re - pi[-1] * lbar_im],
                  pi + [pr[-1] * lbar_im + pi[-1] * lbar_re])
    pw_re, pw_im = jnp.stack(pr), jnp.stack(pi)

    ab_re = pw_re[:S5_CHUNK, :, :, None] * bbar_re - pw_im[:S5_CHUNK, :, :, None] * bbar_im
    ab_im = pw_re[:S5_CHUNK, :, :, None] * bbar_im + pw_im[:S5_CHUNK, :, :, None] * bbar_re
    kd = jnp.sum(cr[None, :, :, :, None] * ab_re[:, :, None] - ci[None, :, :, :, None] * ab_im[:, :, None],
                 axis=3)
    t_idx = jnp.arange(S5_CHUNK)
    lag = t_idx[None, :] - t_idx[:, None]
    kt = jnp.where((lag >= 0)[:, :, None, None, None], kd[jnp.clip(lag, 0)], 0.0)
    mt = kt.transpose(2, 0, 4, 1, 3).reshape(S5_GROUPS, S5_CHUNK * S5_GROUP, S5_CHUNK * S5_GROUP)

    rev = S5_CHUNK - 1 - t_idx
    pin_re = ab_re[rev].transpose(1, 0, 3, 2).reshape(S5_GROUPS, S5_CHUNK * S5_GROUP, S5_STATE)
    pin_im = ab_im[rev].transpose(1, 0, 3, 2).reshape(S5_GROUPS, S5_CHUNK * S5_GROUP, S5_STATE)
    qr, qi = pw_re[1:], pw_im[1:]
    po_re = cr[None] * qr[:, :, None, :] - ci[None] * qi[:, :, None, :]
    po_im = -(cr[None] * qi[:, :, None, :] + ci[None] * qr[:, :, None, :])
    po_re = po_re.transpose(1, 3, 0, 2).reshape(S5_GROUPS, S5_STATE, S5_CHUNK * S5_GROUP)
    po_im = po_im.transpose(1, 3, 0, 2).reshape(S5_GROUPS, S5_STATE, S5_CHUNK * S5_GROUP)
    a_re = pw_re[S5_CHUNK][:, None, :]
    a_im = pw_im[S5_CHUNK][:, None, :]
    return (mt.astype(BF16), pin_re.astype(BF16), pin_im.astype(BF16),
            po_re.astype(BF16), po_im.astype(BF16), a_re, a_im)


def _out_kernel(x_ref, oa_ref, ob_ref, y_ref, u_ref, d_ref, gw_ref, gb_ref, wa_ref, wb_ref, wc_ref, o_ref):
    y = y_ref[...] + d_ref[...] * u_ref[...]
    z = 0.5 * y * (1.0 + jnp.tanh(math.sqrt(2.0 / math.pi) * (y + 0.044715 * (y * y * y))))
    oc = z * _sigmoid(_dot(z.astype(BF16), gw_ref[...]) + gb_ref[...])
    o_ref[...] = (x_ref[...] + _dot(oa_ref[...], wa_ref[...]) + _dot(ob_ref[...], wb_ref[...])
                  + _dot(oc.astype(BF16), wc_ref[...]))


def _out(x2d, oa, ob, y, u, dvec, gw, gb, wa, wb, wc):
    t, d = x2d.shape
    tm = min(OUT_ROWS, t)
    rows = lambda w: pl.BlockSpec((tm, w), lambda i: (i, 0))
    full = lambda shape: pl.BlockSpec(shape, lambda i: tuple(0 for _ in shape))
    est = 2 * (2 * tm * d * 4 + tm * (256 * 2 + 512 * 2 + 256 * 8)) + 2 * 2 * (d * d + 256 * 256) + 4 * tm * d * 4
    return pl.pallas_call(
        _out_kernel,
        grid=(t // tm,),
        in_specs=[rows(d), rows(FOX_WIDTH), rows(HGRN_WIDTH), rows(S5_WIDTH), rows(S5_WIDTH),
                  full((1, S5_WIDTH)), full(gw.shape), full((1, S5_WIDTH)),
                  full(wa.shape), full(wb.shape), full(wc.shape)],
        out_specs=rows(d),
        out_shape=jax.ShapeDtypeStruct((t, d), F32),
        compiler_params=pltpu.CompilerParams(
            dimension_semantics=("arbitrary",), vmem_limit_bytes=_vmem_limit(est)),
        name="out_proj",
    )(x2d, oa, ob, y, u, dvec, gw, gb, wa, wb, wc)


def kernel(x, ffn1_norm, ffn1_w_gate, ffn1_w_up, ffn1_w_down, mix_norm, w_in, fox_f_bias, fox_q_gain, fox_k_gain, hgrn_lb, hgrn_o_gain, s5_lambda_re, s5_lambda_im, s5_log_step, s5_B_re, s5_B_im, s5_C_re, s5_C_im, s5_D, s5_glu_w, s5_glu_b, w_out, ffn2_norm, ffn2_w_gate, ffn2_w_up, ffn2_w_down):
    bsz, seq, d = x.shape
    depth = w_in.shape[0]
    t = bsz * seq
    nchunk = seq // S5_CHUNK

    lb_all = jnp.cumsum(jax.nn.softmax(hgrn_lb.astype(F32), axis=0), axis=0)
    lb_all = lb_all - lb_all[0:1]

    o_q, o_k, o_v = 0, FOX_WIDTH, 2 * FOX_WIDTH
    o_f = 3 * FOX_WIDTH
    o_h = o_f + FOX_HEADS
    o_u = o_h + 4 * HGRN_WIDTH

    for l in range(depth):
        x = _ffn(x.reshape(t, d), ffn1_norm[l][None], ffn1_w_gate[l].astype(BF16),
                 ffn1_w_up[l].astype(BF16), ffn1_w_down[l].astype(BF16)).reshape(bsz, seq, d)

        w = w_in[l]
        wft = jnp.zeros((V7X_SUBLANES, d), F32).at[:FOX_HEADS].set(w[:, o_f:o_h].T).astype(BF16)
        fb = jnp.zeros((V7X_SUBLANES, 1), F32).at[:FOX_HEADS, 0].set(fox_f_bias[l].astype(F32))
        q, k, vt, c_row, hg, u = _proj(
            x, mix_norm[l][None], w[:, o_q:o_v].astype(BF16), w[:, o_v:o_f].T.astype(BF16), wft,
            w[:, o_h:o_u].astype(BF16), w[:, o_u:].astype(BF16),
            jnp.tile(fox_q_gain[l].astype(F32), FOX_HEADS)[None],
            jnp.tile(fox_k_gain[l].astype(F32), FOX_HEADS)[None], fb)

        o_a = _attn(q, k, vt, c_row, jnp.swapaxes(c_row, 1, 2))
        o_b = _hgrn(hg, lb_all[l][None], hgrn_o_gain[l].astype(F32)[None])

        u_flat = (u.astype(BF16).reshape(bsz, nchunk, S5_CHUNK, S5_GROUPS, S5_GROUP)
                  .transpose(3, 1, 0, 2, 4).reshape(S5_GROUPS, nchunk * bsz, S5_CHUNK * S5_GROUP))
        ops = _s5_operators(s5_lambda_re[l], s5_lambda_im[l], s5_log_step[l],
                            s5_B_re[l], s5_B_im[l], s5_C_re[l], s5_C_im[l])
        y_flat = _s5(u_flat, *ops, rows_per_step=bsz)
        y = (y_flat.reshape(S5_GROUPS, nchunk, bsz, S5_CHUNK, S5_GROUP)
             .transpose(2, 1, 3, 0, 4).reshape(t, S5_WIDTH))

        wo = w_out[l].astype(BF16)
        x = _out(x.reshape(t, d), o_a.reshape(t, FOX_WIDTH), o_b.reshape(t, HGRN_WIDTH), y,
                 u.reshape(t, S5_WIDTH), s5_D[l].astype(F32)[None], s5_glu_w[l].astype(BF16),
                 s5_glu_b[l].astype(F32)[None], wo[:FOX_WIDTH], wo[FOX_WIDTH:FOX_WIDTH + HGRN_WIDTH],
                 wo[FOX_WIDTH + HGRN_WIDTH:])

        x = _ffn(x, ffn2_norm[l][None], ffn2_w_gate[l].astype(BF16),
                 ffn2_w_up[l].astype(BF16), ffn2_w_down[l].astype(BF16)).reshape(bsz, seq, d)
    return x
```

```python
import functools
import math

import jax
import jax.numpy as jnp
from jax import lax
from jax.experimental import pallas as pl
from jax.experimental.pallas import tpu as pltpu

F32 = jnp.float32
BF16 = jnp.bfloat16

EPS = 1e-6
D_MODEL = 1024
D_FF = 2816
FOX_HEADS = 4
FOX_HEAD_DIM = 64
FOX_WIDTH = FOX_HEADS * FOX_HEAD_DIM
HGRN_HEADS = 4
HGRN_DIM = 128
HGRN_WIDTH = HGRN_HEADS * HGRN_DIM
S5_GROUPS = 16
S5_GROUP = 16
S5_STATE = 64
S5_WIDTH = S5_GROUPS * S5_GROUP
S5_NSTATE = S5_GROUPS * S5_STATE
S5_CHUNK = 8
S5_TILE = 128
LOG2E = math.log2(math.e)

V7X_VMEM_BYTES = 64 * 1024 * 1024
V7X_SUBLANES = 8

ATTN_BLOCK = 256
HGRN_CHUNK = 128
FFN_ROWS = 512
PROJ_ROWS = 512
OUT_ROWS = 512


def _vmem_limit(estimate_bytes):
    return int(min(estimate_bytes * 5 // 4 + (4 << 20), V7X_VMEM_BYTES - (4 << 20)))


def _rms_norm(x, gain):
    ms = jnp.mean(x * x, axis=-1, keepdims=True)
    return x * lax.rsqrt(ms + EPS) * gain


def _dot(a, b):
    return jnp.dot(a, b, preferred_element_type=F32)


def _dot_nt(a, b):
    return lax.dot_general(a, b, (((1,), (1,)), ((), ())), preferred_element_type=F32)


def _dot_tn(a, b):
    return lax.dot_general(a, b, (((0,), (0,)), ((), ())), preferred_element_type=F32)


def _split3(a):
    a1 = a.astype(BF16)
    r1 = a - a1.astype(F32)
    a2 = r1.astype(BF16)
    a3 = (r1 - a2.astype(F32)).astype(BF16)
    return a1, a2, a3


def _dot_exact_rhs(a, rhs):
    a1, a2, a3 = _split3(a)
    return _dot(a1, rhs) + _dot(a2, rhs) + _dot(a3, rhs)


def _dot_exact_lhs(lhs, a):
    a1, a2, a3 = _split3(a)
    return _dot(lhs, a1) + _dot(lhs, a2) + _dot(lhs, a3)


def _sigmoid(x):
    return 1.0 / (1.0 + jnp.exp(-x))


def _ffn_kernel(x_ref, g_ref, wg_ref, wu_ref, wd_ref, o_ref):
    x = x_ref[...]
    h = _rms_norm(x, g_ref[...]).astype(BF16)
    gate = _dot(h, wg_ref[...])
    up = _dot(h, wu_ref[...])
    act = (gate * _sigmoid(gate) * up).astype(BF16)
    o_ref[...] = x + 0.5 * _dot(act, wd_ref[...])


def _ffn(x2d, gain, wg, wu, wd):
    t, d = x2d.shape
    f = wg.shape[1]
    tm = min(FFN_ROWS, t)
    const = dict(pipeline_mode=pl.Buffered(1))
    est = 3 * d * f * 2 + 4 * tm * d * 4 + tm * f * (4 + 4 + 2 + 4) + tm * d * 4
    return pl.pallas_call(
        _ffn_kernel,
        grid=(t // tm,),
        in_specs=[
            pl.BlockSpec((tm, d), lambda i: (i, 0)),
            pl.BlockSpec((1, d), lambda i: (0, 0)),
            pl.BlockSpec((d, f), lambda i: (0, 0), **const),
            pl.BlockSpec((d, f), lambda i: (0, 0), **const),
            pl.BlockSpec((f, d), lambda i: (0, 0), **const),
        ],
        out_specs=pl.BlockSpec((tm, d), lambda i: (i, 0)),
        out_shape=jax.ShapeDtypeStruct((t, d), F32),
        compiler_params=pltpu.CompilerParams(
            dimension_semantics=("arbitrary",), vmem_limit_bytes=_vmem_limit(est)),
        name="ffn",
    )(x2d, gain, wg, wu, wd)


def _proj_kernel(x_ref, g_ref, wqk_ref, wvt_ref, wf_ref, wh_ref, wu_ref, qg_ref, kg_ref,
                 fb_ref, q_ref, k_ref, vt_ref, cs_ref, hg_ref, u_ref, carry_ref):
    tm = x_ref.shape[0]
    tk = vt_ref.shape[2]

    @pl.when(pl.program_id(1) == 0)
    def _():
        carry_ref[...] = jnp.zeros_like(carry_ref)

    h = _rms_norm(x_ref[...], g_ref[...]).astype(BF16)

    li = lax.broadcasted_iota(jnp.int32, (FOX_WIDTH, FOX_WIDTH), 0) // FOX_HEAD_DIM
    lj = lax.broadcasted_iota(jnp.int32, (FOX_WIDTH, FOX_WIDTH), 1) // FOX_HEAD_DIM
    group = (li == lj).astype(BF16)

    def head_norm(a, gain):
        ms = _dot_exact_rhs(a * a, group) * (1.0 / FOX_HEAD_DIM)
        return a * lax.rsqrt(ms + EPS) * gain

    qk = _dot(h, wqk_ref[...])
    q_ref[...] = (head_norm(qk[:, :FOX_WIDTH], qg_ref[...]) * (LOG2E / math.sqrt(FOX_HEAD_DIM))).astype(BF16)
    k_ref[...] = head_norm(qk[:, FOX_WIDTH:], kg_ref[...]).astype(BF16)
    vt = _dot_nt(wvt_ref[...], h).astype(BF16)
    for blk in range(tm // tk):
        vt_ref[blk] = vt[:, blk * tk:(blk + 1) * tk]

    fl = _dot(h, wf_ref[...]) + fb_ref[...]
    ls = jnp.minimum(fl, 0.0) - jnp.log1p(jnp.exp(-jnp.abs(fl)))
    ri = lax.broadcasted_iota(jnp.int32, (tm, tm), 0)
    ci = lax.broadcasted_iota(jnp.int32, (tm, tm), 1)
    lower = (ci <= ri).astype(BF16)
    c = _dot_exact_lhs(lower, ls) + carry_ref[0:1, :]
    carry_ref[...] = jnp.broadcast_to(c[tm - 1:tm, :], carry_ref.shape)
    c1, c2, c3 = _split3(c * LOG2E)
    lane = lax.broadcasted_iota(jnp.int32, c.shape, 1)
    cs_ref[...] = jnp.where(lane < FOX_HEADS, c1, jnp.where(lane < 2 * FOX_HEADS, c2, c3))

    hg_ref[...] = _dot(h, wh_ref[...])
    u_ref[...] = _dot(h, wu_ref[...])


def _proj(x, gain, wqk, wvt, wf, wh, wu, qg, kg, fb):
    b, l, d = x.shape
    tm = min(PROJ_ROWS, l)
    tk = min(ATTN_BLOCK, l)
    hw = wh.shape[1]
    full = lambda shape: pl.BlockSpec(shape, lambda bi, i: tuple(0 for _ in shape))
    est = 2 * (d * (512 + 256 + 128 + hw + 256) * 2) + 2 * tm * d * 4 + 2 * tm * (hw + 256 + 512) * 4 + tm * hw * 4
    return pl.pallas_call(
        _proj_kernel,
        grid=(b, l // tm),
        in_specs=[
            pl.BlockSpec((None, tm, d), lambda bi, i: (bi, i, 0)),
            full((1, d)), full(wqk.shape), full(wvt.shape), full(wf.shape), full(wh.shape),
            full(wu.shape), full((1, FOX_WIDTH)), full((1, FOX_WIDTH)), full((1, 128)),
        ],
        out_specs=[
            pl.BlockSpec((None, tm, FOX_WIDTH), lambda bi, i: (bi, i, 0)),
            pl.BlockSpec((None, tm, FOX_WIDTH), lambda bi, i: (bi, i, 0)),
            pl.BlockSpec((None, tm // tk, FOX_WIDTH, tk), lambda bi, i: (bi, i, 0, 0)),
            pl.BlockSpec((None, tm, 128), lambda bi, i: (bi, i, 0)),
            pl.BlockSpec((None, tm, hw), lambda bi, i: (bi, i, 0)),
            pl.BlockSpec((tm, S5_WIDTH), lambda bi, i: (i, bi)),
        ],
        out_shape=[
            jax.ShapeDtypeStruct((b, l, FOX_WIDTH), BF16),
            jax.ShapeDtypeStruct((b, l, FOX_WIDTH), BF16),
            jax.ShapeDtypeStruct((b, l // tk, FOX_WIDTH, tk), BF16),
            jax.ShapeDtypeStruct((b, l, 128), BF16),
            jax.ShapeDtypeStruct((b, l, hw), F32),
            jax.ShapeDtypeStruct((l, b * S5_WIDTH), F32),
        ],
        scratch_shapes=[pltpu.VMEM((V7X_SUBLANES, 128), F32)],
        compiler_params=pltpu.CompilerParams(
            dimension_semantics=("arbitrary", "arbitrary"), vmem_limit_bytes=_vmem_limit(est)),
        name="proj",
    )(x, gain, wqk, wvt, wf, wh, wu, qg, kg, fb)


def _attn_kernel(q_ref, k_ref, vt_ref, csq_ref, csk_ref, e_ref, pq_ref, pk_ref, oq_ref, ok_ref,
                 o_ref, ka_ref, qa_ref, m_ref, l_ref, acc_ref, s_ref, p_ref):
    tq = q_ref.shape[0]
    tk = vt_ref.shape[2]
    i = pl.program_id(1)

    @pl.when(i == 0)
    def _():
        ka_ref[...] = (_dot(k_ref[...], e_ref[...]) + _dot(csk_ref[...], pk_ref[...])
                       + ok_ref[...]).astype(BF16)

    qa_ref[...] = (_dot(q_ref[...], e_ref[...]) + _dot(csq_ref[...], pq_ref[...])
                   + oq_ref[...]).astype(BF16)
    m_ref[...] = jnp.full(m_ref.shape, -jnp.inf, F32)
    l_ref[...] = jnp.zeros_like(l_ref)
    acc_ref[...] = jnp.zeros_like(acc_ref)
    causal = (lax.broadcasted_iota(jnp.int32, (tk, tq), 0)
              <= lax.broadcasted_iota(jnp.int32, (tk, tq), 1))

    def step(j, diagonal):
        rows = pl.ds(pl.multiple_of(j * tk, tk), tk)
        for hd in range(FOX_HEADS):
            slab = slice(hd * 128, (hd + 1) * 128)
            s = _dot_nt(ka_ref[rows, slab], qa_ref[:, slab])
            s_ref[hd] = jnp.where(causal, s, -jnp.inf) if diagonal else s
        alphas = []
        for hd in range(FOX_HEADS):
            s = s_ref[hd]
            m_old = m_ref[hd]
            m_new = jnp.maximum(m_old, jnp.max(s, axis=0, keepdims=True))
            alpha = jnp.exp2(m_old - m_new)
            p = jnp.exp2(s - m_new)
            m_ref[hd] = m_new
            l_ref[hd] = alpha * l_ref[hd] + jnp.sum(p, axis=0, keepdims=True)
            p_ref[hd] = p.astype(BF16)
            alphas.append(alpha)
        for hd in range(FOX_HEADS):
            vt = vt_ref[j, hd * FOX_HEAD_DIM:(hd + 1) * FOX_HEAD_DIM, :]
            acc_ref[hd] = alphas[hd] * acc_ref[hd] + _dot(vt, p_ref[hd])

    def body(j, carry):
        step(j, False)
        return carry

    lax.fori_loop(0, i, body, 0)
    step(i, True)
    outs = [acc_ref[hd] / l_ref[hd] for hd in range(FOX_HEADS)]
    o_ref[...] = jnp.concatenate(outs, axis=0).T.astype(o_ref.dtype)


def _attn_placement():
    w = FOX_HEADS * 128
    e = jnp.zeros((FOX_WIDTH, w), F32)
    cols = jnp.arange(FOX_WIDTH)
    e = e.at[cols, (cols // FOX_HEAD_DIM) * 128 + cols % FOX_HEAD_DIM].set(1.0)
    pq, pk = jnp.zeros((128, w), F32), jnp.zeros((128, w), F32)
    oq, ok = jnp.zeros((1, w), F32), jnp.zeros((1, w), F32)
    for hd in range(FOX_HEADS):
        base = hd * 128 + FOX_HEAD_DIM
        for term in range(3):
            pk = pk.at[term * FOX_HEADS + hd, base + term].set(-1.0)
            oq = oq.at[0, base + term].set(1.0)
            pq = pq.at[term * FOX_HEADS + hd, base + 3 + term].set(1.0)
            ok = ok.at[0, base + 3 + term].set(1.0)
    return e.astype(BF16), pq.astype(BF16), pk.astype(BF16), oq, ok


def _attn(q, k, vt, cs):
    b, l, w = q.shape
    nkv, tk = vt.shape[1], vt.shape[3]
    tq = tk
    wa = FOX_HEADS * 128
    e, pq, pk, oq, ok = _attn_placement()
    full = lambda shape: pl.BlockSpec(shape, lambda bi, i: tuple(0 for _ in shape))
    est = (2 * (l * w * 2 * 2 + l * 128 * 2) + l * wa * (2 + 4 + 4) + 4 * tq * wa * 4
           + 12 * tq * tk * 4 + FOX_HEADS * 128 * tq * 4)
    return pl.pallas_call(
        _attn_kernel,
        grid=(b, l // tq),
        in_specs=[
            pl.BlockSpec((None, tq, w), lambda bi, i: (bi, i, 0)),
            pl.BlockSpec((None, l, w), lambda bi, i: (bi, 0, 0)),
            pl.BlockSpec((None, nkv, w, tk), lambda bi, i: (bi, 0, 0, 0)),
            pl.BlockSpec((None, tq, 128), lambda bi, i: (bi, i, 0)),
            pl.BlockSpec((None, l, 128), lambda bi, i: (bi, 0, 0)),
            full(e.shape), full(pq.shape), full(pk.shape), full(oq.shape), full(ok.shape),
        ],
        out_specs=pl.BlockSpec((None, tq, w), lambda bi, i: (bi, i, 0)),
        out_shape=jax.ShapeDtypeStruct((b, l, w), BF16),
        scratch_shapes=[pltpu.VMEM((l, wa), BF16), pltpu.VMEM((tq, wa), BF16),
                        pltpu.VMEM((FOX_HEADS, 1, tq), F32), pltpu.VMEM((FOX_HEADS, 1, tq), F32),
                        pltpu.VMEM((FOX_HEADS, FOX_HEAD_DIM, tq), F32),
                        pltpu.VMEM((FOX_HEADS, tk, tq), F32), pltpu.VMEM((FOX_HEADS, tk, tq), BF16)],
        compiler_params=pltpu.CompilerParams(
            dimension_semantics=("arbitrary", "arbitrary"), vmem_limit_bytes=_vmem_limit(est)),
        name="fox_attn",
    )(q, k, vt, cs, cs, e, pq, pk, oq, ok)


def _hgrn_levels(c):
    return [c >> (s + 1) for s in range(int(math.log2(c)))]


def _hgrn_kernel(q_ref, f_ref, i_ref, g_ref, lb_ref, og_ref, o_ref, st_ref, b_ref):
    c = q_ref.shape[0]

    @pl.when(pl.program_id(1) == 0)
    def _():
        st_ref[...] = jnp.zeros_like(st_ref)

    lb = lb_ref[...]
    f = lb + (1.0 - lb) * _sigmoid(f_ref[...])
    key = 1.0 - f
    q = q_ref[...]
    ti = lax.broadcasted_iota(jnp.int32, (c, c), 0)
    si = lax.broadcasted_iota(jnp.int32, (c, c), 1)
    lower = (si <= ti).astype(BF16)
    b = _dot_exact_lhs(lower, jnp.log(f))
    b_ref[...] = b
    row = lax.broadcasted_iota(jnp.int32, b.shape, 0)

    def ref_rows(m):
        if 2 * m >= V7X_SUBLANES:
            parts = [jnp.broadcast_to(b_ref[g0 + m - 1:g0 + m, :], (2 * m, b.shape[1]))
                     for g0 in range(0, c, 2 * m)]
            return jnp.concatenate(parts, axis=0) if len(parts) > 1 else parts[0]
        off = row % (2 * m)
        out = b
        for o in range(2 * m):
            if o != m - 1:
                out = jnp.where(off == o, pltpu.roll(b, (o - (m - 1)) % c, axis=0), out)
        return out

    diff = ti ^ si
    scores = [jnp.zeros((c, c), F32) for _ in range(HGRN_HEADS)]
    for m in [0] + _hgrn_levels(c)[::-1]:
        if m == 0:
            z, y = q.astype(BF16), key.astype(BF16)
            mask = ti == si
        else:
            d = b - ref_rows(m)
            e = jnp.exp(jnp.minimum(d, -d))
            z, y = (q * e).astype(BF16), (key * e).astype(BF16)
            mask = (diff >= m) & (diff < 2 * m) & (si < ti)
        for hd in range(HGRN_HEADS):
            sl = slice(hd * HGRN_DIM, (hd + 1) * HGRN_DIM)
            scores[hd] = jnp.where(mask, _dot_nt(z[:, sl], y[:, sl]), scores[hd])

    b_last = b[c - 1:c, :]
    q_in = (q * jnp.exp(b)).astype(BF16)
    k_out = (key * jnp.exp(b_last - b)).astype(BF16)
    v = i_ref[...].astype(BF16)
    g = g_ref[...]
    gate = g * _sigmoid(g)
    og = og_ref[...]
    decay = jnp.exp(b_last)
    for hd in range(HGRN_HEADS):
        sl = slice(hd * HGRN_DIM, (hd + 1) * HGRN_DIM)
        st = st_ref[hd]
        o = _dot(scores[hd].astype(BF16), v[:, sl]) + _dot_nt(q_in[:, sl], st.astype(BF16))
        st_ref[hd] = st * decay[:, sl] + _dot_tn(v[:, sl], k_out[:, sl])
        o_ref[:, sl] = (_rms_norm(o, og[:, sl]) * gate[:, sl]).astype(o_ref.dtype)


def _hgrn(hg, lb, og):
    b, l, _ = hg.shape
    c = min(HGRN_CHUNK, l)
    w = HGRN_WIDTH
    part = lambda p: pl.BlockSpec((None, c, w), lambda bi, i, p=p: (bi, i, p))
    est = 2 * 5 * c * w * 4 + 16 * c * w * 4 + HGRN_HEADS * HGRN_DIM * HGRN_DIM * 4
    return pl.pallas_call(
        _hgrn_kernel,
        grid=(b, l // c),
        in_specs=[part(0), part(1), part(2), part(3),
                  pl.BlockSpec((1, w), lambda bi, i: (0, 0)),
                  pl.BlockSpec((1, w), lambda bi, i: (0, 0))],
        out_specs=pl.BlockSpec((None, c, w), lambda bi, i: (bi, i, 0)),
        out_shape=jax.ShapeDtypeStruct((b, l, w), BF16),
        scratch_shapes=[pltpu.VMEM((HGRN_HEADS, HGRN_DIM, HGRN_DIM), F32),
                        pltpu.VMEM((c, w), F32)],
        compiler_params=pltpu.CompilerParams(
            dimension_semantics=("arbitrary", "arbitrary"), vmem_limit_bytes=_vmem_limit(est)),
        name="hgrn2",
    )(hg, hg, hg, hg, lb, og)


def _s5_kernel(u_ref, kd_ref, pin_ref, pout_ref, ar_ref, ai_ref, y_ref, xr_ref, xi_ref):
    nc, cs, nb, w = u_ref.shape
    ns = ar_ref.shape[1]

    @pl.when(pl.program_id(0) == 0)
    def _():
        xr_ref[...] = jnp.zeros_like(xr_ref)
        xi_ref[...] = jnp.zeros_like(xi_ref)

    us = [u_ref[:, s, :, :].reshape(nc * nb, w).astype(BF16) for s in range(cs)]
    win = _dot(us[0], pin_ref[0])
    for s in range(1, cs):
        win = win + _dot(us[s], pin_ref[s])

    ar, ai = ar_ref[...], ai_ref[...]
    xr, xi = xr_ref[...], xi_ref[...]
    prev_r, prev_i = [], []
    for c in range(nc):
        prev_r.append(xr)
        prev_i.append(xi)
        wr = win[c * nb:(c + 1) * nb, :ns]
        wi = win[c * nb:(c + 1) * nb, ns:]
        xr, xi = ar * xr - ai * xi + wr, ar * xi + ai * xr + wi
    xr_ref[...] = xr
    xi_ref[...] = xi
    xp = jnp.concatenate([jnp.concatenate(prev_r, axis=0), jnp.concatenate(prev_i, axis=0)],
                         axis=1).astype(BF16)

    for t in range(cs):
        lags = jnp.concatenate([us[t - d] for d in range(t + 1)], axis=1) if t else us[0]
        yt = _dot(xp, pout_ref[t]) + _dot(lags, kd_ref[0:(t + 1) * w, :])
        y_ref[:, t, :, :] = yt.reshape(nc, nb, w)


def _s5(u_lb, kd, pin, pout, ar, ai):
    l, nb, w = u_lb.shape
    tl = min(S5_TILE, l)
    cs = S5_CHUNK
    const = dict(pipeline_mode=pl.Buffered(1))
    full = lambda shape: pl.BlockSpec(shape, lambda i: tuple(0 for _ in shape), **const)
    rows = tl * nb
    chunked = pl.BlockSpec((tl // cs, cs, nb, w), lambda i: (i, 0, 0, 0))
    est = (4 * rows * w * 4 + (kd.size + pin.size + pout.size) * 2 + rows // S5_CHUNK * 2 * S5_NSTATE * (4 + 4 + 2)
           + rows * w * (2 + 2 + 4))
    return pl.pallas_call(
        _s5_kernel,
        grid=(l // tl,),
        in_specs=[chunked, full(kd.shape), full(pin.shape), full(pout.shape), full(ar.shape), full(ai.shape)],
        out_specs=chunked,
        out_shape=jax.ShapeDtypeStruct((l // cs, cs, nb, w), F32),
        scratch_shapes=[pltpu.VMEM((nb, S5_NSTATE), F32), pltpu.VMEM((nb, S5_NSTATE), F32)],
        compiler_params=pltpu.CompilerParams(
            dimension_semantics=("arbitrary",), vmem_limit_bytes=_vmem_limit(est)),
        name="s5",
    )(u_lb.reshape(l // cs, cs, nb, w), kd, pin, pout, ar, ai).reshape(l, nb, w)


def _s5_operators(lam_re, lam_im, log_step, b_re, b_im, c_re, c_im):
    lr, li = lam_re.astype(F32), lam_im.astype(F32)
    dt = jnp.exp(log_step.astype(F32))[:, None]
    mag = jnp.exp(lr * dt)
    lbar_re, lbar_im = mag * jnp.cos(li * dt), mag * jnp.sin(li * dt)
    nr, ni = lbar_re - 1.0, lbar_im
    den = lr * lr + li * li
    coef_re = (nr * lr + ni * li) / den
    coef_im = (ni * lr - nr * li) / den
    br, bi = b_re.astype(F32), b_im.astype(F32)
    bbar_re = coef_re[..., None] * br - coef_im[..., None] * bi
    bbar_im = coef_re[..., None] * bi + coef_im[..., None] * br
    cr, ci = c_re.astype(F32), c_im.astype(F32)

    pr, pi = [jnp.ones_like(lbar_re)], [jnp.zeros_like(lbar_re)]
    for _ in range(S5_CHUNK):
        pr, pi = (pr + [pr[-1] * lbar_re - pi[-1] * lbar_im],
                  pi + [pr[-1] * lbar_im + pi[-1] * lbar_re])
    pw_re, pw_im = jnp.stack(pr), jnp.stack(pi)
    cs = S5_CHUNK
    eye = jnp.eye(S5_GROUPS, dtype=F32)

    ab_re = pw_re[:cs, :, :, None] * bbar_re - pw_im[:cs, :, :, None] * bbar_im
    ab_im = pw_re[:cs, :, :, None] * bbar_im + pw_im[:cs, :, :, None] * bbar_re
    taps = jnp.sum(cr[None, :, :, :, None] * ab_re[:, :, None] - ci[None, :, :, :, None] * ab_im[:, :, None],
                   axis=3)
    kd = taps.transpose(0, 1, 3, 2)[:, :, :, None, :] * eye[None, :, None, :, None]
    kd = kd.reshape(cs * S5_WIDTH, S5_WIDTH)

    rev = cs - 1 - jnp.arange(cs)
    ab = jnp.stack([ab_re[rev], ab_im[rev]], axis=1)
    pin = ab.transpose(0, 2, 4, 1, 3)[:, :, :, :, None, :] * eye[None, :, None, None, :, None]
    pin = pin.reshape(cs, S5_WIDTH, 2 * S5_NSTATE)

    qr, qi = pw_re[1:], pw_im[1:]
    po_re = cr[None] * qr[:, :, None, :] - ci[None] * qi[:, :, None, :]
    po_im = -(cr[None] * qi[:, :, None, :] + ci[None] * qr[:, :, None, :])
    po = jnp.stack([po_re, po_im], axis=1)
    pout = po.transpose(0, 1, 2, 4, 3)[:, :, :, :, None, :] * eye[None, None, :, None, :, None]
    pout = pout.reshape(cs, 2 * S5_NSTATE, S5_WIDTH)
    a_re = pw_re[cs].reshape(1, S5_NSTATE)
    a_im = pw_im[cs].reshape(1, S5_NSTATE)
    return kd.astype(BF16), pin.astype(BF16), pout.astype(BF16), a_re, a_im


def _out_kernel(x_ref, oa_ref, ob_ref, y_ref, u_ref, d_ref, gw_ref, gb_ref, wa_ref, wb_ref, wc_ref, o_ref):
    y = y_ref[...] + d_ref[...] * u_ref[...]
    z = 0.5 * y * (1.0 + jnp.tanh(math.sqrt(2.0 / math.pi) * (y + 0.044715 * (y * y * y))))
    oc = z * _sigmoid(_dot(z.astype(BF16), gw_ref[...]) + gb_ref[...])
    o_ref[...] = (x_ref[...] + _dot(oa_ref[...], wa_ref[...]) + _dot(ob_ref[...], wb_ref[...])
                  + _dot(oc.astype(BF16), wc_ref[...]))


def _out(x, oa, ob, y_lb, u_lb, dvec, gw, gb, wa, wb, wc):
    b, l, d = x.shape
    tm = min(OUT_ROWS, l)
    rows = lambda w: pl.BlockSpec((None, tm, w), lambda bi, i: (bi, i, 0))
    token_major = pl.BlockSpec((tm, S5_WIDTH), lambda bi, i: (i, bi))
    full = lambda shape: pl.BlockSpec(shape, lambda bi, i: tuple(0 for _ in shape))
    est = 2 * (2 * tm * d * 4 + tm * (256 * 2 + 512 * 2 + 256 * 8)) + 2 * 2 * (d * d + 256 * 256) + 4 * tm * d * 4
    return pl.pallas_call(
        _out_kernel,
        grid=(b, l // tm),
        in_specs=[rows(d), rows(FOX_WIDTH), rows(HGRN_WIDTH), token_major, token_major,
                  full((1, S5_WIDTH)), full(gw.shape), full((1, S5_WIDTH)),
                  full(wa.shape), full(wb.shape), full(wc.shape)],
        out_specs=rows(d),
        out_shape=jax.ShapeDtypeStruct((b, l, d), F32),
        compiler_params=pltpu.CompilerParams(
            dimension_semantics=("arbitrary", "arbitrary"), vmem_limit_bytes=_vmem_limit(est)),
        name="out_proj",
    )(x, oa, ob, y_lb, u_lb, dvec, gw, gb, wa, wb, wc)


def kernel(x, ffn1_norm, ffn1_w_gate, ffn1_w_up, ffn1_w_down, mix_norm, w_in, fox_f_bias, fox_q_gain, fox_k_gain, hgrn_lb, hgrn_o_gain, s5_lambda_re, s5_lambda_im, s5_log_step, s5_B_re, s5_B_im, s5_C_re, s5_C_im, s5_D, s5_glu_w, s5_glu_b, w_out, ffn2_norm, ffn2_w_gate, ffn2_w_up, ffn2_w_down):
    bsz, seq, d = x.shape
    depth = w_in.shape[0]
    t = bsz * seq

    lb_all = jnp.cumsum(jax.nn.softmax(hgrn_lb.astype(F32), axis=0), axis=0)
    lb_all = lb_all - lb_all[0:1]

    o_q, o_k, o_v = 0, FOX_WIDTH, 2 * FOX_WIDTH
    o_f = 3 * FOX_WIDTH
    o_h = o_f + FOX_HEADS
    o_u = o_h + 4 * HGRN_WIDTH

    for l in range(depth):
        x = _ffn(x.reshape(t, d), ffn1_norm[l][None], ffn1_w_gate[l].astype(BF16),
                 ffn1_w_up[l].astype(BF16), ffn1_w_down[l].astype(BF16)).reshape(bsz, seq, d)

        w = w_in[l]
        wf = jnp.zeros((d, 128), F32).at[:, :3 * FOX_HEADS].set(jnp.tile(w[:, o_f:o_h], (1, 3))).astype(BF16)
        fb = jnp.zeros((1, 128), F32).at[0, :3 * FOX_HEADS].set(jnp.tile(fox_f_bias[l].astype(F32), 3))
        q, k, vt, cs, hg, u_lb = _proj(
            x, mix_norm[l][None], w[:, o_q:o_v].astype(BF16), w[:, o_v:o_f].T.astype(BF16), wf,
            w[:, o_h:o_u].astype(BF16), w[:, o_u:].astype(BF16),
            jnp.tile(fox_q_gain[l].astype(F32), FOX_HEADS)[None],
            jnp.tile(fox_k_gain[l].astype(F32), FOX_HEADS)[None], fb)

        o_a = _attn(q, k, vt, cs)
        o_b = _hgrn(hg, lb_all[l][None], hgrn_o_gain[l].astype(F32)[None])
        ops = _s5_operators(s5_lambda_re[l], s5_lambda_im[l], s5_log_step[l],
                            s5_B_re[l], s5_B_im[l], s5_C_re[l], s5_C_im[l])
        y_lb = _s5(u_lb.reshape(seq, bsz, S5_WIDTH), *ops).reshape(seq, bsz * S5_WIDTH)

        wo = w_out[l].astype(BF16)
        x = _out(x, o_a, o_b, y_lb, u_lb, s5_D[l].astype(F32)[None], s5_glu_w[l].astype(BF16),
                 s5_glu_b[l].astype(F32)[None], wo[:FOX_WIDTH], wo[FOX_WIDTH:FOX_WIDTH + HGRN_WIDTH],
                 wo[FOX_WIDTH + HGRN_WIDTH:])

        x = _ffn(x.reshape(t, d), ffn2_norm[l][None], ffn2_w_gate[l].astype(BF16),
                 ffn2_w_up[l].astype(BF16), ffn2_w_down[l].astype(BF16)).reshape(bsz, seq, d)
    return x
```

```python
import functools
import math

import jax
import jax.numpy as jnp
import numpy as np
from jax import lax
from jax.experimental import pallas as pl
from jax.experimental.pallas import tpu as pltpu

F32 = jnp.float32
BF16 = jnp.bfloat16

EPS = 1e-6
D_MODEL = 1024
D_FF = 2816
FOX_HEADS = 4
FOX_HEAD_DIM = 64
FOX_WIDTH = FOX_HEADS * FOX_HEAD_DIM
HGRN_HEADS = 4
HGRN_DIM = 128
HGRN_WIDTH = HGRN_HEADS * HGRN_DIM
S5_GROUPS = 16
S5_GROUP = 16
S5_STATE = 64
S5_WIDTH = S5_GROUPS * S5_GROUP
S5_NSTATE = S5_GROUPS * S5_STATE
S5_CHUNK = 8
S5_TILE = 128
LOG2E = math.log2(math.e)

V7X_VMEM_BYTES = 64 * 1024 * 1024
V7X_SUBLANES = 8

ATTN_BLOCK = 256
ATTN_KV_BLOCKS = 2
HGRN_CHUNK = 128
FFN_ROWS = 512
PROJ_ROWS = 512
OUT_ROWS = 512


def _vmem_limit(estimate_bytes):
    return int(min(estimate_bytes * 5 // 4 + (4 << 20), V7X_VMEM_BYTES - (4 << 20)))


def _rms_norm(x, gain):
    ms = jnp.mean(x * x, axis=-1, keepdims=True)
    return x * lax.rsqrt(ms + EPS) * gain


def _dot(a, b):
    return jnp.dot(a, b, preferred_element_type=F32)


def _dot_nt(a, b):
    return lax.dot_general(a, b, (((1,), (1,)), ((), ())), preferred_element_type=F32)


def _dot_tn(a, b):
    return lax.dot_general(a, b, (((0,), (0,)), ((), ())), preferred_element_type=F32)


def _split3(a):
    a1 = a.astype(BF16)
    r1 = a - a1.astype(F32)
    a2 = r1.astype(BF16)
    a3 = (r1 - a2.astype(F32)).astype(BF16)
    return a1, a2, a3


def _dot_exact_rhs(a, rhs):
    a1, a2, a3 = _split3(a)
    return _dot(a1, rhs) + _dot(a2, rhs) + _dot(a3, rhs)


def _dot_exact_lhs(lhs, a):
    a1, a2, a3 = _split3(a)
    return _dot(lhs, a1) + _dot(lhs, a2) + _dot(lhs, a3)


def _sigmoid(x):
    return 1.0 / (1.0 + jnp.exp(-x))


def _ffn_kernel(x_ref, g_ref, wg_ref, wu_ref, wd_ref, o_ref):
    x = x_ref[...]
    h = _rms_norm(x, g_ref[...]).astype(BF16)
    gate = _dot(h, wg_ref[...])
    up = _dot(h, wu_ref[...])
    act = (gate * _sigmoid(gate) * up).astype(BF16)
    o_ref[...] = x + 0.5 * _dot(act, wd_ref[...])


def _ffn(x2d, gain, wg, wu, wd):
    t, d = x2d.shape
    f = wg.shape[1]
    tm = min(FFN_ROWS, t)
    const = dict(pipeline_mode=pl.Buffered(1))
    est = 3 * d * f * 2 + 4 * tm * d * 4 + tm * f * (4 + 4 + 2 + 4) + tm * d * 4
    return pl.pallas_call(
        _ffn_kernel,
        grid=(t // tm,),
        in_specs=[
            pl.BlockSpec((tm, d), lambda i: (i, 0)),
            pl.BlockSpec((1, d), lambda i: (0, 0)),
            pl.BlockSpec((d, f), lambda i: (0, 0), **const),
            pl.BlockSpec((d, f), lambda i: (0, 0), **const),
            pl.BlockSpec((f, d), lambda i: (0, 0), **const),
        ],
        out_specs=pl.BlockSpec((tm, d), lambda i: (i, 0)),
        out_shape=jax.ShapeDtypeStruct((t, d), F32),
        compiler_params=pltpu.CompilerParams(
            dimension_semantics=("arbitrary",), vmem_limit_bytes=_vmem_limit(est)),
        name="ffn",
    )(x2d, gain, wg, wu, wd)


def _proj_kernel(x_ref, g_ref, wqk_ref, wvt_ref, wf_ref, wh_ref, wu_ref, qg_ref, kg_ref,
                 fb_ref, q_ref, k_ref, vt_ref, cs_ref, hg_ref, u_ref, carry_ref):
    tm = x_ref.shape[0]
    tk = vt_ref.shape[2]

    @pl.when(pl.program_id(1) == 0)
    def _():
        carry_ref[...] = jnp.zeros_like(carry_ref)

    h = _rms_norm(x_ref[...], g_ref[...]).astype(BF16)

    li = lax.broadcasted_iota(jnp.int32, (FOX_WIDTH, FOX_WIDTH), 0) // FOX_HEAD_DIM
    lj = lax.broadcasted_iota(jnp.int32, (FOX_WIDTH, FOX_WIDTH), 1) // FOX_HEAD_DIM
    group = (li == lj).astype(BF16)

    def head_norm(a, gain):
        ms = _dot_exact_rhs(a * a, group) * (1.0 / FOX_HEAD_DIM)
        return a * lax.rsqrt(ms + EPS) * gain

    qk = _dot(h, wqk_ref[...])
    q_ref[...] = (head_norm(qk[:, :FOX_WIDTH], qg_ref[...]) * (LOG2E / math.sqrt(FOX_HEAD_DIM))).astype(BF16)
    k_ref[...] = head_norm(qk[:, FOX_WIDTH:], kg_ref[...]).astype(BF16)
    vt = _dot_nt(wvt_ref[...], h).astype(BF16)
    for blk in range(tm // tk):
        vt_ref[blk] = vt[:, blk * tk:(blk + 1) * tk]

    fl = _dot(h, wf_ref[...]) + fb_ref[...]
    ls = jnp.minimum(fl, 0.0) - jnp.log1p(jnp.exp(-jnp.abs(fl)))
    ri = lax.broadcasted_iota(jnp.int32, (tm, tm), 0)
    ci = lax.broadcasted_iota(jnp.int32, (tm, tm), 1)
    lower = (ci <= ri).astype(BF16)
    c = _dot_exact_lhs(lower, ls) + carry_ref[0:1, :]
    carry_ref[...] = jnp.broadcast_to(c[tm - 1:tm, :], carry_ref.shape)
    c1, c2, c3 = _split3(c * LOG2E)
    lane = lax.broadcasted_iota(jnp.int32, c.shape, 1)
    cs_ref[...] = jnp.where(lane < FOX_HEADS, c1, jnp.where(lane < 2 * FOX_HEADS, c2, c3))

    hg_ref[...] = _dot(h, wh_ref[...])
    u_ref[...] = _dot(h, wu_ref[...]).reshape(u_ref.shape)


def _proj(x, gain, wqk, wvt, wf, wh, wu, qg, kg, fb):
    b, l, d = x.shape
    tm = min(PROJ_ROWS, l)
    tk = min(ATTN_BLOCK, l)
    hw = wh.shape[1]
    full = lambda shape: pl.BlockSpec(shape, lambda bi, i: tuple(0 for _ in shape))
    est = 2 * (d * (512 + 256 + 128 + hw + 256) * 2) + 2 * tm * d * 4 + 2 * tm * (hw + 256 + 512) * 4 + tm * hw * 4
    return pl.pallas_call(
        _proj_kernel,
        grid=(b, l // tm),
        in_specs=[
            pl.BlockSpec((None, tm, d), lambda bi, i: (bi, i, 0)),
            full((1, d)), full(wqk.shape), full(wvt.shape), full(wf.shape), full(wh.shape),
            full(wu.shape), full((1, FOX_WIDTH)), full((1, FOX_WIDTH)), full((1, 128)),
        ],
        out_specs=[
            pl.BlockSpec((None, tm, FOX_WIDTH), lambda bi, i: (bi, i, 0)),
            pl.BlockSpec((None, tm, FOX_WIDTH), lambda bi, i: (bi, i, 0)),
            pl.BlockSpec((None, tm // tk, FOX_WIDTH, tk), lambda bi, i: (bi, i, 0, 0)),
            pl.BlockSpec((None, tm, 128), lambda bi, i: (bi, i, 0)),
            pl.BlockSpec((None, tm, hw), lambda bi, i: (bi, i, 0)),
            pl.BlockSpec((tm // S5_CHUNK, None, S5_CHUNK, S5_WIDTH), lambda bi, i: (i, bi, 0, 0)),
        ],
        out_shape=[
            jax.ShapeDtypeStruct((b, l, FOX_WIDTH), BF16),
            jax.ShapeDtypeStruct((b, l, FOX_WIDTH), BF16),
            jax.ShapeDtypeStruct((b, l // tk, FOX_WIDTH, tk), BF16),
            jax.ShapeDtypeStruct((b, l, 128), BF16),
            jax.ShapeDtypeStruct((b, l, hw), F32),
            jax.ShapeDtypeStruct((l // S5_CHUNK, b, S5_CHUNK, S5_WIDTH), F32),
        ],
        scratch_shapes=[pltpu.VMEM((V7X_SUBLANES, 128), F32)],
        compiler_params=pltpu.CompilerParams(
            dimension_semantics=("arbitrary", "arbitrary"), vmem_limit_bytes=_vmem_limit(est)),
        name="proj",
    )(x, gain, wqk, wvt, wf, wh, wu, qg, kg, fb)


def _attn_kernel(q_ref, k_ref, vt_ref, csq_ref, csk_ref, e_ref, pq_ref, pk_ref, oq_ref, ok_ref,
                 o_ref, ka_ref, qa_ref, m_ref, l_ref, acc_ref, s_ref, p_ref):
    tq = q_ref.shape[0]
    i = pl.program_id(1)

    @pl.when(i == 0)
    def _():
        ka_ref[...] = (_dot(k_ref[...], e_ref[...]) + _dot(csk_ref[...], pk_ref[...])
                       + ok_ref[...]).astype(BF16)

    qa_ref[...] = (_dot(q_ref[...], e_ref[...]) + _dot(csq_ref[...], pq_ref[...])
                   + oq_ref[...]).astype(BF16)
    m_ref[...] = jnp.full(m_ref.shape, -jnp.inf, F32)
    l_ref[...] = jnp.zeros_like(l_ref)
    acc_ref[...] = jnp.zeros_like(acc_ref)
    causal = (lax.broadcasted_iota(jnp.int32, (tq, tq), 0)
              <= lax.broadcasted_iota(jnp.int32, (tq, tq), 1))

    def step(j0, nblk, diagonal):
        tk = nblk * tq
        rows = pl.ds(pl.multiple_of(j0 * tq, tq), tk)
        for hd in range(FOX_HEADS):
            slab = slice(hd * 128, (hd + 1) * 128)
            s = _dot_nt(ka_ref[rows, slab], qa_ref[:, slab])
            s_ref[hd, 0:tk, :] = jnp.where(causal, s, -jnp.inf) if diagonal else s
        alphas = []
        for hd in range(FOX_HEADS):
            s = s_ref[hd, 0:tk, :]
            m_old = m_ref[hd]
            m_new = jnp.maximum(m_old, jnp.max(s, axis=0, keepdims=True))
            alpha = jnp.exp2(m_old - m_new)
            p = jnp.exp2(s - m_new)
            m_ref[hd] = m_new
            l_ref[hd] = alpha * l_ref[hd] + jnp.sum(p, axis=0, keepdims=True)
            p_ref[hd, 0:tk, :] = p.astype(BF16)
            alphas.append(alpha)
        for hd in range(FOX_HEADS):
            head_rows = slice(hd * FOX_HEAD_DIM, (hd + 1) * FOX_HEAD_DIM)
            pv = _dot(vt_ref[j0, head_rows, :], p_ref[hd, 0:tq, :])
            for bk in range(1, nblk):
                pv = pv + _dot(vt_ref[j0 + bk, head_rows, :], p_ref[hd, bk * tq:(bk + 1) * tq, :])
            acc_ref[hd] = alphas[hd] * acc_ref[hd] + pv

    def body(jj, carry):
        step(jj * ATTN_KV_BLOCKS, ATTN_KV_BLOCKS, False)
        return carry

    lax.fori_loop(0, i // ATTN_KV_BLOCKS, body, 0)
    for rem in range(1, ATTN_KV_BLOCKS):
        @pl.when(i % ATTN_KV_BLOCKS == rem)
        def _(rem=rem):
            step(i - rem, rem, False)
    step(i, 1, True)
    outs = [acc_ref[hd] / l_ref[hd] for hd in range(FOX_HEADS)]
    o_ref[...] = jnp.concatenate(outs, axis=0).T.astype(o_ref.dtype)


def _attn_placement():
    w = FOX_HEADS * 128
    e = np.zeros((FOX_WIDTH, w), np.float32)
    cols = np.arange(FOX_WIDTH)
    e[cols, (cols // FOX_HEAD_DIM) * 128 + cols % FOX_HEAD_DIM] = 1.0
    pq, pk = np.zeros((128, w), np.float32), np.zeros((128, w), np.float32)
    oq, ok = np.zeros((1, w), np.float32), np.zeros((1, w), np.float32)
    for hd in range(FOX_HEADS):
        base = hd * 128 + FOX_HEAD_DIM
        for term in range(3):
            pk[term * FOX_HEADS + hd, base + term] = -1.0
            oq[0, base + term] = 1.0
            pq[term * FOX_HEADS + hd, base + 3 + term] = 1.0
            ok[0, base + 3 + term] = 1.0
    return (jnp.asarray(e, BF16), jnp.asarray(pq, BF16), jnp.asarray(pk, BF16),
            jnp.asarray(oq), jnp.asarray(ok))


def _attn(q, k, vt, cs):
    b, l, w = q.shape
    nkv, tk = vt.shape[1], vt.shape[3]
    tq = tk
    wa = FOX_HEADS * 128
    e, pq, pk, oq, ok = _attn_placement()
    full = lambda shape: pl.BlockSpec(shape, lambda bi, i: tuple(0 for _ in shape))
    est = (2 * (l * w * 2 * 2 + l * 128 * 2) + l * wa * (2 + 4 + 4) + 4 * tq * wa * 4
           + 12 * tq * tk * 4 + FOX_HEADS * 128 * tq * 4)
    return pl.pallas_call(
        _attn_kernel,
        grid=(b, l // tq),
        in_specs=[
            pl.BlockSpec((None, tq, w), lambda bi, i: (bi, i, 0)),
            pl.BlockSpec((None, l, w), lambda bi, i: (bi, 0, 0)),
            pl.BlockSpec((None, nkv, w, tk), lambda bi, i: (bi, 0, 0, 0)),
            pl.BlockSpec((None, tq, 128), lambda bi, i: (bi, i, 0)),
            pl.BlockSpec((None, l, 128), lambda bi, i: (bi, 0, 0)),
            full(e.shape), full(pq.shape), full(pk.shape), full(oq.shape), full(ok.shape),
        ],
        out_specs=pl.BlockSpec((None, tq, w), lambda bi, i: (bi, i, 0)),
        out_shape=jax.ShapeDtypeStruct((b, l, w), BF16),
        scratch_shapes=[pltpu.VMEM((l, wa), BF16), pltpu.VMEM((tq, wa), BF16),
                        pltpu.VMEM((FOX_HEADS, 1, tq), F32), pltpu.VMEM((FOX_HEADS, 1, tq), F32),
                        pltpu.VMEM((FOX_HEADS, FOX_HEAD_DIM, tq), F32),
                        pltpu.VMEM((FOX_HEADS, ATTN_KV_BLOCKS * tq, tq), F32),
                        pltpu.VMEM((FOX_HEADS, ATTN_KV_BLOCKS * tq, tq), BF16)],
        compiler_params=pltpu.CompilerParams(
            dimension_semantics=("arbitrary", "arbitrary"), vmem_limit_bytes=_vmem_limit(est)),
        name="fox_attn",
    )(q, k, vt, cs, cs, e, pq, pk, oq, ok)


def _hgrn_levels(c):
    return [c >> (s + 1) for s in range(int(math.log2(c)))]


def _hgrn_level_ids(c):
    t = np.arange(c)[:, None]
    s = np.arange(c)[None, :]
    lvl = np.floor(np.log2(np.maximum(t ^ s, 1))).astype(np.int32) + 1
    return jnp.asarray(np.where(s > t, -1, np.where(s == t, 0, lvl)).astype(np.int32))


def _hgrn_kernel(q_ref, f_ref, i_ref, g_ref, lb_ref, og_ref, lvl_ref, o_ref, st_ref, b_ref):
    c = q_ref.shape[0]

    @pl.when(pl.program_id(1) == 0)
    def _():
        st_ref[...] = jnp.zeros_like(st_ref)

    lb = lb_ref[...]
    f = lb + (1.0 - lb) * _sigmoid(f_ref[...])
    key = 1.0 - f
    q = q_ref[...]
    ti = lax.broadcasted_iota(jnp.int32, (c, c), 0)
    si = lax.broadcasted_iota(jnp.int32, (c, c), 1)
    lower = (si <= ti).astype(BF16)
    b = _dot_exact_lhs(lower, jnp.log(f))
    b_ref[...] = b
    row = lax.broadcasted_iota(jnp.int32, b.shape, 0)

    def ref_rows(m):
        if 2 * m >= V7X_SUBLANES:
            parts = [jnp.broadcast_to(b_ref[g0 + m - 1:g0 + m, :], (2 * m, b.shape[1]))
                     for g0 in range(0, c, 2 * m)]
            return jnp.concatenate(parts, axis=0) if len(parts) > 1 else parts[0]
        off = row % (2 * m)
        out = b
        for o in range(2 * m):
            if o != m - 1:
                out = jnp.where(off == o, pltpu.roll(b, (o - (m - 1)) % c, axis=0), out)
        return out

    lvl = lvl_ref[...]
    qb, kb = q.astype(BF16), key.astype(BF16)
    scores = [jnp.zeros((c, c), F32) for _ in range(HGRN_HEADS)]
    for idx, m in enumerate([0] + _hgrn_levels(c)[::-1]):
        if m == 0:
            z, y = qb, kb
        else:
            e = jnp.exp2(jnp.abs(b - ref_rows(m)) * (-LOG2E)).astype(BF16)
            z, y = qb * e, kb * e
        mask = lvl == idx
        for hd in range(HGRN_HEADS):
            sl = slice(hd * HGRN_DIM, (hd + 1) * HGRN_DIM)
            scores[hd] = jnp.where(mask, _dot_nt(z[:, sl], y[:, sl]), scores[hd])

    b_last = b[c - 1:c, :]
    q_in = (q * jnp.exp(b)).astype(BF16)
    k_out = (key * jnp.exp(b_last - b)).astype(BF16)
    v = i_ref[...].astype(BF16)
    g = g_ref[...]
    gate = g * _sigmoid(g)
    og = og_ref[...]
    decay = jnp.exp(b_last)
    for hd in range(HGRN_HEADS):
        sl = slice(hd * HGRN_DIM, (hd + 1) * HGRN_DIM)
        st = st_ref[hd]
        o = _dot(scores[hd].astype(BF16), v[:, sl]) + _dot_nt(q_in[:, sl], st.astype(BF16))
        st_ref[hd] = st * decay[:, sl] + _dot_tn(v[:, sl], k_out[:, sl])
        o_ref[:, sl] = (_rms_norm(o, og[:, sl]) * gate[:, sl]).astype(o_ref.dtype)


def _hgrn(hg, lb, og):
    b, l, _ = hg.shape
    c = min(HGRN_CHUNK, l)
    w = HGRN_WIDTH
    part = lambda p: pl.BlockSpec((None, c, w), lambda bi, i, p=p: (bi, i, p))
    est = 2 * 5 * c * w * 4 + 16 * c * w * 4 + HGRN_HEADS * HGRN_DIM * HGRN_DIM * 4
    return pl.pallas_call(
        _hgrn_kernel,
        grid=(b, l // c),
        in_specs=[part(0), part(1), part(2), part(3),
                  pl.BlockSpec((1, w), lambda bi, i: (0, 0)),
                  pl.BlockSpec((1, w), lambda bi, i: (0, 0)),
                  pl.BlockSpec((c, c), lambda bi, i: (0, 0))],
        out_specs=pl.BlockSpec((None, c, w), lambda bi, i: (bi, i, 0)),
        out_shape=jax.ShapeDtypeStruct((b, l, w), BF16),
        scratch_shapes=[pltpu.VMEM((HGRN_HEADS, HGRN_DIM, HGRN_DIM), F32),
                        pltpu.VMEM((c, w), F32)],
        compiler_params=pltpu.CompilerParams(
            dimension_semantics=("arbitrary", "arbitrary"), vmem_limit_bytes=_vmem_limit(est)),
        name="hgrn2",
    )(hg, hg, hg, hg, lb, og, _hgrn_level_ids(c))


def _s5_kernel(u_ref, kd_ref, pin_ref, pout_ref, ar_ref, ai_ref, y_ref, xr_ref, xi_ref):
    nc, nb, cs, w = u_ref.shape
    ns = ar_ref.shape[1]

    @pl.when(pl.program_id(0) == 0)
    def _():
        xr_ref[...] = jnp.zeros_like(xr_ref)
        xi_ref[...] = jnp.zeros_like(xi_ref)

    us = [u_ref[:, :, s, :].reshape(nc * nb, w).astype(BF16) for s in range(cs)]
    win = _dot(us[0], pin_ref[0])
    for s in range(1, cs):
        win = win + _dot(us[s], pin_ref[s])

    ar, ai = ar_ref[...], ai_ref[...]
    xr, xi = xr_ref[...], xi_ref[...]
    prev_r, prev_i = [], []
    for c in range(nc):
        prev_r.append(xr)
        prev_i.append(xi)
        wr = win[c * nb:(c + 1) * nb, :ns]
        wi = win[c * nb:(c + 1) * nb, ns:]
        xr, xi = ar * xr - ai * xi + wr, ar * xi + ai * xr + wi
    xr_ref[...] = xr
    xi_ref[...] = xi
    xp = jnp.concatenate([jnp.concatenate(prev_r, axis=0), jnp.concatenate(prev_i, axis=0)],
                         axis=1).astype(BF16)

    for t in range(cs):
        lags = jnp.concatenate([us[t - d] for d in range(t + 1)], axis=1) if t else us[0]
        yt = _dot(xp, pout_ref[t]) + _dot(lags, kd_ref[0:(t + 1) * w, :])
        y_ref[:, :, t, :] = yt.reshape(nc, nb, w)


def _s5(u4, kd, pin, pout, ar, ai):
    nchunk, nb, cs, w = u4.shape
    nc = min(S5_TILE // cs, nchunk)
    const = dict(pipeline_mode=pl.Buffered(1))
    full = lambda shape: pl.BlockSpec(shape, lambda i: tuple(0 for _ in shape), **const)
    rows = nc * cs * nb
    chunked = pl.BlockSpec((nc, nb, cs, w), lambda i: (i, 0, 0, 0))
    est = (4 * rows * w * 4 + (kd.size + pin.size + pout.size) * 2 + rows // cs * 2 * S5_NSTATE * (4 + 4 + 2)
           + rows * w * (2 + 2 + 4))
    return pl.pallas_call(
        _s5_kernel,
        grid=(nchunk // nc,),
        in_specs=[chunked, full(kd.shape), full(pin.shape), full(pout.shape), full(ar.shape), full(ai.shape)],
        out_specs=chunked,
        out_shape=jax.ShapeDtypeStruct(u4.shape, F32),
        scratch_shapes=[pltpu.VMEM((nb, S5_NSTATE), F32), pltpu.VMEM((nb, S5_NSTATE), F32)],
        compiler_params=pltpu.CompilerParams(
            dimension_semantics=("arbitrary",), vmem_limit_bytes=_vmem_limit(est)),
        name="s5",
    )(u4, kd, pin, pout, ar, ai)


def _s5_operators(lam_re, lam_im, log_step, b_re, b_im, c_re, c_im):
    lr, li = lam_re.astype(F32), lam_im.astype(F32)
    dt = jnp.exp(log_step.astype(F32))[:, None]
    mag = jnp.exp(lr * dt)
    lbar_re, lbar_im = mag * jnp.cos(li * dt), mag * jnp.sin(li * dt)
    nr, ni = lbar_re - 1.0, lbar_im
    den = lr * lr + li * li
    coef_re = (nr * lr + ni * li) / den
    coef_im = (ni * lr - nr * li) / den
    br, bi = b_re.astype(F32), b_im.astype(F32)
    bbar_re = coef_re[..., None] * br - coef_im[..., None] * bi
    bbar_im = coef_re[..., None] * bi + coef_im[..., None] * br
    cr, ci = c_re.astype(F32), c_im.astype(F32)

    pr, pi = [jnp.ones_like(lbar_re)], [jnp.zeros_like(lbar_re)]
    for _ in range(S5_CHUNK):
        pr, pi = (pr + [pr[-1] * lbar_re - pi[-1] * lbar_im],
                  pi + [pr[-1] * lbar_im + pi[-1] * lbar_re])
    pw_re, pw_im = jnp.stack(pr), jnp.stack(pi)
    cs = S5_CHUNK

    def block_diag(compact, rows_group, cols_group, reps):
        tiled = jnp.tile(compact, reps)
        r = lax.broadcasted_iota(jnp.int32, tiled.shape, tiled.ndim - 2)
        c = lax.broadcasted_iota(jnp.int32, tiled.shape, tiled.ndim - 1)
        return jnp.where(rows_group(r) == cols_group(c), tiled, 0.0).astype(BF16)

    lane_group = lambda c: c // S5_GROUP
    state_group = lambda c: (c % S5_NSTATE) // S5_STATE

    ab_re = pw_re[:cs, :, :, None] * bbar_re - pw_im[:cs, :, :, None] * bbar_im
    ab_im = pw_re[:cs, :, :, None] * bbar_im + pw_im[:cs, :, :, None] * bbar_re
    taps = jnp.sum(cr[None, :, :, :, None] * ab_re[:, :, None] - ci[None, :, :, :, None] * ab_im[:, :, None],
                   axis=3)
    kd = block_diag(taps.transpose(0, 1, 3, 2).reshape(cs * S5_WIDTH, S5_GROUP),
                    lambda r: (r // S5_GROUP) % S5_GROUPS, lane_group, (1, S5_GROUPS))

    rev = cs - 1 - jnp.arange(cs)
    ab = jnp.stack([ab_re[rev], ab_im[rev]], axis=1)
    pin = block_diag(ab.transpose(0, 4, 1, 2, 3).reshape(cs, S5_GROUP, 2 * S5_NSTATE),
                     lane_group, state_group, (1, S5_GROUPS, 1))

    qr, qi = pw_re[1:], pw_im[1:]
    po_re = cr[None] * qr[:, :, None, :] - ci[None] * qi[:, :, None, :]
    po_im = -(cr[None] * qi[:, :, None, :] + ci[None] * qr[:, :, None, :])
    po = jnp.stack([po_re, po_im], axis=1)
    pout = block_diag(po.transpose(0, 1, 2, 4, 3).reshape(cs, 2 * S5_NSTATE, S5_GROUP),
                      state_group, lane_group, (1, 1, S5_GROUPS))
    a_re = pw_re[cs].reshape(1, S5_NSTATE)
    a_im = pw_im[cs].reshape(1, S5_NSTATE)
    return kd, pin, pout, a_re, a_im


def _out_kernel(x_ref, oa_ref, ob_ref, y_ref, u_ref, d_ref, gw_ref, gb_ref, wa_ref, wb_ref, wc_ref, o_ref):
    tm = x_ref.shape[0]
    y = y_ref[...].reshape(tm, S5_WIDTH) + d_ref[...] * u_ref[...].reshape(tm, S5_WIDTH)
    z = 0.5 * y * (1.0 + jnp.tanh(math.sqrt(2.0 / math.pi) * (y + 0.044715 * (y * y * y))))
    oc = z * _sigmoid(_dot(z.astype(BF16), gw_ref[...]) + gb_ref[...])
    o_ref[...] = (x_ref[...] + _dot(oa_ref[...], wa_ref[...]) + _dot(ob_ref[...], wb_ref[...])
                  + _dot(oc.astype(BF16), wc_ref[...]))


def _out(x, oa, ob, y_lb, u_lb, dvec, gw, gb, wa, wb, wc):
    b, l, d = x.shape
    tm = min(OUT_ROWS, l)
    rows = lambda w: pl.BlockSpec((None, tm, w), lambda bi, i: (bi, i, 0))
    token_major = pl.BlockSpec((tm // S5_CHUNK, None, S5_CHUNK, S5_WIDTH), lambda bi, i: (i, bi, 0, 0))
    full = lambda shape: pl.BlockSpec(shape, lambda bi, i: tuple(0 for _ in shape))
    est = 2 * (2 * tm * d * 4 + tm * (256 * 2 + 512 * 2 + 256 * 8)) + 2 * 2 * (d * d + 256 * 256) + 4 * tm * d * 4
    return pl.pallas_call(
        _out_kernel,
        grid=(b, l // tm),
        in_specs=[rows(d), rows(FOX_WIDTH), rows(HGRN_WIDTH), token_major, token_major,
                  full((1, S5_WIDTH)), full(gw.shape), full((1, S5_WIDTH)),
                  full(wa.shape), full(wb.shape), full(wc.shape)],
        out_specs=rows(d),
        out_shape=jax.ShapeDtypeStruct((b, l, d), F32),
        compiler_params=pltpu.CompilerParams(
            dimension_semantics=("arbitrary", "arbitrary"), vmem_limit_bytes=_vmem_limit(est)),
        name="out_proj",
    )(x, oa, ob, y_lb, u_lb, dvec, gw, gb, wa, wb, wc)


def kernel(x, ffn1_norm, ffn1_w_gate, ffn1_w_up, ffn1_w_down, mix_norm, w_in, fox_f_bias, fox_q_gain, fox_k_gain, hgrn_lb, hgrn_o_gain, s5_lambda_re, s5_lambda_im, s5_log_step, s5_B_re, s5_B_im, s5_C_re, s5_C_im, s5_D, s5_glu_w, s5_glu_b, w_out, ffn2_norm, ffn2_w_gate, ffn2_w_up, ffn2_w_down):
    bsz, seq, d = x.shape
    depth = w_in.shape[0]
    t = bsz * seq

    lb_all = jnp.cumsum(jax.nn.softmax(hgrn_lb.astype(F32), axis=0), axis=0)
    lb_all = lb_all - lb_all[0:1]

    o_q, o_k, o_v = 0, FOX_WIDTH, 2 * FOX_WIDTH
    o_f = 3 * FOX_WIDTH
    o_h = o_f + FOX_HEADS
    o_u = o_h + 4 * HGRN_WIDTH

    for l in range(depth):
        x = _ffn(x.reshape(t, d), ffn1_norm[l][None], ffn1_w_gate[l].astype(BF16),
                 ffn1_w_up[l].astype(BF16), ffn1_w_down[l].astype(BF16)).reshape(bsz, seq, d)

        w = w_in[l]
        wf = jnp.zeros((d, 128), F32).at[:, :3 * FOX_HEADS].set(jnp.tile(w[:, o_f:o_h], (1, 3))).astype(BF16)
        fb = jnp.zeros((1, 128), F32).at[0, :3 * FOX_HEADS].set(jnp.tile(fox_f_bias[l].astype(F32), 3))
        q, k, vt, cs, hg, u_lb = _proj(
            x, mix_norm[l][None], w[:, o_q:o_v].astype(BF16), w[:, o_v:o_f].T.astype(BF16), wf,
            w[:, o_h:o_u].astype(BF16), w[:, o_u:].astype(BF16),
            jnp.tile(fox_q_gain[l].astype(F32), FOX_HEADS)[None],
            jnp.tile(fox_k_gain[l].astype(F32), FOX_HEADS)[None], fb)

        o_a = _attn(q, k, vt, cs)
        o_b = _hgrn(hg, lb_all[l][None], hgrn_o_gain[l].astype(F32)[None])
        ops = _s5_operators(s5_lambda_re[l], s5_lambda_im[l], s5_log_step[l],
                            s5_B_re[l], s5_B_im[l], s5_C_re[l], s5_C_im[l])
        y_lb = _s5(u_lb, *ops)

        wo = w_out[l].astype(BF16)
        x = _out(x, o_a, o_b, y_lb, u_lb, s5_D[l].astype(F32)[None], s5_glu_w[l].astype(BF16),
                 s5_glu_b[l].astype(F32)[None], wo[:FOX_WIDTH], wo[FOX_WIDTH:FOX_WIDTH + HGRN_WIDTH],
                 wo[FOX_WIDTH + HGRN_WIDTH:])

        x = _ffn(x.reshape(t, d), ffn2_norm[l][None], ffn2_w_gate[l].astype(BF16),
                 ffn2_w_up[l].astype(BF16), ffn2_w_down[l].astype(BF16)).reshape(bsz, seq, d)
    return x
```

```python
import functools
import math

import jax
import jax.numpy as jnp
import numpy as np
from jax import lax
from jax.experimental import pallas as pl
from jax.experimental.pallas import tpu as pltpu

F32 = jnp.float32
BF16 = jnp.bfloat16

EPS = 1e-6
D_MODEL = 1024
D_FF = 2816
FOX_HEADS = 4
FOX_HEAD_DIM = 64
FOX_WIDTH = FOX_HEADS * FOX_HEAD_DIM
HGRN_HEADS = 4
HGRN_DIM = 128
HGRN_WIDTH = HGRN_HEADS * HGRN_DIM
S5_GROUPS = 16
S5_GROUP = 16
S5_STATE = 64
S5_WIDTH = S5_GROUPS * S5_GROUP
S5_NSTATE = S5_GROUPS * S5_STATE
S5_CHUNK = 8
S5_TILE = 128
LOG2E = math.log2(math.e)

V7X_VMEM_BYTES = 64 * 1024 * 1024
V7X_SUBLANES = 8

ATTN_BLOCK = 256
ATTN_KV_BLOCKS = 2
HGRN_CHUNK = 128
FFN_ROWS = 512
PROJ_ROWS = 512
OUT_ROWS = 512


def _vmem_limit(estimate_bytes):
    return int(min(estimate_bytes * 5 // 4 + (4 << 20), V7X_VMEM_BYTES - (4 << 20)))


def _rms_norm(x, gain):
    ms = jnp.mean(x * x, axis=-1, keepdims=True)
    return x * lax.rsqrt(ms + EPS) * gain


def _dot(a, b):
    return jnp.dot(a, b, preferred_element_type=F32)


def _dot_nt(a, b):
    return lax.dot_general(a, b, (((1,), (1,)), ((), ())), preferred_element_type=F32)


def _dot_tn(a, b):
    return lax.dot_general(a, b, (((0,), (0,)), ((), ())), preferred_element_type=F32)


def _split3(a):
    a1 = a.astype(BF16)
    r1 = a - a1.astype(F32)
    a2 = r1.astype(BF16)
    a3 = (r1 - a2.astype(F32)).astype(BF16)
    return a1, a2, a3


def _dot_exact_rhs(a, rhs, terms=3):
    parts = _split3(a)[:terms]
    out = _dot(parts[0], rhs)
    for part in parts[1:]:
        out = out + _dot(part, rhs)
    return out


def _dot_exact_lhs(lhs, a, terms=3):
    parts = _split3(a)[:terms]
    out = _dot(lhs, parts[0])
    for part in parts[1:]:
        out = out + _dot(lhs, part)
    return out


def _sigmoid(x):
    return 1.0 / (1.0 + jnp.exp(-x))


def _ffn_kernel(x_ref, g_ref, wg_ref, wu_ref, wd_ref, o_ref):
    x = x_ref[...]
    h = _rms_norm(x, g_ref[...]).astype(BF16)
    gate = _dot(h, wg_ref[...])
    up = _dot(h, wu_ref[...])
    act = (gate * _sigmoid(gate) * up).astype(BF16)
    o_ref[...] = x + 0.5 * _dot(act, wd_ref[...])


def _ffn(x2d, gain, wg, wu, wd):
    t, d = x2d.shape
    f = wg.shape[1]
    tm = min(FFN_ROWS, t)
    const = dict(pipeline_mode=pl.Buffered(1))
    est = 3 * d * f * 2 + 4 * tm * d * 4 + tm * f * (4 + 4 + 2 + 4) + tm * d * 4
    return pl.pallas_call(
        _ffn_kernel,
        grid=(t // tm,),
        in_specs=[
            pl.BlockSpec((tm, d), lambda i: (i, 0)),
            pl.BlockSpec((1, d), lambda i: (0, 0)),
            pl.BlockSpec((d, f), lambda i: (0, 0), **const),
            pl.BlockSpec((d, f), lambda i: (0, 0), **const),
            pl.BlockSpec((f, d), lambda i: (0, 0), **const),
        ],
        out_specs=pl.BlockSpec((tm, d), lambda i: (i, 0)),
        out_shape=jax.ShapeDtypeStruct((t, d), F32),
        compiler_params=pltpu.CompilerParams(
            dimension_semantics=("arbitrary",), vmem_limit_bytes=_vmem_limit(est)),
        name="ffn",
    )(x2d, gain, wg, wu, wd)


def _proj_kernel(x_ref, g_ref, wqk_ref, wvt_ref, wf_ref, wh_ref, wu_ref, qg_ref, kg_ref,
                 fb_ref, q_ref, k_ref, vt_ref, cs_ref, hg_ref, u_ref, carry_ref):
    tm = x_ref.shape[0]
    tk = vt_ref.shape[2]

    @pl.when(pl.program_id(1) == 0)
    def _():
        carry_ref[...] = jnp.zeros_like(carry_ref)

    h = _rms_norm(x_ref[...], g_ref[...]).astype(BF16)

    li = lax.broadcasted_iota(jnp.int32, (FOX_WIDTH, FOX_WIDTH), 0) // FOX_HEAD_DIM
    lj = lax.broadcasted_iota(jnp.int32, (FOX_WIDTH, FOX_WIDTH), 1) // FOX_HEAD_DIM
    group = (li == lj).astype(BF16)

    def head_norm(a, gain):
        ms = _dot_exact_rhs(a * a, group, terms=2) * (1.0 / FOX_HEAD_DIM)
        return a * lax.rsqrt(ms + EPS) * gain

    qk = _dot(h, wqk_ref[...])
    q_ref[...] = (head_norm(qk[:, :FOX_WIDTH], qg_ref[...]) * (LOG2E / math.sqrt(FOX_HEAD_DIM))).astype(BF16)
    k_ref[...] = head_norm(qk[:, FOX_WIDTH:], kg_ref[...]).astype(BF16)
    vt = _dot_nt(wvt_ref[...], h).astype(BF16)
    for blk in range(tm // tk):
        vt_ref[blk] = vt[:, blk * tk:(blk + 1) * tk]

    fl = _dot(h, wf_ref[...]) + fb_ref[...]
    ls = jnp.minimum(fl, 0.0) - jnp.log1p(jnp.exp(-jnp.abs(fl)))
    ri = lax.broadcasted_iota(jnp.int32, (tm, tm), 0)
    ci = lax.broadcasted_iota(jnp.int32, (tm, tm), 1)
    lower = (ci <= ri).astype(BF16)
    c = _dot_exact_lhs(lower, ls) + carry_ref[0:1, :]
    carry_ref[...] = jnp.broadcast_to(c[tm - 1:tm, :], carry_ref.shape)
    c1, c2, c3 = _split3(c * LOG2E)
    lane = lax.broadcasted_iota(jnp.int32, c.shape, 1)
    cs_ref[...] = jnp.where(lane < FOX_HEADS, c1, jnp.where(lane < 2 * FOX_HEADS, c2, c3))

    hg_ref[...] = _dot(h, wh_ref[...])
    u_ref[...] = _dot(h, wu_ref[...]).reshape(u_ref.shape)


def _proj(x, gain, wqk, wvt, wf, wh, wu, qg, kg, fb):
    b, l, d = x.shape
    tm = min(PROJ_ROWS, l)
    tk = min(ATTN_BLOCK, l)
    hw = wh.shape[1]
    full = lambda shape: pl.BlockSpec(shape, lambda bi, i: tuple(0 for _ in shape))
    est = 2 * (d * (512 + 256 + 128 + hw + 256) * 2) + 2 * tm * d * 4 + 2 * tm * (hw + 256 + 512) * 4 + tm * hw * 4
    return pl.pallas_call(
        _proj_kernel,
        grid=(b, l // tm),
        in_specs=[
            pl.BlockSpec((None, tm, d), lambda bi, i: (bi, i, 0)),
            full((1, d)), full(wqk.shape), full(wvt.shape), full(wf.shape), full(wh.shape),
            full(wu.shape), full((1, FOX_WIDTH)), full((1, FOX_WIDTH)), full((1, 128)),
        ],
        out_specs=[
            pl.BlockSpec((None, tm, FOX_WIDTH), lambda bi, i: (bi, i, 0)),
            pl.BlockSpec((None, tm, FOX_WIDTH), lambda bi, i: (bi, i, 0)),
            pl.BlockSpec((None, tm // tk, FOX_WIDTH, tk), lambda bi, i: (bi, i, 0, 0)),
            pl.BlockSpec((None, tm, 128), lambda bi, i: (bi, i, 0)),
            pl.BlockSpec((None, tm, hw), lambda bi, i: (bi, i, 0)),
            pl.BlockSpec((tm // S5_CHUNK, None, S5_CHUNK, S5_WIDTH), lambda bi, i: (i, bi, 0, 0)),
        ],
        out_shape=[
            jax.ShapeDtypeStruct((b, l, FOX_WIDTH), BF16),
            jax.ShapeDtypeStruct((b, l, FOX_WIDTH), BF16),
            jax.ShapeDtypeStruct((b, l // tk, FOX_WIDTH, tk), BF16),
            jax.ShapeDtypeStruct((b, l, 128), BF16),
            jax.ShapeDtypeStruct((b, l, hw), F32),
            jax.ShapeDtypeStruct((l // S5_CHUNK, b, S5_CHUNK, S5_WIDTH), F32),
        ],
        scratch_shapes=[pltpu.VMEM((V7X_SUBLANES, 128), F32)],
        compiler_params=pltpu.CompilerParams(
            dimension_semantics=("arbitrary", "arbitrary"), vmem_limit_bytes=_vmem_limit(est)),
        name="proj",
    )(x, gain, wqk, wvt, wf, wh, wu, qg, kg, fb)


def _attn_kernel(q_ref, k_ref, vt_ref, csq_ref, csk_ref, e_ref, pq_ref, pk_ref, oq_ref, ok_ref,
                 o_ref, ka_ref, qa_ref, m_ref, l_ref, acc_ref, s_ref, p_ref):
    tq = q_ref.shape[0]
    i = pl.program_id(1)

    @pl.when(i == 0)
    def _():
        ka_ref[...] = (_dot(k_ref[...], e_ref[...]) + _dot(csk_ref[...], pk_ref[...])
                       + ok_ref[...]).astype(BF16)

    qa_ref[...] = (_dot(q_ref[...], e_ref[...]) + _dot(csq_ref[...], pq_ref[...])
                   + oq_ref[...]).astype(BF16)
    m_ref[...] = jnp.full(m_ref.shape, -jnp.inf, F32)
    l_ref[...] = jnp.zeros_like(l_ref)
    acc_ref[...] = jnp.zeros_like(acc_ref)
    nblk = ATTN_KV_BLOCKS
    tk = nblk * tq
    nfull = i // nblk
    key_minus_qry = (lax.broadcasted_iota(jnp.int32, (tk, tq), 0)
                     - lax.broadcasted_iota(jnp.int32, (tk, tq), 1))

    def logits(jj, slot):
        rows = pl.ds(pl.multiple_of(jj * tk, tk), tk)
        for hd in range(FOX_HEADS):
            slab = slice(hd * 128, (hd + 1) * 128)
            s_ref[slot, hd] = _dot_nt(ka_ref[rows, slab], qa_ref[:, slab])

    def update(jj, slot, last):
        alphas = []
        for hd in range(FOX_HEADS):
            s = s_ref[slot, hd]
            if last:
                s = jnp.where(key_minus_qry <= (i - jj * nblk) * tq, s, -jnp.inf)
            m_old = m_ref[hd]
            m_new = jnp.maximum(m_old, jnp.max(s, axis=0, keepdims=True))
            alpha = jnp.exp2(m_old - m_new)
            p = jnp.exp2(s - m_new)
            m_ref[hd] = m_new
            l_ref[hd] = alpha * l_ref[hd] + jnp.sum(p, axis=0, keepdims=True)
            p_ref[hd] = p.astype(BF16)
            alphas.append(alpha)
        for hd in range(FOX_HEADS):
            head_rows = slice(hd * FOX_HEAD_DIM, (hd + 1) * FOX_HEAD_DIM)
            pv = _dot(vt_ref[jj * nblk, head_rows, :], p_ref[hd, 0:tq, :])
            for bk in range(1, nblk):
                pv = pv + _dot(vt_ref[jj * nblk + bk, head_rows, :], p_ref[hd, bk * tq:(bk + 1) * tq, :])
            acc_ref[hd] = alphas[hd] * acc_ref[hd] + pv

    def pair(kk, carry):
        logits(2 * kk + 1, 1)
        update(2 * kk, 0, False)
        logits(2 * kk + 2, 0)
        update(2 * kk + 1, 1, False)
        return carry

    logits(0, 0)
    npair = nfull // 2
    lax.fori_loop(0, npair, pair, 0)

    @pl.when(nfull % 2 == 0)
    def _():
        update(2 * npair, 0, True)

    @pl.when(nfull % 2 == 1)
    def _():
        logits(2 * npair + 1, 1)
        update(2 * npair, 0, False)
        update(2 * npair + 1, 1, True)

    outs = [acc_ref[hd] / l_ref[hd] for hd in range(FOX_HEADS)]
    o_ref[...] = jnp.concatenate(outs, axis=0).T.astype(o_ref.dtype)


def _attn_placement():
    w = FOX_HEADS * 128
    e = np.zeros((FOX_WIDTH, w), np.float32)
    cols = np.arange(FOX_WIDTH)
    e[cols, (cols // FOX_HEAD_DIM) * 128 + cols % FOX_HEAD_DIM] = 1.0
    pq, pk = np.zeros((128, w), np.float32), np.zeros((128, w), np.float32)
    oq, ok = np.zeros((1, w), np.float32), np.zeros((1, w), np.float32)
    for hd in range(FOX_HEADS):
        base = hd * 128 + FOX_HEAD_DIM
        for term in range(3):
            pk[term * FOX_HEADS + hd, base + term] = -1.0
            oq[0, base + term] = 1.0
            pq[term * FOX_HEADS + hd, base + 3 + term] = 1.0
            ok[0, base + 3 + term] = 1.0
    return (jnp.asarray(e, BF16), jnp.asarray(pq, BF16), jnp.asarray(pk, BF16),
            jnp.asarray(oq), jnp.asarray(ok))


def _attn(q, k, vt, cs):
    b, l, w = q.shape
    nkv, tk = vt.shape[1], vt.shape[3]
    tq = tk
    wa = FOX_HEADS * 128
    e, pq, pk, oq, ok = _attn_placement()
    full = lambda shape: pl.BlockSpec(shape, lambda bi, i: tuple(0 for _ in shape))
    assert (l // tq) % ATTN_KV_BLOCKS == 0
    est = (2 * (l * w * 2 * 2 + l * 128 * 2) + l * wa * (2 + 4 + 4) + 4 * tq * wa * 4
           + FOX_HEADS * ATTN_KV_BLOCKS * tq * tq * (2 * 4 + 2 + 2 * 4) + FOX_HEADS * 128 * tq * 4)
    return pl.pallas_call(
        _attn_kernel,
        grid=(b, l // tq),
        in_specs=[
            pl.BlockSpec((None, tq, w), lambda bi, i: (bi, i, 0)),
            pl.BlockSpec((None, l, w), lambda bi, i: (bi, 0, 0)),
            pl.BlockSpec((None, nkv, w, tk), lambda bi, i: (bi, 0, 0, 0)),
            pl.BlockSpec((None, tq, 128), lambda bi, i: (bi, i, 0)),
            pl.BlockSpec((None, l, 128), lambda bi, i: (bi, 0, 0)),
            full(e.shape), full(pq.shape), full(pk.shape), full(oq.shape), full(ok.shape),
        ],
        out_specs=pl.BlockSpec((None, tq, w), lambda bi, i: (bi, i, 0)),
        out_shape=jax.ShapeDtypeStruct((b, l, w), BF16),
        scratch_shapes=[pltpu.VMEM((l, wa), BF16), pltpu.VMEM((tq, wa), BF16),
                        pltpu.VMEM((FOX_HEADS, 1, tq), F32), pltpu.VMEM((FOX_HEADS, 1, tq), F32),
                        pltpu.VMEM((FOX_HEADS, FOX_HEAD_DIM, tq), F32),
                        pltpu.VMEM((2, FOX_HEADS, ATTN_KV_BLOCKS * tq, tq), F32),
                        pltpu.VMEM((FOX_HEADS, ATTN_KV_BLOCKS * tq, tq), BF16)],
        compiler_params=pltpu.CompilerParams(
            dimension_semantics=("arbitrary", "arbitrary"), vmem_limit_bytes=_vmem_limit(est)),
        name="fox_attn",
    )(q, k, vt, cs, cs, e, pq, pk, oq, ok)


def _hgrn_levels(c):
    return [c >> (s + 1) for s in range(int(math.log2(c)))]


def _hgrn_level_ids(c):
    t = np.arange(c)[:, None]
    s = np.arange(c)[None, :]
    lvl = np.floor(np.log2(np.maximum(t ^ s, 1))).astype(np.int32) + 1
    return jnp.asarray(np.where(s > t, -1, np.where(s == t, 0, lvl)).astype(np.int32))


def _hgrn_kernel(q_ref, f_ref, i_ref, g_ref, lb_ref, og_ref, lvl_ref, o_ref, st_ref, b_ref):
    c = q_ref.shape[0]

    @pl.when(pl.program_id(1) == 0)
    def _():
        st_ref[...] = jnp.zeros_like(st_ref)

    lb = lb_ref[...]
    f = lb + (1.0 - lb) * _sigmoid(f_ref[...])
    key = 1.0 - f
    q = q_ref[...]
    ti = lax.broadcasted_iota(jnp.int32, (c, c), 0)
    si = lax.broadcasted_iota(jnp.int32, (c, c), 1)
    lower = (si <= ti).astype(BF16)
    b = _dot_exact_lhs(lower, jnp.log(f), terms=2)
    b_ref[...] = b
    row = lax.broadcasted_iota(jnp.int32, b.shape, 0)

    def ref_rows(m):
        parts = [jnp.broadcast_to(b_ref[g0 + m - 1:g0 + m, :], (2 * m, b.shape[1]))
                 for g0 in range(0, c, 2 * m)]
        return jnp.concatenate(parts, axis=0) if len(parts) > 1 else parts[0]

    def level_decay(m):
        if m == 1:
            return jnp.where(row % 2 == 1, f, 1.0)
        if m == 2:
            off = row % 4
            nxt, prv = pltpu.roll(f, c - 1, axis=0), pltpu.roll(f, 1, axis=0)
            return jnp.where(off == 0, nxt, jnp.where(off == 1, 1.0, jnp.where(off == 2, f, f * prv)))
        return jnp.exp2(jnp.abs(b - ref_rows(m)) * (-LOG2E))

    lvl = lvl_ref[...]
    qb, kb = q.astype(BF16), key.astype(BF16)
    scores = [jnp.zeros((c, c), F32) for _ in range(HGRN_HEADS)]
    for idx, m in enumerate([0] + _hgrn_levels(c)[::-1]):
        if m == 0:
            z, y = qb, kb
        else:
            e = level_decay(m).astype(BF16)
            z, y = qb * e, kb * e
        mask = lvl == idx
        for hd in range(HGRN_HEADS):
            sl = slice(hd * HGRN_DIM, (hd + 1) * HGRN_DIM)
            scores[hd] = jnp.where(mask, _dot_nt(z[:, sl], y[:, sl]), scores[hd])

    b_last = b[c - 1:c, :]
    q_in = (q * jnp.exp(b)).astype(BF16)
    k_out = (key * jnp.exp(b_last - b)).astype(BF16)
    v = i_ref[...].astype(BF16)
    g = g_ref[...]
    gate = g * (0.5 * (jnp.tanh(0.5 * g) + 1.0))
    og = og_ref[...]
    decay = jnp.exp(b_last)
    for hd in range(HGRN_HEADS):
        sl = slice(hd * HGRN_DIM, (hd + 1) * HGRN_DIM)
        st = st_ref[hd]
        o = _dot(scores[hd].astype(BF16), v[:, sl]) + _dot_nt(q_in[:, sl], st.astype(BF16))
        st_ref[hd] = st * decay[:, sl] + _dot_tn(v[:, sl], k_out[:, sl])
        o_ref[:, sl] = (_rms_norm(o, og[:, sl]) * gate[:, sl]).astype(o_ref.dtype)


def _hgrn(hg, lb, og):
    b, l, _ = hg.shape
    c = min(HGRN_CHUNK, l)
    w = HGRN_WIDTH
    part = lambda p: pl.BlockSpec((None, c, w), lambda bi, i, p=p: (bi, i, p))
    est = 2 * 5 * c * w * 4 + 16 * c * w * 4 + HGRN_HEADS * HGRN_DIM * HGRN_DIM * 4
    return pl.pallas_call(
        _hgrn_kernel,
        grid=(b, l // c),
        in_specs=[part(0), part(1), part(2), part(3),
                  pl.BlockSpec((1, w), lambda bi, i: (0, 0)),
                  pl.BlockSpec((1, w), lambda bi, i: (0, 0)),
                  pl.BlockSpec((c, c), lambda bi, i: (0, 0))],
        out_specs=pl.BlockSpec((None, c, w), lambda bi, i: (bi, i, 0)),
        out_shape=jax.ShapeDtypeStruct((b, l, w), BF16),
        scratch_shapes=[pltpu.VMEM((HGRN_HEADS, HGRN_DIM, HGRN_DIM), F32),
                        pltpu.VMEM((c, w), F32)],
        compiler_params=pltpu.CompilerParams(
            dimension_semantics=("arbitrary", "arbitrary"), vmem_limit_bytes=_vmem_limit(est)),
        name="hgrn2",
    )(hg, hg, hg, hg, lb, og, _hgrn_level_ids(c))


def _s5_kernel(u_ref, kd_ref, pin_ref, pout_ref, ar_ref, ai_ref, y_ref, xr_ref, xi_ref):
    nc, nb, cs, w = u_ref.shape
    ns = ar_ref.shape[1]

    @pl.when(pl.program_id(0) == 0)
    def _():
        xr_ref[...] = jnp.zeros_like(xr_ref)
        xi_ref[...] = jnp.zeros_like(xi_ref)

    us = [u_ref[:, :, s, :].reshape(nc * nb, w).astype(BF16) for s in range(cs)]
    win = _dot(us[0], pin_ref[0])
    for s in range(1, cs):
        win = win + _dot(us[s], pin_ref[s])

    ar, ai = ar_ref[...], ai_ref[...]
    xr, xi = xr_ref[...], xi_ref[...]
    prev_r, prev_i = [], []
    for c in range(nc):
        prev_r.append(xr)
        prev_i.append(xi)
        wr = win[c * nb:(c + 1) * nb, :ns]
        wi = win[c * nb:(c + 1) * nb, ns:]
        xr, xi = ar * xr - ai * xi + wr, ar * xi + ai * xr + wi
    xr_ref[...] = xr
    xi_ref[...] = xi
    xp = jnp.concatenate([jnp.concatenate(prev_r, axis=0), jnp.concatenate(prev_i, axis=0)],
                         axis=1).astype(BF16)

    for t in range(cs):
        lags = jnp.concatenate([us[t - d] for d in range(t + 1)], axis=1) if t else us[0]
        yt = _dot(xp, pout_ref[t]) + _dot(lags, kd_ref[0:(t + 1) * w, :])
        y_ref[:, :, t, :] = yt.reshape(nc, nb, w)


def _s5(u4, kd, pin, pout, ar, ai):
    nchunk, nb, cs, w = u4.shape
    nc = min(S5_TILE // cs, nchunk)
    const = dict(pipeline_mode=pl.Buffered(1))
    full = lambda shape: pl.BlockSpec(shape, lambda i: tuple(0 for _ in shape), **const)
    rows = nc * cs * nb
    chunked = pl.BlockSpec((nc, nb, cs, w), lambda i: (i, 0, 0, 0))
    est = (4 * rows * w * 4 + (kd.size + pin.size + pout.size) * 2 + rows // cs * 2 * S5_NSTATE * (4 + 4 + 2)
           + rows * w * (2 + 2 + 4))
    return pl.pallas_call(
        _s5_kernel,
        grid=(nchunk // nc,),
        in_specs=[chunked, full(kd.shape), full(pin.shape), full(pout.shape), full(ar.shape), full(ai.shape)],
        out_specs=chunked,
        out_shape=jax.ShapeDtypeStruct(u4.shape, F32),
        scratch_shapes=[pltpu.VMEM((nb, S5_NSTATE), F32), pltpu.VMEM((nb, S5_NSTATE), F32)],
        compiler_params=pltpu.CompilerParams(
            dimension_semantics=("arbitrary",), vmem_limit_bytes=_vmem_limit(est)),
        name="s5",
    )(u4, kd, pin, pout, ar, ai)


def _s5_operators(lam_re, lam_im, log_step, b_re, b_im, c_re, c_im):
    lr, li = lam_re.astype(F32), lam_im.astype(F32)
    dt = jnp.exp(log_step.astype(F32))[:, None]
    mag = jnp.exp(lr * dt)
    lbar_re, lbar_im = mag * jnp.cos(li * dt), mag * jnp.sin(li * dt)
    nr, ni = lbar_re - 1.0, lbar_im
    den = lr * lr + li * li
    coef_re = (nr * lr + ni * li) / den
    coef_im = (ni * lr - nr * li) / den
    br, bi = b_re.astype(F32), b_im.astype(F32)
    bbar_re = coef_re[..., None] * br - coef_im[..., None] * bi
    bbar_im = coef_re[..., None] * bi + coef_im[..., None] * br
    cr, ci = c_re.astype(F32), c_im.astype(F32)

    pr, pi = [jnp.ones_like(lbar_re)], [jnp.zeros_like(lbar_re)]
    for _ in range(S5_CHUNK):
        pr, pi = (pr + [pr[-1] * lbar_re - pi[-1] * lbar_im],
                  pi + [pr[-1] * lbar_im + pi[-1] * lbar_re])
    pw_re, pw_im = jnp.stack(pr), jnp.stack(pi)
    cs = S5_CHUNK

    def block_diag(compact, rows_group, cols_group, reps):
        tiled = jnp.tile(compact, reps)
        r = lax.broadcasted_iota(jnp.int32, tiled.shape, tiled.ndim - 2)
        c = lax.broadcasted_iota(jnp.int32, tiled.shape, tiled.ndim - 1)
        return jnp.where(rows_group(r) == cols_group(c), tiled, 0.0).astype(BF16)

    lane_group = lambda c: c // S5_GROUP
    state_group = lambda c: (c % S5_NSTATE) // S5_STATE

    ab_re = pw_re[:cs, :, :, None] * bbar_re - pw_im[:cs, :, :, None] * bbar_im
    ab_im = pw_re[:cs, :, :, None] * bbar_im + pw_im[:cs, :, :, None] * bbar_re
    taps = jnp.sum(cr[None, :, :, :, None] * ab_re[:, :, None] - ci[None, :, :, :, None] * ab_im[:, :, None],
                   axis=3)
    kd = block_diag(taps.transpose(0, 1, 3, 2).reshape(cs * S5_WIDTH, S5_GROUP),
                    lambda r: (r // S5_GROUP) % S5_GROUPS, lane_group, (1, S5_GROUPS))

    rev = cs - 1 - jnp.arange(cs)
    ab = jnp.stack([ab_re[rev], ab_im[rev]], axis=1)
    pin = block_diag(ab.transpose(0, 4, 1, 2, 3).reshape(cs, S5_GROUP, 2 * S5_NSTATE),
                     lane_group, state_group, (1, S5_GROUPS, 1))

    qr, qi = pw_re[1:], pw_im[1:]
    po_re = cr[None] * qr[:, :, None, :] - ci[None] * qi[:, :, None, :]
    po_im = -(cr[None] * qi[:, :, None, :] + ci[None] * qr[:, :, None, :])
    po = jnp.stack([po_re, po_im], axis=1)
    pout = block_diag(po.transpose(0, 1, 2, 4, 3).reshape(cs, 2 * S5_NSTATE, S5_GROUP),
                      state_group, lane_group, (1, 1, S5_GROUPS))
    a_re = pw_re[cs].reshape(1, S5_NSTATE)
    a_im = pw_im[cs].reshape(1, S5_NSTATE)
    return kd, pin, pout, a_re, a_im


def _out_ffn_kernel(x_ref, oa_ref, ob_ref, y_ref, u_ref, d_ref, gw_ref, gb_ref, wa_ref, wb_ref, wc_ref,
                    g_ref, wg_ref, wu_ref, wd_ref, o_ref):
    tm = x_ref.shape[0]
    y = y_ref[...].reshape(tm, S5_WIDTH) + d_ref[...] * u_ref[...].reshape(tm, S5_WIDTH)
    z = 0.5 * y * (1.0 + jnp.tanh(math.sqrt(2.0 / math.pi) * (y + 0.044715 * (y * y * y))))
    oc = z * _sigmoid(_dot(z.astype(BF16), gw_ref[...]) + gb_ref[...])
    x = (x_ref[...] + _dot(oa_ref[...], wa_ref[...]) + _dot(ob_ref[...], wb_ref[...])
         + _dot(oc.astype(BF16), wc_ref[...]))
    h = _rms_norm(x, g_ref[...]).astype(BF16)
    gate = _dot(h, wg_ref[...])
    up = _dot(h, wu_ref[...])
    act = (gate * _sigmoid(gate) * up).astype(BF16)
    o_ref[...] = x + 0.5 * _dot(act, wd_ref[...])


def _out_ffn(x, oa, ob, y_lb, u_lb, dvec, gw, gb, wa, wb, wc, gain, wg, wu, wd):
    b, l, d = x.shape
    f = wg.shape[1]
    tm = min(OUT_ROWS, l)
    rows = lambda w: pl.BlockSpec((None, tm, w), lambda bi, i: (bi, i, 0))
    token_major = pl.BlockSpec((tm // S5_CHUNK, None, S5_CHUNK, S5_WIDTH), lambda bi, i: (i, bi, 0, 0))
    full = lambda shape: pl.BlockSpec(shape, lambda bi, i: tuple(0 for _ in shape),
                                      pipeline_mode=pl.Buffered(1))
    est = (2 * (2 * tm * d * 4 + tm * (256 * 2 + 512 * 2 + 256 * 8)) + 2 * (d * d + 256 * 256) + 3 * d * f * 2
           + tm * f * (4 + 4 + 2 + 4) + 4 * tm * d * 4)
    return pl.pallas_call(
        _out_ffn_kernel,
        grid=(b, l // tm),
        in_specs=[rows(d), rows(FOX_WIDTH), rows(HGRN_WIDTH), token_major, token_major,
                  full((1, S5_WIDTH)), full(gw.shape), full((1, S5_WIDTH)),
                  full(wa.shape), full(wb.shape), full(wc.shape),
                  full((1, d)), full(wg.shape), full(wu.shape), full(wd.shape)],
        out_specs=rows(d),
        out_shape=jax.ShapeDtypeStruct((b, l, d), F32),
        compiler_params=pltpu.CompilerParams(
            dimension_semantics=("arbitrary", "arbitrary"), vmem_limit_bytes=_vmem_limit(est)),
        name="out_ffn",
    )(x, oa, ob, y_lb, u_lb, dvec, gw, gb, wa, wb, wc, gain, wg, wu, wd)


def kernel(x, ffn1_norm, ffn1_w_gate, ffn1_w_up, ffn1_w_down, mix_norm, w_in, fox_f_bias, fox_q_gain, fox_k_gain, hgrn_lb, hgrn_o_gain, s5_lambda_re, s5_lambda_im, s5_log_step, s5_B_re, s5_B_im, s5_C_re, s5_C_im, s5_D, s5_glu_w, s5_glu_b, w_out, ffn2_norm, ffn2_w_gate, ffn2_w_up, ffn2_w_down):
    bsz, seq, d = x.shape
    depth = w_in.shape[0]
    t = bsz * seq

    lb_all = jnp.cumsum(jax.nn.softmax(hgrn_lb.astype(F32), axis=0), axis=0)
    lb_all = lb_all - lb_all[0:1]

    o_q, o_k, o_v = 0, FOX_WIDTH, 2 * FOX_WIDTH
    o_f = 3 * FOX_WIDTH
    o_h = o_f + FOX_HEADS
    o_u = o_h + 4 * HGRN_WIDTH

    for l in range(depth):
        x = _ffn(x.reshape(t, d), ffn1_norm[l][None], ffn1_w_gate[l].astype(BF16),
                 ffn1_w_up[l].astype(BF16), ffn1_w_down[l].astype(BF16)).reshape(bsz, seq, d)

        w = w_in[l]
        wf = jnp.zeros((d, 128), F32).at[:, :3 * FOX_HEADS].set(jnp.tile(w[:, o_f:o_h], (1, 3))).astype(BF16)
        fb = jnp.zeros((1, 128), F32).at[0, :3 * FOX_HEADS].set(jnp.tile(fox_f_bias[l].astype(F32), 3))
        q, k, vt, cs, hg, u_lb = _proj(
            x, mix_norm[l][None], w[:, o_q:o_v].astype(BF16), w[:, o_v:o_f].T.astype(BF16), wf,
            w[:, o_h:o_u].astype(BF16), w[:, o_u:].astype(BF16),
            jnp.tile(fox_q_gain[l].astype(F32), FOX_HEADS)[None],
            jnp.tile(fox_k_gain[l].astype(F32), FOX_HEADS)[None], fb)

        o_a = _attn(q, k, vt, cs)
        o_b = _hgrn(hg, lb_all[l][None], hgrn_o_gain[l].astype(F32)[None])
        ops = _s5_operators(s5_lambda_re[l], s5_lambda_im[l], s5_log_step[l],
                            s5_B_re[l], s5_B_im[l], s5_C_re[l], s5_C_im[l])
        y_lb = _s5(u_lb, *ops)

        wo = w_out[l].astype(BF16)
        x = _out_ffn(x, o_a, o_b, y_lb, u_lb, s5_D[l].astype(F32)[None], s5_glu_w[l].astype(BF16),
                     s5_glu_b[l].astype(F32)[None], wo[:FOX_WIDTH], wo[FOX_WIDTH:FOX_WIDTH + HGRN_WIDTH],
                     wo[FOX_WIDTH + HGRN_WIDTH:], ffn2_norm[l][None], ffn2_w_gate[l].astype(BF16),
                     ffn2_w_up[l].astype(BF16), ffn2_w_down[l].astype(BF16))
    return x
```

```python
import functools
import math

import jax
import jax.numpy as jnp
import numpy as np
from jax import lax
from jax.experimental import pallas as pl
from jax.experimental.pallas import tpu as pltpu

F32 = jnp.float32
BF16 = jnp.bfloat16

EPS = 1e-6
D_MODEL = 1024
D_FF = 2816
FOX_HEADS = 4
FOX_HEAD_DIM = 64
FOX_WIDTH = FOX_HEADS * FOX_HEAD_DIM
HGRN_HEADS = 4
HGRN_DIM = 128
HGRN_WIDTH = HGRN_HEADS * HGRN_DIM
S5_GROUPS = 16
S5_GROUP = 16
S5_STATE = 64
S5_WIDTH = S5_GROUPS * S5_GROUP
S5_NSTATE = S5_GROUPS * S5_STATE
S5_CHUNK = 8
S5_TILE = 128
LOG2E = math.log2(math.e)

V7X_VMEM_BYTES = 64 * 1024 * 1024
V7X_SUBLANES = 8

ATTN_BLOCK = 256
ATTN_KV_BLOCKS = 2
ATTN_MAX_FREE_LOGIT = 40.0
HGRN_CHUNK = 128
FFN_ROWS = 512
PROJ_ROWS = 512
OUT_ROWS = 512


def _vmem_limit(estimate_bytes):
    return int(min(estimate_bytes * 5 // 4 + (4 << 20), V7X_VMEM_BYTES - (4 << 20)))


def _rms_norm(x, gain):
    ms = jnp.mean(x * x, axis=-1, keepdims=True)
    return x * lax.rsqrt(ms + EPS) * gain


def _dot(a, b):
    return jnp.dot(a, b, preferred_element_type=F32)


def _dot_nt(a, b):
    return lax.dot_general(a, b, (((1,), (1,)), ((), ())), preferred_element_type=F32)


def _dot_tn(a, b):
    return lax.dot_general(a, b, (((0,), (0,)), ((), ())), preferred_element_type=F32)


def _split3(a):
    a1 = a.astype(BF16)
    r1 = a - a1.astype(F32)
    a2 = r1.astype(BF16)
    a3 = (r1 - a2.astype(F32)).astype(BF16)
    return a1, a2, a3


def _dot_exact_rhs(a, rhs, terms=3):
    parts = _split3(a)[:terms]
    out = _dot(parts[0], rhs)
    for part in parts[1:]:
        out = out + _dot(part, rhs)
    return out


def _dot_exact_lhs(lhs, a, terms=3):
    parts = _split3(a)[:terms]
    out = _dot(lhs, parts[0])
    for part in parts[1:]:
        out = out + _dot(lhs, part)
    return out


def _sigmoid(x):
    return 1.0 / (1.0 + jnp.exp(-x))


def _ffn_kernel(x_ref, g_ref, wg_ref, wu_ref, wd_ref, o_ref):
    x = x_ref[...]
    h = _rms_norm(x, g_ref[...]).astype(BF16)
    gate = _dot(h, wg_ref[...])
    up = _dot(h, wu_ref[...])
    act = (gate * _sigmoid(gate) * up).astype(BF16)
    o_ref[...] = x + 0.5 * _dot(act, wd_ref[...])


def _ffn(x2d, gain, wg, wu, wd):
    t, d = x2d.shape
    f = wg.shape[1]
    tm = min(FFN_ROWS, t)
    const = dict(pipeline_mode=pl.Buffered(1))
    est = 3 * d * f * 2 + 4 * tm * d * 4 + tm * f * (4 + 4 + 2 + 4) + tm * d * 4
    return pl.pallas_call(
        _ffn_kernel,
        grid=(t // tm,),
        in_specs=[
            pl.BlockSpec((tm, d), lambda i: (i, 0)),
            pl.BlockSpec((1, d), lambda i: (0, 0)),
            pl.BlockSpec((d, f), lambda i: (0, 0), **const),
            pl.BlockSpec((d, f), lambda i: (0, 0), **const),
            pl.BlockSpec((f, d), lambda i: (0, 0), **const),
        ],
        out_specs=pl.BlockSpec((tm, d), lambda i: (i, 0)),
        out_shape=jax.ShapeDtypeStruct((t, d), F32),
        compiler_params=pltpu.CompilerParams(
            dimension_semantics=("arbitrary",), vmem_limit_bytes=_vmem_limit(est)),
        name="ffn",
    )(x2d, gain, wg, wu, wd)


def _proj_kernel(x_ref, g_ref, wqk_ref, wvt_ref, wf_ref, wh_ref, wu_ref, qg_ref, kg_ref,
                 fb_ref, q_ref, k_ref, vt_ref, cs_ref, hg_ref, u_ref, carry_ref):
    tm = x_ref.shape[0]
    tk = vt_ref.shape[2]

    @pl.when(pl.program_id(1) == 0)
    def _():
        carry_ref[...] = jnp.zeros_like(carry_ref)

    h = _rms_norm(x_ref[...], g_ref[...]).astype(BF16)

    li = lax.broadcasted_iota(jnp.int32, (FOX_WIDTH, FOX_WIDTH), 0) // FOX_HEAD_DIM
    lj = lax.broadcasted_iota(jnp.int32, (FOX_WIDTH, FOX_WIDTH), 1) // FOX_HEAD_DIM
    group = (li == lj).astype(BF16)

    def head_norm(a, gain):
        ms = _dot_exact_rhs(a * a, group, terms=2) * (1.0 / FOX_HEAD_DIM)
        return a * lax.rsqrt(ms + EPS) * gain

    qk = _dot(h, wqk_ref[...])
    q_ref[...] = (head_norm(qk[:, :FOX_WIDTH], qg_ref[...]) * (LOG2E / math.sqrt(FOX_HEAD_DIM))).astype(BF16)
    k_ref[...] = head_norm(qk[:, FOX_WIDTH:], kg_ref[...]).astype(BF16)
    vt = _dot_nt(wvt_ref[...], h).astype(BF16)
    for blk in range(tm // tk):
        vt_ref[blk] = vt[:, blk * tk:(blk + 1) * tk]

    fl = _dot(h, wf_ref[...]) + fb_ref[...]
    ls = jnp.minimum(fl, 0.0) - jnp.log1p(jnp.exp(-jnp.abs(fl)))
    ri = lax.broadcasted_iota(jnp.int32, (tm, tm), 0)
    ci = lax.broadcasted_iota(jnp.int32, (tm, tm), 1)
    lower = (ci <= ri).astype(BF16)
    c = _dot_exact_lhs(lower, ls) + carry_ref[0:1, :]
    carry_ref[...] = jnp.broadcast_to(c[tm - 1:tm, :], carry_ref.shape)
    c1, c2, c3 = _split3(c * LOG2E)
    lane = lax.broadcasted_iota(jnp.int32, c.shape, 1)
    cs_ref[...] = jnp.where(lane < FOX_HEADS, c1, jnp.where(lane < 2 * FOX_HEADS, c2, c3))

    hg_ref[...] = _dot(h, wh_ref[...])
    u_ref[...] = _dot(h, wu_ref[...]).reshape(u_ref.shape)


def _proj(x, gain, wqk, wvt, wf, wh, wu, qg, kg, fb):
    b, l, d = x.shape
    tm = min(PROJ_ROWS, l)
    tk = min(ATTN_BLOCK, l)
    hw = wh.shape[1]
    full = lambda shape: pl.BlockSpec(shape, lambda bi, i: tuple(0 for _ in shape))
    est = 2 * (d * (512 + 256 + 128 + hw + 256) * 2) + 2 * tm * d * 4 + 2 * tm * (hw + 256 + 512) * 4 + tm * hw * 4
    return pl.pallas_call(
        _proj_kernel,
        grid=(b, l // tm),
        in_specs=[
            pl.BlockSpec((None, tm, d), lambda bi, i: (bi, i, 0)),
            full((1, d)), full(wqk.shape), full(wvt.shape), full(wf.shape), full(wh.shape),
            full(wu.shape), full((1, FOX_WIDTH)), full((1, FOX_WIDTH)), full((1, 128)),
        ],
        out_specs=[
            pl.BlockSpec((None, tm, FOX_WIDTH), lambda bi, i: (bi, i, 0)),
            pl.BlockSpec((None, tm, FOX_WIDTH), lambda bi, i: (bi, i, 0)),
            pl.BlockSpec((None, tm // tk, FOX_WIDTH, tk), lambda bi, i: (bi, i, 0, 0)),
            pl.BlockSpec((None, tm, 128), lambda bi, i: (bi, i, 0)),
            pl.BlockSpec((None, tm, hw), lambda bi, i: (bi, i, 0)),
            pl.BlockSpec((tm // S5_CHUNK, None, S5_CHUNK, S5_WIDTH), lambda bi, i: (i, bi, 0, 0)),
        ],
        out_shape=[
            jax.ShapeDtypeStruct((b, l, FOX_WIDTH), BF16),
            jax.ShapeDtypeStruct((b, l, FOX_WIDTH), BF16),
            jax.ShapeDtypeStruct((b, l // tk, FOX_WIDTH, tk), BF16),
            jax.ShapeDtypeStruct((b, l, 128), BF16),
            jax.ShapeDtypeStruct((b, l, hw), F32),
            jax.ShapeDtypeStruct((l // S5_CHUNK, b, S5_CHUNK, S5_WIDTH), F32),
        ],
        scratch_shapes=[pltpu.VMEM((V7X_SUBLANES, 128), F32)],
        compiler_params=pltpu.CompilerParams(
            dimension_semantics=("arbitrary", "arbitrary"), vmem_limit_bytes=_vmem_limit(est)),
        name="proj",
    )(x, gain, wqk, wvt, wf, wh, wu, qg, kg, fb)


def _attn_kernel(bounded_ref, q_ref, k_ref, vt_ref, cs_ref, e_ref, pq_ref, pk_ref, oq_ref, ok_ref,
                 o_ref, ka_ref, qa_ref, m_ref, l_ref, acc_ref, s_ref, p_ref):
    tq = o_ref.shape[0]
    i = pl.program_id(1)

    @pl.when(i == 0)
    def _():
        cs = cs_ref[...]
        ka_ref[...] = (_dot(k_ref[...], e_ref[...]) + _dot(cs, pk_ref[...]) + ok_ref[...]).astype(BF16)
        qa_ref[...] = (_dot(q_ref[...], e_ref[...]) + _dot(cs, pq_ref[...]) + oq_ref[...]).astype(BF16)

    l_ref[...] = jnp.zeros_like(l_ref)
    acc_ref[...] = jnp.zeros_like(acc_ref)
    nblk = ATTN_KV_BLOCKS
    tk = nblk * tq
    nfull = i // nblk
    qrows = pl.ds(pl.multiple_of(i * tq, tq), tq)
    key_minus_qry = (lax.broadcasted_iota(jnp.int32, (tk, tq), 0)
                     - lax.broadcasted_iota(jnp.int32, (tk, tq), 1))

    def causal(s, jj):
        return jnp.where(key_minus_qry <= (i - jj * nblk) * tq, s, -jnp.inf)

    def pv_dot(jj, hd, p):
        head_rows = slice(hd * FOX_HEAD_DIM, (hd + 1) * FOX_HEAD_DIM)
        pv = _dot(vt_ref[jj * nblk, head_rows, :], p[0:tq, :])
        for bk in range(1, nblk):
            pv = pv + _dot(vt_ref[jj * nblk + bk, head_rows, :], p[bk * tq:(bk + 1) * tq, :])
        return pv

    def bounded_steps(steps):
        for slot, jj in enumerate(steps):
            rows = pl.ds(pl.multiple_of(jj * tk, tk), tk)
            for hd in range(FOX_HEADS):
                slab = slice(hd * 128, (hd + 1) * 128)
                p = jnp.exp2(causal(_dot_nt(ka_ref[rows, slab], qa_ref[qrows, slab]), jj))
                l_ref[hd] = l_ref[hd] + jnp.sum(p, axis=0, keepdims=True)
                p_ref[slot, hd] = p.astype(BF16)
        for hd in range(FOX_HEADS):
            pv = pv_dot(steps[0], hd, p_ref[0, hd])
            for slot, jj in enumerate(steps[1:], 1):
                pv = pv + pv_dot(jj, hd, p_ref[slot, hd])
            acc_ref[hd] = acc_ref[hd] + pv

    @pl.when(bounded_ref[0] == 1)
    def _():
        def pair(kk, carry):
            bounded_steps([2 * kk, 2 * kk + 1])
            return carry

        lax.fori_loop(0, (nfull + 1) // 2, pair, 0)

        @pl.when(nfull % 2 == 0)
        def _():
            bounded_steps([nfull])

    @pl.when(bounded_ref[0] == 0)
    def _():
        _attn_online(i, nfull, tk, qrows, causal, pv_dot, ka_ref, qa_ref, m_ref, l_ref, acc_ref,
                     s_ref, p_ref.at[0])

    outs = [acc_ref[hd] / l_ref[hd] for hd in range(FOX_HEADS)]
    o_ref[...] = jnp.concatenate(outs, axis=0).T.astype(o_ref.dtype)


def _attn_online(i, nfull, tk, qrows, causal, pv_dot, ka_ref, qa_ref, m_ref, l_ref, acc_ref, s_ref, p_ref):
    m_ref[...] = jnp.full(m_ref.shape, -jnp.inf, F32)

    def logits(jj, slot):
        rows = pl.ds(pl.multiple_of(jj * tk, tk), tk)
        for hd in range(FOX_HEADS):
            slab = slice(hd * 128, (hd + 1) * 128)
            s_ref[slot, hd] = _dot_nt(ka_ref[rows, slab], qa_ref[qrows, slab])

    def update(jj, slot, last):
        alphas = []
        for hd in range(FOX_HEADS):
            s = s_ref[slot, hd]
            if last:
                s = causal(s, jj)
            m_old = m_ref[hd]
            m_new = jnp.maximum(m_old, jnp.max(s, axis=0, keepdims=True))
            alpha = jnp.exp2(m_old - m_new)
            p = jnp.exp2(s - m_new)
            m_ref[hd] = m_new
            l_ref[hd] = alpha * l_ref[hd] + jnp.sum(p, axis=0, keepdims=True)
            p_ref[hd] = p.astype(BF16)
            alphas.append(alpha)
        for hd in range(FOX_HEADS):
            acc_ref[hd] = alphas[hd] * acc_ref[hd] + pv_dot(jj, hd, p_ref[hd])

    def pair(kk, carry):
        logits(2 * kk + 1, 1)
        update(2 * kk, 0, False)
        logits(2 * kk + 2, 0)
        update(2 * kk + 1, 1, False)
        return carry

    logits(0, 0)
    npair = nfull // 2
    lax.fori_loop(0, npair, pair, 0)

    @pl.when(nfull % 2 == 0)
    def _():
        update(2 * npair, 0, True)

    @pl.when(nfull % 2 == 1)
    def _():
        logits(2 * npair + 1, 1)
        update(2 * npair, 0, False)
        update(2 * npair + 1, 1, True)


def _attn_placement():
    w = FOX_HEADS * 128
    e = np.zeros((FOX_WIDTH, w), np.float32)
    cols = np.arange(FOX_WIDTH)
    e[cols, (cols // FOX_HEAD_DIM) * 128 + cols % FOX_HEAD_DIM] = 1.0
    pq, pk = np.zeros((128, w), np.float32), np.zeros((128, w), np.float32)
    oq, ok = np.zeros((1, w), np.float32), np.zeros((1, w), np.float32)
    for hd in range(FOX_HEADS):
        base = hd * 128 + FOX_HEAD_DIM
        for term in range(3):
            pk[term * FOX_HEADS + hd, base + term] = -1.0
            oq[0, base + term] = 1.0
            pq[term * FOX_HEADS + hd, base + 3 + term] = 1.0
            ok[0, base + 3 + term] = 1.0
    return (jnp.asarray(e, BF16), jnp.asarray(pq, BF16), jnp.asarray(pk, BF16),
            jnp.asarray(oq), jnp.asarray(ok))


def _attn(q, k, vt, cs, q_gain, k_gain):
    b, l, w = q.shape
    nkv, tk = vt.shape[1], vt.shape[3]
    tq = tk
    wa = FOX_HEADS * 128
    e, pq, pk, oq, ok = _attn_placement()
    full = lambda shape: pl.BlockSpec(shape, lambda bi, i: tuple(0 for _ in shape))
    seq = lambda width: pl.BlockSpec((None, l, width), lambda bi, i: (bi, 0, 0))
    assert (l // tq) % ATTN_KV_BLOCKS == 0
    logit_bound = (1.01 * FOX_HEAD_DIM * LOG2E / math.sqrt(FOX_HEAD_DIM)
                   * jnp.max(jnp.abs(q_gain.astype(F32))) * jnp.max(jnp.abs(k_gain.astype(F32))))
    bounded = (logit_bound < ATTN_MAX_FREE_LOGIT).astype(jnp.int32).reshape(1)
    est = (2 * (l * w * 2 * 2 + l * 128 * 2) + 2 * l * wa * (2 + 4 + 4)
           + FOX_HEADS * ATTN_KV_BLOCKS * tq * tq * (2 * 4 + 2 * 2 + 2 * 4) + FOX_HEADS * 128 * tq * 4)
    return pl.pallas_call(
        _attn_kernel,
        grid=(b, l // tq),
        in_specs=[
            pl.BlockSpec(memory_space=pltpu.SMEM),
            seq(w), seq(w),
            pl.BlockSpec((None, nkv, w, tk), lambda bi, i: (bi, 0, 0, 0)),
            seq(128),
            full(e.shape), full(pq.shape), full(pk.shape), full(oq.shape), full(ok.shape),
        ],
        out_specs=pl.BlockSpec((None, tq, w), lambda bi, i: (bi, i, 0)),
        out_shape=jax.ShapeDtypeStruct((b, l, w), BF16),
        scratch_shapes=[pltpu.VMEM((l, wa), BF16), pltpu.VMEM((l, wa), BF16),
                        pltpu.VMEM((FOX_HEADS, 1, tq), F32), pltpu.VMEM((FOX_HEADS, 1, tq), F32),
                        pltpu.VMEM((FOX_HEADS, FOX_HEAD_DIM, tq), F32),
                        pltpu.VMEM((2, FOX_HEADS, ATTN_KV_BLOCKS * tq, tq), F32),
                        pltpu.VMEM((2, FOX_HEADS, ATTN_KV_BLOCKS * tq, tq), BF16)],
        compiler_params=pltpu.CompilerParams(
            dimension_semantics=("arbitrary", "arbitrary"), vmem_limit_bytes=_vmem_limit(est)),
        name="fox_attn",
    )(bounded, q, k, vt, cs, e, pq, pk, oq, ok)


def _hgrn_levels(c):
    return [c >> (s + 1) for s in range(int(math.log2(c)))]


def _hgrn_level_ids(c):
    t = np.arange(c)[:, None]
    s = np.arange(c)[None, :]
    lvl = np.floor(np.log2(np.maximum(t ^ s, 1))).astype(np.int32) + 1
    return jnp.asarray(np.where(s > t, -1, np.where(s == t, 0, lvl)).astype(np.int32))


def _hgrn_kernel(q_ref, f_ref, i_ref, g_ref, lb_ref, og_ref, lvl_ref, o_ref, st_ref, b_ref):
    c = q_ref.shape[0]

    @pl.when(pl.program_id(1) == 0)
    def _():
        st_ref[...] = jnp.zeros_like(st_ref)

    lb = lb_ref[...]
    f = lb + (1.0 - lb) * _sigmoid(f_ref[...])
    key = 1.0 - f
    q = q_ref[...]
    ti = lax.broadcasted_iota(jnp.int32, (c, c), 0)
    si = lax.broadcasted_iota(jnp.int32, (c, c), 1)
    lower = (si <= ti).astype(BF16)
    b = _dot_exact_lhs(lower, jnp.log(f), terms=2)
    b_ref[...] = b
    row = lax.broadcasted_iota(jnp.int32, b.shape, 0)

    def ref_rows(m):
        parts = [jnp.broadcast_to(b_ref[g0 + m - 1:g0 + m, :], (2 * m, b.shape[1]))
                 for g0 in range(0, c, 2 * m)]
        return jnp.concatenate(parts, axis=0) if len(parts) > 1 else parts[0]

    def level_decay(m):
        if m == 1:
            return jnp.where(row % 2 == 1, f, 1.0)
        if m == 2:
            off = row % 4
            nxt, prv = pltpu.roll(f, c - 1, axis=0), pltpu.roll(f, 1, axis=0)
            return jnp.where(off == 0, nxt, jnp.where(off == 1, 1.0, jnp.where(off == 2, f, f * prv)))
        return jnp.exp2(jnp.abs(b - ref_rows(m)) * (-LOG2E))

    lvl = lvl_ref[...]
    qb, kb = q.astype(BF16), key.astype(BF16)
    scores = [jnp.zeros((c, c), F32) for _ in range(HGRN_HEADS)]
    for idx, m in enumerate([0] + _hgrn_levels(c)[::-1]):
        if m == 0:
            z, y = qb, kb
        else:
            e = level_decay(m).astype(BF16)
            z, y = qb * e, kb * e
        mask = lvl == idx
        for hd in range(HGRN_HEADS):
            sl = slice(hd * HGRN_DIM, (hd + 1) * HGRN_DIM)
            scores[hd] = jnp.where(mask, _dot_nt(z[:, sl], y[:, sl]), scores[hd])

    b_last = b[c - 1:c, :]
    q_in = (q * jnp.exp(b)).astype(BF16)
    k_out = (key * jnp.exp(b_last - b)).astype(BF16)
    v = i_ref[...].astype(BF16)
    g = g_ref[...]
    gate = g * (0.5 * (jnp.tanh(0.5 * g) + 1.0))
    og = og_ref[...]
    decay = jnp.exp(b_last)
    for hd in range(HGRN_HEADS):
        sl = slice(hd * HGRN_DIM, (hd + 1) * HGRN_DIM)
        st = st_ref[hd]
        o = _dot(scores[hd].astype(BF16), v[:, sl]) + _dot_nt(q_in[:, sl], st.astype(BF16))
        st_ref[hd] = st * decay[:, sl] + _dot_tn(v[:, sl], k_out[:, sl])
        o_ref[:, sl] = (_rms_norm(o, og[:, sl]) * gate[:, sl]).astype(o_ref.dtype)


def _hgrn(hg, lb, og):
    b, l, _ = hg.shape
    c = min(HGRN_CHUNK, l)
    w = HGRN_WIDTH
    part = lambda p: pl.BlockSpec((None, c, w), lambda bi, i, p=p: (bi, i, p))
    est = 2 * 5 * c * w * 4 + 16 * c * w * 4 + HGRN_HEADS * HGRN_DIM * HGRN_DIM * 4
    return pl.pallas_call(
        _hgrn_kernel,
        grid=(b, l // c),
        in_specs=[part(0), part(1), part(2), part(3),
                  pl.BlockSpec((1, w), lambda bi, i: (0, 0)),
                  pl.BlockSpec((1, w), lambda bi, i: (0, 0)),
                  pl.BlockSpec((c, c), lambda bi, i: (0, 0))],
        out_specs=pl.BlockSpec((None, c, w), lambda bi, i: (bi, i, 0)),
        out_shape=jax.ShapeDtypeStruct((b, l, w), BF16),
        scratch_shapes=[pltpu.VMEM((HGRN_HEADS, HGRN_DIM, HGRN_DIM), F32),
                        pltpu.VMEM((c, w), F32)],
        compiler_params=pltpu.CompilerParams(
            dimension_semantics=("arbitrary", "arbitrary"), vmem_limit_bytes=_vmem_limit(est)),
        name="hgrn2",
    )(hg, hg, hg, hg, lb, og, _hgrn_level_ids(c))


def _s5_kernel(u_ref, kd_ref, pin_ref, pout_ref, ar_ref, ai_ref, y_ref, xr_ref, xi_ref):
    nc, nb, cs, w = u_ref.shape
    ns = ar_ref.shape[1]

    @pl.when(pl.program_id(0) == 0)
    def _():
        xr_ref[...] = jnp.zeros_like(xr_ref)
        xi_ref[...] = jnp.zeros_like(xi_ref)

    us = [u_ref[:, :, s, :].reshape(nc * nb, w).astype(BF16) for s in range(cs)]
    win = _dot(us[0], pin_ref[0])
    for s in range(1, cs):
        win = win + _dot(us[s], pin_ref[s])

    ar, ai = ar_ref[...], ai_ref[...]
    xr, xi = xr_ref[...], xi_ref[...]
    prev_r, prev_i = [], []
    for c in range(nc):
        prev_r.append(xr)
        prev_i.append(xi)
        wr = win[c * nb:(c + 1) * nb, :ns]
        wi = win[c * nb:(c + 1) * nb, ns:]
        xr, xi = ar * xr - ai * xi + wr, ar * xi + ai * xr + wi
    xr_ref[...] = xr
    xi_ref[...] = xi
    xp = jnp.concatenate([jnp.concatenate(prev_r, axis=0), jnp.concatenate(prev_i, axis=0)],
                         axis=1).astype(BF16)

    for t in range(cs):
        lags = jnp.concatenate([us[t - d] for d in range(t + 1)], axis=1) if t else us[0]
        yt = _dot(xp, pout_ref[t]) + _dot(lags, kd_ref[0:(t + 1) * w, :])
        y_ref[:, :, t, :] = yt.reshape(nc, nb, w)


def _s5(u4, kd, pin, pout, ar, ai):
    nchunk, nb, cs, w = u4.shape
    nc = min(S5_TILE // cs, nchunk)
    const = dict(pipeline_mode=pl.Buffered(1))
    full = lambda shape: pl.BlockSpec(shape, lambda i: tuple(0 for _ in shape), **const)
    rows = nc * cs * nb
    chunked = pl.BlockSpec((nc, nb, cs, w), lambda i: (i, 0, 0, 0))
    est = (4 * rows * w * 4 + (kd.size + pin.size + pout.size) * 2 + rows // cs * 2 * S5_NSTATE * (4 + 4 + 2)
           + rows * w * (2 + 2 + 4))
    return pl.pallas_call(
        _s5_kernel,
        grid=(nchunk // nc,),
        in_specs=[chunked, full(kd.shape), full(pin.shape), full(pout.shape), full(ar.shape), full(ai.shape)],
        out_specs=chunked,
        out_shape=jax.ShapeDtypeStruct(u4.shape, F32),
        scratch_shapes=[pltpu.VMEM((nb, S5_NSTATE), F32), pltpu.VMEM((nb, S5_NSTATE), F32)],
        compiler_params=pltpu.CompilerParams(
            dimension_semantics=("arbitrary",), vmem_limit_bytes=_vmem_limit(est)),
        name="s5",
    )(u4, kd, pin, pout, ar, ai)


def _s5_operators(lam_re, lam_im, log_step, b_re, b_im, c_re, c_im):
    lr, li = lam_re.astype(F32), lam_im.astype(F32)
    dt = jnp.exp(log_step.astype(F32))[:, None]
    mag = jnp.exp(lr * dt)
    lbar_re, lbar_im = mag * jnp.cos(li * dt), mag * jnp.sin(li * dt)
    nr, ni = lbar_re - 1.0, lbar_im
    den = lr * lr + li * li
    coef_re = (nr * lr + ni * li) / den
    coef_im = (ni * lr - nr * li) / den
    br, bi = b_re.astype(F32), b_im.astype(F32)
    bbar_re = coef_re[..., None] * br - coef_im[..., None] * bi
    bbar_im = coef_re[..., None] * bi + coef_im[..., None] * br
    cr, ci = c_re.astype(F32), c_im.astype(F32)

    pr, pi = [jnp.ones_like(lbar_re)], [jnp.zeros_like(lbar_re)]
    for _ in range(S5_CHUNK):
        pr, pi = (pr + [pr[-1] * lbar_re - pi[-1] * lbar_im],
                  pi + [pr[-1] * lbar_im + pi[-1] * lbar_re])
    pw_re, pw_im = jnp.stack(pr), jnp.stack(pi)
    cs = S5_CHUNK

    def block_diag(compact, rows_group, cols_group, reps):
        tiled = jnp.tile(compact, reps)
        r = lax.broadcasted_iota(jnp.int32, tiled.shape, tiled.ndim - 2)
        c = lax.broadcasted_iota(jnp.int32, tiled.shape, tiled.ndim - 1)
        return jnp.where(rows_group(r) == cols_group(c), tiled, 0.0).astype(BF16)

    lane_group = lambda c: c // S5_GROUP
    state_group = lambda c: (c % S5_NSTATE) // S5_STATE

    ab_re = pw_re[:cs, :, :, None] * bbar_re - pw_im[:cs, :, :, None] * bbar_im
    ab_im = pw_re[:cs, :, :, None] * bbar_im + pw_im[:cs, :, :, None] * bbar_re
    taps = jnp.sum(cr[None, :, :, :, None] * ab_re[:, :, None] - ci[None, :, :, :, None] * ab_im[:, :, None],
                   axis=3)
    kd = block_diag(taps.transpose(0, 1, 3, 2).reshape(cs * S5_WIDTH, S5_GROUP),
                    lambda r: (r // S5_GROUP) % S5_GROUPS, lane_group, (1, S5_GROUPS))

    rev = cs - 1 - jnp.arange(cs)
    ab = jnp.stack([ab_re[rev], ab_im[rev]], axis=1)
    pin = block_diag(ab.transpose(0, 4, 1, 2, 3).reshape(cs, S5_GROUP, 2 * S5_NSTATE),
                     lane_group, state_group, (1, S5_GROUPS, 1))

    qr, qi = pw_re[1:], pw_im[1:]
    po_re = cr[None] * qr[:, :, None, :] - ci[None] * qi[:, :, None, :]
    po_im = -(cr[None] * qi[:, :, None, :] + ci[None] * qr[:, :, None, :])
    po = jnp.stack([po_re, po_im], axis=1)
    pout = block_diag(po.transpose(0, 1, 2, 4, 3).reshape(cs, 2 * S5_NSTATE, S5_GROUP),
                      state_group, lane_group, (1, 1, S5_GROUPS))
    a_re = pw_re[cs].reshape(1, S5_NSTATE)
    a_im = pw_im[cs].reshape(1, S5_NSTATE)
    return kd, pin, pout, a_re, a_im


def _out_ffn_kernel(x_ref, oa_ref, ob_ref, y_ref, u_ref, d_ref, gw_ref, gb_ref, wa_ref, wb_ref, wc_ref,
                    g_ref, wg_ref, wu_ref, wd_ref, o_ref):
    tm = x_ref.shape[0]
    y = y_ref[...].reshape(tm, S5_WIDTH) + d_ref[...] * u_ref[...].reshape(tm, S5_WIDTH)
    z = 0.5 * y * (1.0 + jnp.tanh(math.sqrt(2.0 / math.pi) * (y + 0.044715 * (y * y * y))))
    oc = z * _sigmoid(_dot(z.astype(BF16), gw_ref[...]) + gb_ref[...])
    x = (x_ref[...] + _dot(oa_ref[...], wa_ref[...]) + _dot(ob_ref[...], wb_ref[...])
         + _dot(oc.astype(BF16), wc_ref[...]))
    h = _rms_norm(x, g_ref[...]).astype(BF16)
    gate = _dot(h, wg_ref[...])
    up = _dot(h, wu_ref[...])
    act = (gate * _sigmoid(gate) * up).astype(BF16)
    o_ref[...] = x + 0.5 * _dot(act, wd_ref[...])


def _out_ffn(x, oa, ob, y_lb, u_lb, dvec, gw, gb, wa, wb, wc, gain, wg, wu, wd):
    b, l, d = x.shape
    f = wg.shape[1]
    tm = min(OUT_ROWS, l)
    rows = lambda w: pl.BlockSpec((None, tm, w), lambda bi, i: (bi, i, 0))
    token_major = pl.BlockSpec((tm // S5_CHUNK, None, S5_CHUNK, S5_WIDTH), lambda bi, i: (i, bi, 0, 0))
    full = lambda shape: pl.BlockSpec(shape, lambda bi, i: tuple(0 for _ in shape),
                                      pipeline_mode=pl.Buffered(1))
    est = (2 * (2 * tm * d * 4 + tm * (256 * 2 + 512 * 2 + 256 * 8)) + 2 * (d * d + 256 * 256) + 3 * d * f * 2
           + tm * f * (4 + 4 + 2 + 4) + 4 * tm * d * 4)
    return pl.pallas_call(
        _out_ffn_kernel,
        grid=(b, l // tm),
        in_specs=[rows(d), rows(FOX_WIDTH), rows(HGRN_WIDTH), token_major, token_major,
                  full((1, S5_WIDTH)), full(gw.shape), full((1, S5_WIDTH)),
                  full(wa.shape), full(wb.shape), full(wc.shape),
                  full((1, d)), full(wg.shape), full(wu.shape), full(wd.shape)],
        out_specs=rows(d),
        out_shape=jax.ShapeDtypeStruct((b, l, d), F32),
        compiler_params=pltpu.CompilerParams(
            dimension_semantics=("arbitrary", "arbitrary"), vmem_limit_bytes=_vmem_limit(est)),
        name="out_ffn",
    )(x, oa, ob, y_lb, u_lb, dvec, gw, gb, wa, wb, wc, gain, wg, wu, wd)


def kernel(x, ffn1_norm, ffn1_w_gate, ffn1_w_up, ffn1_w_down, mix_norm, w_in, fox_f_bias, fox_q_gain, fox_k_gain, hgrn_lb, hgrn_o_gain, s5_lambda_re, s5_lambda_im, s5_log_step, s5_B_re, s5_B_im, s5_C_re, s5_C_im, s5_D, s5_glu_w, s5_glu_b, w_out, ffn2_norm, ffn2_w_gate, ffn2_w_up, ffn2_w_down):
    bsz, seq, d = x.shape
    depth = w_in.shape[0]
    t = bsz * seq

    lb_all = jnp.cumsum(jax.nn.softmax(hgrn_lb.astype(F32), axis=0), axis=0)
    lb_all = lb_all - lb_all[0:1]

    o_q, o_k, o_v = 0, FOX_WIDTH, 2 * FOX_WIDTH
    o_f = 3 * FOX_WIDTH
    o_h = o_f + FOX_HEADS
    o_u = o_h + 4 * HGRN_WIDTH

    for l in range(depth):
        x = _ffn(x.reshape(t, d), ffn1_norm[l][None], ffn1_w_gate[l].astype(BF16),
                 ffn1_w_up[l].astype(BF16), ffn1_w_down[l].astype(BF16)).reshape(bsz, seq, d)

        w = w_in[l]
        wf = jnp.zeros((d, 128), F32).at[:, :3 * FOX_HEADS].set(jnp.tile(w[:, o_f:o_h], (1, 3))).astype(BF16)
        fb = jnp.zeros((1, 128), F32).at[0, :3 * FOX_HEADS].set(jnp.tile(fox_f_bias[l].astype(F32), 3))
        q, k, vt, cs, hg, u_lb = _proj(
            x, mix_norm[l][None], w[:, o_q:o_v].astype(BF16), w[:, o_v:o_f].T.astype(BF16), wf,
            w[:, o_h:o_u].astype(BF16), w[:, o_u:].astype(BF16),
            jnp.tile(fox_q_gain[l].astype(F32), FOX_HEADS)[None],
            jnp.tile(fox_k_gain[l].astype(F32), FOX_HEADS)[None], fb)

        o_a = _attn(q, k, vt, cs, fox_q_gain[l], fox_k_gain[l])
        o_b = _hgrn(hg, lb_all[l][None], hgrn_o_gain[l].astype(F32)[None])
        ops = _s5_operators(s5_lambda_re[l], s5_lambda_im[l], s5_log_step[l],
                            s5_B_re[l], s5_B_im[l], s5_C_re[l], s5_C_im[l])
        y_lb = _s5(u_lb, *ops)

        wo = w_out[l].astype(BF16)
        x = _out_ffn(x, o_a, o_b, y_lb, u_lb, s5_D[l].astype(F32)[None], s5_glu_w[l].astype(BF16),
                     s5_glu_b[l].astype(F32)[None], wo[:FOX_WIDTH], wo[FOX_WIDTH:FOX_WIDTH + HGRN_WIDTH],
                     wo[FOX_WIDTH + HGRN_WIDTH:], ffn2_norm[l][None], ffn2_w_gate[l].astype(BF16),
                     ffn2_w_up[l].astype(BF16), ffn2_w_down[l].astype(BF16))
    return x
```

```python
import functools
import math

import jax
import jax.numpy as jnp
import numpy as np
from jax import lax
from jax.experimental import pallas as pl
from jax.experimental.pallas import tpu as pltpu

F32 = jnp.float32
BF16 = jnp.bfloat16

EPS = 1e-6
D_MODEL = 1024
D_FF = 2816
FOX_HEADS = 4
FOX_HEAD_DIM = 64
FOX_WIDTH = FOX_HEADS * FOX_HEAD_DIM
HGRN_HEADS = 4
HGRN_DIM = 128
HGRN_WIDTH = HGRN_HEADS * HGRN_DIM
S5_GROUPS = 16
S5_GROUP = 16
S5_STATE = 64
S5_WIDTH = S5_GROUPS * S5_GROUP
S5_NSTATE = S5_GROUPS * S5_STATE
S5_CHUNK = 8
S5_TILE = 128
CAST_ROWS = 256
LOG2E = math.log2(math.e)

V7X_VMEM_BYTES = 64 * 1024 * 1024
V7X_SUBLANES = 8

ATTN_BLOCK = 256
ATTN_KV_BLOCKS = 2
ATTN_MAX_FREE_LOGIT = 40.0
HGRN_CHUNK = 128
FFN_ROWS = 512
PROJ_ROWS = 512
OUT_ROWS = 512


def _vmem_limit(estimate_bytes):
    return int(min(estimate_bytes * 5 // 4 + (4 << 20), V7X_VMEM_BYTES - (4 << 20)))


def _rms_norm(x, gain):
    ms = jnp.mean(x * x, axis=-1, keepdims=True)
    return x * lax.rsqrt(ms + EPS) * gain


def _dot(a, b):
    return jnp.dot(a, b, preferred_element_type=F32)


def _dot_nt(a, b):
    return lax.dot_general(a, b, (((1,), (1,)), ((), ())), preferred_element_type=F32)


def _dot_tn(a, b):
    return lax.dot_general(a, b, (((0,), (0,)), ((), ())), preferred_element_type=F32)


def _split3(a):
    a1 = a.astype(BF16)
    r1 = a - a1.astype(F32)
    a2 = r1.astype(BF16)
    a3 = (r1 - a2.astype(F32)).astype(BF16)
    return a1, a2, a3


def _dot_exact_rhs(a, rhs, terms=3):
    parts = _split3(a)[:terms]
    out = _dot(parts[0], rhs)
    for part in parts[1:]:
        out = out + _dot(part, rhs)
    return out


def _dot_exact_lhs(lhs, a, terms=3):
    parts = _split3(a)[:terms]
    out = _dot(lhs, parts[0])
    for part in parts[1:]:
        out = out + _dot(lhs, part)
    return out


def _sigmoid(x):
    return 1.0 / (1.0 + jnp.exp(-x))


def _cast_kernel(w_ref, o_ref):
    o_ref[...] = w_ref[...].astype(o_ref.dtype)


def _to_bf16(w):
    depth, rows, cols = w.shape
    tr = CAST_ROWS
    assert rows % tr == 0
    spec = pl.BlockSpec((None, tr, cols), lambda l, i: (l, i, 0))
    return pl.pallas_call(
        _cast_kernel,
        grid=(depth, rows // tr),
        in_specs=[spec],
        out_specs=spec,
        out_shape=jax.ShapeDtypeStruct(w.shape, BF16),
        compiler_params=pltpu.CompilerParams(
            dimension_semantics=("arbitrary", "arbitrary"),
            vmem_limit_bytes=_vmem_limit(2 * tr * cols * (4 + 2))),
        name="cast_bf16",
    )(w)


def _ffn_kernel(x_ref, g_ref, wg_ref, wu_ref, wd_ref, o_ref):
    x = x_ref[...]
    h = _rms_norm(x, g_ref[...]).astype(BF16)
    gate = _dot(h, wg_ref[...])
    up = _dot(h, wu_ref[...])
    act = (gate * _sigmoid(gate) * up).astype(BF16)
    o_ref[...] = x + 0.5 * _dot(act, wd_ref[...])


def _ffn(x2d, gain, wg, wu, wd, layer):
    t, d = x2d.shape
    f = wg.shape[2]
    tm = min(FFN_ROWS, t)
    const = dict(pipeline_mode=pl.Buffered(1))
    est = 3 * d * f * 2 + 4 * tm * d * 4 + tm * f * (4 + 4 + 2 + 4) + tm * d * 4
    return pl.pallas_call(
        _ffn_kernel,
        grid=(t // tm,),
        in_specs=[
            pl.BlockSpec((tm, d), lambda i: (i, 0)),
            pl.BlockSpec((1, d), lambda i: (0, 0)),
            pl.BlockSpec((None, d, f), lambda i: (layer, 0, 0), **const),
            pl.BlockSpec((None, d, f), lambda i: (layer, 0, 0), **const),
            pl.BlockSpec((None, f, d), lambda i: (layer, 0, 0), **const),
        ],
        out_specs=pl.BlockSpec((tm, d), lambda i: (i, 0)),
        out_shape=jax.ShapeDtypeStruct((t, d), F32),
        compiler_params=pltpu.CompilerParams(
            dimension_semantics=("arbitrary",), vmem_limit_bytes=_vmem_limit(est)),
        name="ffn",
    )(x2d, gain, wg, wu, wd)


def _proj_kernel(x_ref, g_ref, wqk_ref, wvt_ref, wf_ref, wh_ref, wu_ref, qg_ref, kg_ref,
                 fb_ref, lb_ref, q_ref, k_ref, vt_ref, cs_ref, hg_ref, u_ref, carry_ref):
    tm = x_ref.shape[0]
    tk = vt_ref.shape[2]

    @pl.when(pl.program_id(1) == 0)
    def _():
        carry_ref[...] = jnp.zeros_like(carry_ref)

    h = _rms_norm(x_ref[...], g_ref[...]).astype(BF16)

    li = lax.broadcasted_iota(jnp.int32, (FOX_WIDTH, FOX_WIDTH), 0) // FOX_HEAD_DIM
    lj = lax.broadcasted_iota(jnp.int32, (FOX_WIDTH, FOX_WIDTH), 1) // FOX_HEAD_DIM
    group = (li == lj).astype(BF16)

    def head_norm(a, gain):
        ms = _dot_exact_rhs(a * a, group, terms=2) * (1.0 / FOX_HEAD_DIM)
        return a * lax.rsqrt(ms + EPS) * gain

    qk = _dot(h, wqk_ref[...])
    q_ref[...] = (head_norm(qk[:, :FOX_WIDTH], qg_ref[...]) * (LOG2E / math.sqrt(FOX_HEAD_DIM))).astype(BF16)
    k_ref[...] = head_norm(qk[:, FOX_WIDTH:], kg_ref[...]).astype(BF16)
    vt = _dot_nt(wvt_ref[...], h).astype(BF16)
    for blk in range(tm // tk):
        vt_ref[blk] = vt[:, blk * tk:(blk + 1) * tk]

    fl = _dot(h, wf_ref[...]) + fb_ref[...]
    ls = jnp.minimum(fl, 0.0) - jnp.log1p(jnp.exp(-jnp.abs(fl)))
    ri = lax.broadcasted_iota(jnp.int32, (tm, tm), 0)
    ci = lax.broadcasted_iota(jnp.int32, (tm, tm), 1)
    lower = (ci <= ri).astype(BF16)
    c = _dot_exact_lhs(lower, ls) + carry_ref[0:1, :]
    carry_ref[...] = jnp.broadcast_to(c[tm - 1:tm, :], carry_ref.shape)
    c1, c2, c3 = _split3(c * LOG2E)
    lane = lax.broadcasted_iota(jnp.int32, c.shape, 1)
    cs_ref[...] = jnp.where(lane < FOX_HEADS, c1, jnp.where(lane < 2 * FOX_HEADS, c2, c3))

    hg = _dot(h, wh_ref[...])
    w = HGRN_WIDTH
    lb = lb_ref[...]
    f = lb + (1.0 - lb) * _sigmoid(hg[:, w:2 * w])
    g = hg[:, 3 * w:4 * w]
    hg_ref[:, 0:w] = hg[:, 0:w]
    hg_ref[:, w:2 * w] = jnp.log(f)
    hg_ref[:, 2 * w:3 * w] = f
    hg_ref[:, 3 * w:4 * w] = hg[:, 2 * w:3 * w]
    hg_ref[:, 4 * w:5 * w] = g * (0.5 * (jnp.tanh(0.5 * g) + 1.0))
    u_ref[...] = _dot(h, wu_ref[...]).reshape(u_ref.shape)


def _proj(x, gain, wqk, wvt, wf, wh, wu, qg, kg, fb, lb):
    b, l, d = x.shape
    tm = min(PROJ_ROWS, l)
    tk = min(ATTN_BLOCK, l)
    hw = 5 * HGRN_WIDTH
    full = lambda shape: pl.BlockSpec(shape, lambda bi, i: tuple(0 for _ in shape))
    est = 2 * (d * (512 + 256 + 128 + hw + 256) * 2) + 2 * tm * d * 4 + 2 * tm * (hw + 256 + 512) * 4 + 2 * tm * hw * 4
    return pl.pallas_call(
        _proj_kernel,
        grid=(b, l // tm),
        in_specs=[
            pl.BlockSpec((None, tm, d), lambda bi, i: (bi, i, 0)),
            full((1, d)), full(wqk.shape), full(wvt.shape), full(wf.shape), full(wh.shape),
            full(wu.shape), full((1, FOX_WIDTH)), full((1, FOX_WIDTH)), full((1, 128)),
            full((1, HGRN_WIDTH)),
        ],
        out_specs=[
            pl.BlockSpec((None, tm, FOX_WIDTH), lambda bi, i: (bi, i, 0)),
            pl.BlockSpec((None, tm, FOX_WIDTH), lambda bi, i: (bi, i, 0)),
            pl.BlockSpec((None, tm // tk, FOX_WIDTH, tk), lambda bi, i: (bi, i, 0, 0)),
            pl.BlockSpec((None, tm, 128), lambda bi, i: (bi, i, 0)),
            pl.BlockSpec((None, tm, hw), lambda bi, i: (bi, i, 0)),
            pl.BlockSpec((tm // S5_CHUNK, None, S5_CHUNK, S5_WIDTH), lambda bi, i: (i, bi, 0, 0)),
        ],
        out_shape=[
            jax.ShapeDtypeStruct((b, l, FOX_WIDTH), BF16),
            jax.ShapeDtypeStruct((b, l, FOX_WIDTH), BF16),
            jax.ShapeDtypeStruct((b, l // tk, FOX_WIDTH, tk), BF16),
            jax.ShapeDtypeStruct((b, l, 128), BF16),
            jax.ShapeDtypeStruct((b, l, hw), F32),
            jax.ShapeDtypeStruct((l // S5_CHUNK, b, S5_CHUNK, S5_WIDTH), F32),
        ],
        scratch_shapes=[pltpu.VMEM((V7X_SUBLANES, 128), F32)],
        compiler_params=pltpu.CompilerParams(
            dimension_semantics=("arbitrary", "arbitrary"), vmem_limit_bytes=_vmem_limit(est)),
        name="proj",
    )(x, gain, wqk, wvt, wf, wh, wu, qg, kg, fb, lb)


def _attn_kernel(bounded_ref, q_ref, k_ref, vt_ref, cs_ref, e_ref, pq_ref, pk_ref, oq_ref, ok_ref,
                 o_ref, ka_ref, qa_ref, m_ref, l_ref, acc_ref, s_ref, p_ref):
    tq = o_ref.shape[0]
    i = pl.program_id(1)

    @pl.when(i == 0)
    def _():
        cs = cs_ref[...]
        ka_ref[...] = (_dot(k_ref[...], e_ref[...]) + _dot(cs, pk_ref[...]) + ok_ref[...]).astype(BF16)
        qa_ref[...] = (_dot(q_ref[...], e_ref[...]) + _dot(cs, pq_ref[...]) + oq_ref[...]).astype(BF16)

    l_ref[...] = jnp.zeros_like(l_ref)
    acc_ref[...] = jnp.zeros_like(acc_ref)
    nblk = ATTN_KV_BLOCKS
    tk = nblk * tq
    nfull = i // nblk
    qrows = pl.ds(pl.multiple_of(i * tq, tq), tq)
    key_minus_qry = (lax.broadcasted_iota(jnp.int32, (tk, tq), 0)
                     - lax.broadcasted_iota(jnp.int32, (tk, tq), 1))

    def causal(s, jj):
        return jnp.where(key_minus_qry <= (i - jj * nblk) * tq, s, -jnp.inf)

    def pv_dot(jj, hd, p):
        head_rows = slice(hd * FOX_HEAD_DIM, (hd + 1) * FOX_HEAD_DIM)
        pv = _dot(vt_ref[jj * nblk, head_rows, :], p[0:tq, :])
        for bk in range(1, nblk):
            pv = pv + _dot(vt_ref[jj * nblk + bk, head_rows, :], p[bk * tq:(bk + 1) * tq, :])
        return pv

    def bounded_steps(steps):
        for slot, jj in enumerate(steps):
            rows = pl.ds(pl.multiple_of(jj * tk, tk), tk)
            for hd in range(FOX_HEADS):
                slab = slice(hd * 128, (hd + 1) * 128)
                p = jnp.exp2(causal(_dot_nt(ka_ref[rows, slab], qa_ref[qrows, slab]), jj))
                l_ref[hd] = l_ref[hd] + jnp.sum(p, axis=0, keepdims=True)
                p_ref[slot, hd] = p.astype(BF16)
        for hd in range(FOX_HEADS):
            pv = pv_dot(steps[0], hd, p_ref[0, hd])
            for slot, jj in enumerate(steps[1:], 1):
                pv = pv + pv_dot(jj, hd, p_ref[slot, hd])
            acc_ref[hd] = acc_ref[hd] + pv

    @pl.when(bounded_ref[0] == 1)
    def _():
        def pair(kk, carry):
            bounded_steps([2 * kk, 2 * kk + 1])
            return carry

        lax.fori_loop(0, (nfull + 1) // 2, pair, 0)

        @pl.when(nfull % 2 == 0)
        def _():
            bounded_steps([nfull])

    @pl.when(bounded_ref[0] == 0)
    def _():
        _attn_online(i, nfull, tk, qrows, causal, pv_dot, ka_ref, qa_ref, m_ref, l_ref, acc_ref,
                     s_ref, p_ref.at[0])

    outs = [acc_ref[hd] / l_ref[hd] for hd in range(FOX_HEADS)]
    o_ref[...] = jnp.concatenate(outs, axis=0).T.astype(o_ref.dtype)


def _attn_online(i, nfull, tk, qrows, causal, pv_dot, ka_ref, qa_ref, m_ref, l_ref, acc_ref, s_ref, p_ref):
    m_ref[...] = jnp.full(m_ref.shape, -jnp.inf, F32)

    def logits(jj, slot):
        rows = pl.ds(pl.multiple_of(jj * tk, tk), tk)
        for hd in range(FOX_HEADS):
            slab = slice(hd * 128, (hd + 1) * 128)
            s_ref[slot, hd] = _dot_nt(ka_ref[rows, slab], qa_ref[qrows, slab])

    def update(jj, slot, last):
        alphas = []
        for hd in range(FOX_HEADS):
            s = s_ref[slot, hd]
            if last:
                s = causal(s, jj)
            m_old = m_ref[hd]
            m_new = jnp.maximum(m_old, jnp.max(s, axis=0, keepdims=True))
            alpha = jnp.exp2(m_old - m_new)
            p = jnp.exp2(s - m_new)
            m_ref[hd] = m_new
            l_ref[hd] = alpha * l_ref[hd] + jnp.sum(p, axis=0, keepdims=True)
            p_ref[hd] = p.astype(BF16)
            alphas.append(alpha)
        for hd in range(FOX_HEADS):
            acc_ref[hd] = alphas[hd] * acc_ref[hd] + pv_dot(jj, hd, p_ref[hd])

    def pair(kk, carry):
        logits(2 * kk + 1, 1)
        update(2 * kk, 0, False)
        logits(2 * kk + 2, 0)
        update(2 * kk + 1, 1, False)
        return carry

    logits(0, 0)
    npair = nfull // 2
    lax.fori_loop(0, npair, pair, 0)

    @pl.when(nfull % 2 == 0)
    def _():
        update(2 * npair, 0, True)

    @pl.when(nfull % 2 == 1)
    def _():
        logits(2 * npair + 1, 1)
        update(2 * npair, 0, False)
        update(2 * npair + 1, 1, True)


def _attn_placement():
    w = FOX_HEADS * 128
    e = np.zeros((FOX_WIDTH, w), np.float32)
    cols = np.arange(FOX_WIDTH)
    e[cols, (cols // FOX_HEAD_DIM) * 128 + cols % FOX_HEAD_DIM] = 1.0
    pq, pk = np.zeros((128, w), np.float32), np.zeros((128, w), np.float32)
    oq, ok = np.zeros((1, w), np.float32), np.zeros((1, w), np.float32)
    for hd in range(FOX_HEADS):
        base = hd * 128 + FOX_HEAD_DIM
        for term in range(3):
            pk[term * FOX_HEADS + hd, base + term] = -1.0
            oq[0, base + term] = 1.0
            pq[term * FOX_HEADS + hd, base + 3 + term] = 1.0
            ok[0, base + 3 + term] = 1.0
    return (jnp.asarray(e, BF16), jnp.asarray(pq, BF16), jnp.asarray(pk, BF16),
            jnp.asarray(oq), jnp.asarray(ok))


def _attn(q, k, vt, cs, q_gain, k_gain):
    b, l, w = q.shape
    nkv, tk = vt.shape[1], vt.shape[3]
    tq = tk
    wa = FOX_HEADS * 128
    e, pq, pk, oq, ok = _attn_placement()
    full = lambda shape: pl.BlockSpec(shape, lambda bi, i: tuple(0 for _ in shape))
    seq = lambda width: pl.BlockSpec((None, l, width), lambda bi, i: (bi, 0, 0))
    assert (l // tq) % ATTN_KV_BLOCKS == 0
    logit_bound = (1.01 * FOX_HEAD_DIM * LOG2E / math.sqrt(FOX_HEAD_DIM)
                   * jnp.max(jnp.abs(q_gain.astype(F32))) * jnp.max(jnp.abs(k_gain.astype(F32))))
    bounded = (logit_bound < ATTN_MAX_FREE_LOGIT).astype(jnp.int32).reshape(1)
    est = (2 * (l * w * 2 * 2 + l * 128 * 2) + 2 * l * wa * (2 + 4 + 4)
           + FOX_HEADS * ATTN_KV_BLOCKS * tq * tq * (2 * 4 + 2 * 2 + 2 * 4) + FOX_HEADS * 128 * tq * 4)
    return pl.pallas_call(
        _attn_kernel,
        grid=(b, l // tq),
        in_specs=[
            pl.BlockSpec(memory_space=pltpu.SMEM),
            seq(w), seq(w),
            pl.BlockSpec((None, nkv, w, tk), lambda bi, i: (bi, 0, 0, 0)),
            seq(128),
            full(e.shape), full(pq.shape), full(pk.shape), full(oq.shape), full(ok.shape),
        ],
        out_specs=pl.BlockSpec((None, tq, w), lambda bi, i: (bi, i, 0)),
        out_shape=jax.ShapeDtypeStruct((b, l, w), BF16),
        scratch_shapes=[pltpu.VMEM((l, wa), BF16), pltpu.VMEM((l, wa), BF16),
                        pltpu.VMEM((FOX_HEADS, 1, tq), F32), pltpu.VMEM((FOX_HEADS, 1, tq), F32),
                        pltpu.VMEM((FOX_HEADS, FOX_HEAD_DIM, tq), F32),
                        pltpu.VMEM((2, FOX_HEADS, ATTN_KV_BLOCKS * tq, tq), F32),
                        pltpu.VMEM((2, FOX_HEADS, ATTN_KV_BLOCKS * tq, tq), BF16)],
        compiler_params=pltpu.CompilerParams(
            dimension_semantics=("arbitrary", "arbitrary"), vmem_limit_bytes=_vmem_limit(est)),
        name="fox_attn",
    )(bounded, q, k, vt, cs, e, pq, pk, oq, ok)


def _hgrn_levels(c):
    return [c >> (s + 1) for s in range(int(math.log2(c)))]


def _hgrn_level_ids(c):
    t = np.arange(c)[:, None]
    s = np.arange(c)[None, :]
    lvl = np.floor(np.log2(np.maximum(t ^ s, 1))).astype(np.int32) + 1
    return jnp.asarray(np.where(s > t, -1, np.where(s == t, 0, lvl)).astype(np.int32))


def _hgrn_kernel(q_ref, lf_ref, f_ref, i_ref, gate_ref, og_ref, lvl_ref, o_ref, st_ref, b_ref):
    c = q_ref.shape[0]

    @pl.when(pl.program_id(1) == 0)
    def _():
        st_ref[...] = jnp.zeros_like(st_ref)

    f = f_ref[...]
    key = 1.0 - f
    q = q_ref[...]
    ti = lax.broadcasted_iota(jnp.int32, (c, c), 0)
    si = lax.broadcasted_iota(jnp.int32, (c, c), 1)
    lower = (si <= ti).astype(BF16)
    b = _dot_exact_lhs(lower, lf_ref[...], terms=2)
    b_ref[...] = b
    row = lax.broadcasted_iota(jnp.int32, b.shape, 0)

    def ref_rows(m):
        parts = [jnp.broadcast_to(b_ref[g0 + m - 1:g0 + m, :], (2 * m, b.shape[1]))
                 for g0 in range(0, c, 2 * m)]
        return jnp.concatenate(parts, axis=0) if len(parts) > 1 else parts[0]

    def level_decay(m):
        if m == 1:
            return jnp.where(row % 2 == 1, f, 1.0)
        if m == 2:
            off = row % 4
            nxt, prv = pltpu.roll(f, c - 1, axis=0), pltpu.roll(f, 1, axis=0)
            return jnp.where(off == 0, nxt, jnp.where(off == 1, 1.0, jnp.where(off == 2, f, f * prv)))
        return jnp.exp2(jnp.abs(b - ref_rows(m)) * (-LOG2E))

    lvl = lvl_ref[...]
    qb, kb = q.astype(BF16), key.astype(BF16)
    scores = [jnp.zeros((c, c), F32) for _ in range(HGRN_HEADS)]
    for idx, m in enumerate([0] + _hgrn_levels(c)[::-1]):
        if m == 0:
            z, y = qb, kb
        else:
            e = level_decay(m).astype(BF16)
            z, y = qb * e, kb * e
        mask = lvl == idx
        for hd in range(HGRN_HEADS):
            sl = slice(hd * HGRN_DIM, (hd + 1) * HGRN_DIM)
            scores[hd] = jnp.where(mask, _dot_nt(z[:, sl], y[:, sl]), scores[hd])

    b_last = b[c - 1:c, :]
    q_in = (q * jnp.exp(b)).astype(BF16)
    k_out = (key * jnp.exp(b_last - b)).astype(BF16)
    v = i_ref[...].astype(BF16)
    gate = gate_ref[...]
    og = og_ref[...]
    decay = jnp.exp(b_last)
    for hd in range(HGRN_HEADS):
        sl = slice(hd * HGRN_DIM, (hd + 1) * HGRN_DIM)
        st = st_ref[hd]
        o = _dot(scores[hd].astype(BF16), v[:, sl]) + _dot_nt(q_in[:, sl], st.astype(BF16))
        st_ref[hd] = st * decay[:, sl] + _dot_tn(v[:, sl], k_out[:, sl])
        o_ref[:, sl] = (_rms_norm(o, og[:, sl]) * gate[:, sl]).astype(o_ref.dtype)


def _hgrn(hg, og):
    b, l, _ = hg.shape
    c = min(HGRN_CHUNK, l)
    w = HGRN_WIDTH
    part = lambda p: pl.BlockSpec((None, c, w), lambda bi, i, p=p: (bi, i, p))
    est = 2 * 6 * c * w * 4 + 16 * c * w * 4 + HGRN_HEADS * HGRN_DIM * HGRN_DIM * 4
    return pl.pallas_call(
        _hgrn_kernel,
        grid=(b, l // c),
        in_specs=[part(0), part(1), part(2), part(3), part(4),
                  pl.BlockSpec((1, w), lambda bi, i: (0, 0)),
                  pl.BlockSpec((c, c), lambda bi, i: (0, 0))],
        out_specs=pl.BlockSpec((None, c, w), lambda bi, i: (bi, i, 0)),
        out_shape=jax.ShapeDtypeStruct((b, l, w), BF16),
        scratch_shapes=[pltpu.VMEM((HGRN_HEADS, HGRN_DIM, HGRN_DIM), F32),
                        pltpu.VMEM((c, w), F32)],
        compiler_params=pltpu.CompilerParams(
            dimension_semantics=("arbitrary", "arbitrary"), vmem_limit_bytes=_vmem_limit(est)),
        name="hgrn2",
    )(hg, hg, hg, hg, hg, og, _hgrn_level_ids(c))


def _s5_kernel(u_ref, kd_ref, pin_ref, pout_ref, ar_ref, ai_ref, y_ref, xr_ref, xi_ref):
    nc, nb, cs, w = u_ref.shape
    ns = ar_ref.shape[1]

    @pl.when(pl.program_id(0) == 0)
    def _():
        xr_ref[...] = jnp.zeros_like(xr_ref)
        xi_ref[...] = jnp.zeros_like(xi_ref)

    us = [u_ref[:, :, s, :].reshape(nc * nb, w).astype(BF16) for s in range(cs)]
    win = _dot(us[0], pin_ref[0])
    for s in range(1, cs):
        win = win + _dot(us[s], pin_ref[s])

    ar, ai = ar_ref[...], ai_ref[...]
    xr, xi = xr_ref[...], xi_ref[...]
    prev_r, prev_i = [], []
    for c in range(nc):
        prev_r.append(xr)
        prev_i.append(xi)
        wr = win[c * nb:(c + 1) * nb, :ns]
        wi = win[c * nb:(c + 1) * nb, ns:]
        xr, xi = ar * xr - ai * xi + wr, ar * xi + ai * xr + wi
    xr_ref[...] = xr
    xi_ref[...] = xi
    xp = jnp.concatenate([jnp.concatenate(prev_r, axis=0), jnp.concatenate(prev_i, axis=0)],
                         axis=1).astype(BF16)

    for t in range(cs):
        lags = jnp.concatenate([us[t - d] for d in range(t + 1)], axis=1) if t else us[0]
        yt = _dot(xp, pout_ref[t]) + _dot(lags, kd_ref[0:(t + 1) * w, :])
        y_ref[:, :, t, :] = yt.reshape(nc, nb, w)


def _s5(u4, kd, pin, pout, ar, ai):
    nchunk, nb, cs, w = u4.shape
    nc = min(S5_TILE // cs, nchunk)
    const = dict(pipeline_mode=pl.Buffered(1))
    full = lambda shape: pl.BlockSpec(shape, lambda i: tuple(0 for _ in shape), **const)
    rows = nc * cs * nb
    chunked = pl.BlockSpec((nc, nb, cs, w), lambda i: (i, 0, 0, 0))
    est = (4 * rows * w * 4 + (kd.size + pin.size + pout.size) * 2 + rows // cs * 2 * S5_NSTATE * (4 + 4 + 2)
           + rows * w * (2 + 2 + 4))
    return pl.pallas_call(
        _s5_kernel,
        grid=(nchunk // nc,),
        in_specs=[chunked, full(kd.shape), full(pin.shape), full(pout.shape), full(ar.shape), full(ai.shape)],
        out_specs=chunked,
        out_shape=jax.ShapeDtypeStruct(u4.shape, F32),
        scratch_shapes=[pltpu.VMEM((nb, S5_NSTATE), F32), pltpu.VMEM((nb, S5_NSTATE), F32)],
        compiler_params=pltpu.CompilerParams(
            dimension_semantics=("arbitrary",), vmem_limit_bytes=_vmem_limit(est)),
        name="s5",
    )(u4, kd, pin, pout, ar, ai)


def _s5_operators(lam_re, lam_im, log_step, b_re, b_im, c_re, c_im):
    lr, li = lam_re.astype(F32), lam_im.astype(F32)
    dt = jnp.exp(log_step.astype(F32))[:, None]
    mag = jnp.exp(lr * dt)
    lbar_re, lbar_im = mag * jnp.cos(li * dt), mag * jnp.sin(li * dt)
    nr, ni = lbar_re - 1.0, lbar_im
    den = lr * lr + li * li
    coef_re = (nr * lr + ni * li) / den
    coef_im = (ni * lr - nr * li) / den
    br, bi = b_re.astype(F32), b_im.astype(F32)
    bbar_re = coef_re[..., None] * br - coef_im[..., None] * bi
    bbar_im = coef_re[..., None] * bi + coef_im[..., None] * br
    cr, ci = c_re.astype(F32), c_im.astype(F32)

    pr, pi = [jnp.ones_like(lbar_re)], [jnp.zeros_like(lbar_re)]
    for _ in range(S5_CHUNK):
        pr, pi = (pr + [pr[-1] * lbar_re - pi[-1] * lbar_im],
                  pi + [pr[-1] * lbar_im + pi[-1] * lbar_re])
    pw_re, pw_im = jnp.stack(pr), jnp.stack(pi)
    cs = S5_CHUNK

    def block_diag(compact, rows_group, cols_group, reps):
        tiled = jnp.tile(compact, reps)
        r = lax.broadcasted_iota(jnp.int32, tiled.shape, tiled.ndim - 2)
        c = lax.broadcasted_iota(jnp.int32, tiled.shape, tiled.ndim - 1)
        return jnp.where(rows_group(r) == cols_group(c), tiled, 0.0).astype(BF16)

    lane_group = lambda c: c // S5_GROUP
    state_group = lambda c: (c % S5_NSTATE) // S5_STATE

    ab_re = pw_re[:cs, :, :, None] * bbar_re - pw_im[:cs, :, :, None] * bbar_im
    ab_im = pw_re[:cs, :, :, None] * bbar_im + pw_im[:cs, :, :, None] * bbar_re
    taps = jnp.sum(cr[None, :, :, :, None] * ab_re[:, :, None] - ci[None, :, :, :, None] * ab_im[:, :, None],
                   axis=3)
    kd = block_diag(taps.transpose(0, 1, 3, 2).reshape(cs * S5_WIDTH, S5_GROUP),
                    lambda r: (r // S5_GROUP) % S5_GROUPS, lane_group, (1, S5_GROUPS))

    rev = cs - 1 - jnp.arange(cs)
    ab = jnp.stack([ab_re[rev], ab_im[rev]], axis=1)
    pin = block_diag(ab.transpose(0, 4, 1, 2, 3).reshape(cs, S5_GROUP, 2 * S5_NSTATE),
                     lane_group, state_group, (1, S5_GROUPS, 1))

    qr, qi = pw_re[1:], pw_im[1:]
    po_re = cr[None] * qr[:, :, None, :] - ci[None] * qi[:, :, None, :]
    po_im = -(cr[None] * qi[:, :, None, :] + ci[None] * qr[:, :, None, :])
    po = jnp.stack([po_re, po_im], axis=1)
    pout = block_diag(po.transpose(0, 1, 2, 4, 3).reshape(cs, 2 * S5_NSTATE, S5_GROUP),
                      state_group, lane_group, (1, 1, S5_GROUPS))
    a_re = pw_re[cs].reshape(1, S5_NSTATE)
    a_im = pw_im[cs].reshape(1, S5_NSTATE)
    return kd, pin, pout, a_re, a_im


def _out_ffn_kernel(x_ref, oa_ref, ob_ref, y_ref, u_ref, d_ref, gw_ref, gb_ref, wa_ref, wb_ref, wc_ref,
                    g_ref, wg_ref, wu_ref, wd_ref, o_ref):
    tm = x_ref.shape[0]
    y = y_ref[...].reshape(tm, S5_WIDTH) + d_ref[...] * u_ref[...].reshape(tm, S5_WIDTH)
    z = 0.5 * y * (1.0 + jnp.tanh(math.sqrt(2.0 / math.pi) * (y + 0.044715 * (y * y * y))))
    oc = z * _sigmoid(_dot(z.astype(BF16), gw_ref[...]) + gb_ref[...])
    x = (x_ref[...] + _dot(oa_ref[...], wa_ref[...]) + _dot(ob_ref[...], wb_ref[...])
         + _dot(oc.astype(BF16), wc_ref[...]))
    h = _rms_norm(x, g_ref[...]).astype(BF16)
    gate = _dot(h, wg_ref[...])
    up = _dot(h, wu_ref[...])
    act = (gate * _sigmoid(gate) * up).astype(BF16)
    o_ref[...] = x + 0.5 * _dot(act, wd_ref[...])


def _out_ffn(x, oa, ob, y_lb, u_lb, dvec, gw, gb, wa, wb, wc, gain, wg, wu, wd, layer):
    b, l, d = x.shape
    f = wg.shape[2]
    stacked = lambda shape: pl.BlockSpec((None,) + shape[1:], lambda bi, i: (layer, 0, 0),
                                         pipeline_mode=pl.Buffered(1))
    tm = min(OUT_ROWS, l)
    rows = lambda w: pl.BlockSpec((None, tm, w), lambda bi, i: (bi, i, 0))
    token_major = pl.BlockSpec((tm // S5_CHUNK, None, S5_CHUNK, S5_WIDTH), lambda bi, i: (i, bi, 0, 0))
    full = lambda shape: pl.BlockSpec(shape, lambda bi, i: tuple(0 for _ in shape),
                                      pipeline_mode=pl.Buffered(1))
    est = (2 * (2 * tm * d * 4 + tm * (256 * 2 + 512 * 2 + 256 * 8)) + 2 * (d * d + 256 * 256) + 3 * d * f * 2
           + tm * f * (4 + 4 + 2 + 4) + 4 * tm * d * 4)
    return pl.pallas_call(
        _out_ffn_kernel,
        grid=(b, l // tm),
        in_specs=[rows(d), rows(FOX_WIDTH), rows(HGRN_WIDTH), token_major, token_major,
                  full((1, S5_WIDTH)), full(gw.shape), full((1, S5_WIDTH)),
                  full(wa.shape), full(wb.shape), full(wc.shape),
                  full((1, d)), stacked(wg.shape), stacked(wu.shape), stacked(wd.shape)],
        out_specs=rows(d),
        out_shape=jax.ShapeDtypeStruct((b, l, d), F32),
        compiler_params=pltpu.CompilerParams(
            dimension_semantics=("arbitrary", "arbitrary"), vmem_limit_bytes=_vmem_limit(est)),
        name="out_ffn",
    )(x, oa, ob, y_lb, u_lb, dvec, gw, gb, wa, wb, wc, gain, wg, wu, wd)


def kernel(x, ffn1_norm, ffn1_w_gate, ffn1_w_up, ffn1_w_down, mix_norm, w_in, fox_f_bias, fox_q_gain, fox_k_gain, hgrn_lb, hgrn_o_gain, s5_lambda_re, s5_lambda_im, s5_log_step, s5_B_re, s5_B_im, s5_C_re, s5_C_im, s5_D, s5_glu_w, s5_glu_b, w_out, ffn2_norm, ffn2_w_gate, ffn2_w_up, ffn2_w_down):
    bsz, seq, d = x.shape
    depth = w_in.shape[0]
    t = bsz * seq

    lb_all = jnp.cumsum(jax.nn.softmax(hgrn_lb.astype(F32), axis=0), axis=0)
    lb_all = lb_all - lb_all[0:1]

    o_q, o_k, o_v = 0, FOX_WIDTH, 2 * FOX_WIDTH
    o_f = 3 * FOX_WIDTH
    o_h = o_f + FOX_HEADS
    o_u = o_h + 4 * HGRN_WIDTH

    ffn1 = [_to_bf16(w) for w in (ffn1_w_gate, ffn1_w_up, ffn1_w_down)]
    ffn2 = [_to_bf16(w) for w in (ffn2_w_gate, ffn2_w_up, ffn2_w_down)]

    for l in range(depth):
        x = _ffn(x.reshape(t, d), ffn1_norm[l][None], *ffn1, layer=l).reshape(bsz, seq, d)

        w = w_in[l]
        wf = jnp.zeros((d, 128), F32).at[:, :3 * FOX_HEADS].set(jnp.tile(w[:, o_f:o_h], (1, 3))).astype(BF16)
        fb = jnp.zeros((1, 128), F32).at[0, :3 * FOX_HEADS].set(jnp.tile(fox_f_bias[l].astype(F32), 3))
        q, k, vt, cs, hg, u_lb = _proj(
            x, mix_norm[l][None], w[:, o_q:o_v].astype(BF16), w[:, o_v:o_f].T.astype(BF16), wf,
            w[:, o_h:o_u].astype(BF16), w[:, o_u:].astype(BF16),
            jnp.tile(fox_q_gain[l].astype(F32), FOX_HEADS)[None],
            jnp.tile(fox_k_gain[l].astype(F32), FOX_HEADS)[None], fb, lb_all[l][None])

        o_a = _attn(q, k, vt, cs, fox_q_gain[l], fox_k_gain[l])
        o_b = _hgrn(hg, hgrn_o_gain[l].astype(F32)[None])
        ops = _s5_operators(s5_lambda_re[l], s5_lambda_im[l], s5_log_step[l],
                            s5_B_re[l], s5_B_im[l], s5_C_re[l], s5_C_im[l])
        y_lb = _s5(u_lb, *ops)

        wo = w_out[l].astype(BF16)
        x = _out_ffn(x, o_a, o_b, y_lb, u_lb, s5_D[l].astype(F32)[None], s5_glu_w[l].astype(BF16),
                     s5_glu_b[l].astype(F32)[None], wo[:FOX_WIDTH], wo[FOX_WIDTH:FOX_WIDTH + HGRN_WIDTH],
                     wo[FOX_WIDTH + HGRN_WIDTH:], ffn2_norm[l][None], *ffn2, layer=l)
    return x
```

```python
import functools
import math

import jax
import jax.numpy as jnp
import numpy as np
from jax import lax
from jax.experimental import pallas as pl
from jax.experimental.pallas import tpu as pltpu

F32 = jnp.float32
BF16 = jnp.bfloat16

EPS = 1e-6
D_MODEL = 1024
D_FF = 2816
FOX_HEADS = 4
FOX_HEAD_DIM = 64
FOX_WIDTH = FOX_HEADS * FOX_HEAD_DIM
HGRN_HEADS = 4
HGRN_DIM = 128
HGRN_WIDTH = HGRN_HEADS * HGRN_DIM
S5_GROUPS = 16
S5_GROUP = 16
S5_STATE = 64
S5_WIDTH = S5_GROUPS * S5_GROUP
S5_NSTATE = S5_GROUPS * S5_STATE
S5_CHUNK = 8
S5_TILE = 128
FFN_STAGE_ROWS_WIDE = 64
FFN_STAGE_ROWS_TALL = 128
LOG2E = math.log2(math.e)

V7X_VMEM_BYTES = 64 * 1024 * 1024
V7X_SUBLANES = 8

ATTN_BLOCK = 256
ATTN_KV_BLOCKS = 2
ATTN_MAX_FREE_LOGIT = 40.0
HGRN_CHUNK = 128
FFN_ROWS = 512
FFN_SPLIT = 2
PROJ_ROWS = 512
OUT_ROWS = 512


def _vmem_limit(estimate_bytes):
    return int(min(estimate_bytes * 5 // 4 + (4 << 20), V7X_VMEM_BYTES - (4 << 20)))


def _rms_norm(x, gain):
    ms = jnp.mean(x * x, axis=-1, keepdims=True)
    return x * lax.rsqrt(ms + EPS) * gain


def _dot(a, b):
    return jnp.dot(a, b, preferred_element_type=F32)


def _dot_nt(a, b):
    return lax.dot_general(a, b, (((1,), (1,)), ((), ())), preferred_element_type=F32)


def _dot_tn(a, b):
    return lax.dot_general(a, b, (((0,), (0,)), ((), ())), preferred_element_type=F32)


def _split3(a):
    a1 = a.astype(BF16)
    r1 = a - a1.astype(F32)
    a2 = r1.astype(BF16)
    a3 = (r1 - a2.astype(F32)).astype(BF16)
    return a1, a2, a3


def _dot_exact_rhs(a, rhs, terms=3):
    parts = _split3(a)[:terms]
    out = _dot(parts[0], rhs)
    for part in parts[1:]:
        out = out + _dot(part, rhs)
    return out


def _dot_exact_lhs(lhs, a, terms=3):
    parts = _split3(a)[:terms]
    out = _dot(lhs, parts[0])
    for part in parts[1:]:
        out = out + _dot(lhs, part)
    return out


def _sigmoid(x):
    return 1.0 / (1.0 + jnp.exp(-x))


def _fetch_bf16(w_hbm, layer, dst_ref, stage_ref, sem):
    rows = stage_ref.shape[1]
    n = dst_ref.shape[0] // rows

    def chunk(c, slot):
        return pltpu.make_async_copy(w_hbm.at[layer, pl.ds(c * rows, rows), :], stage_ref.at[slot],
                                     sem.at[slot])

    chunk(0, 0).start()

    def body(c, carry):
        slot = c % 2

        @pl.when(c + 1 < n)
        def _():
            chunk(c + 1, 1 - slot).start()

        chunk(c, slot).wait()
        dst_ref[pl.ds(pl.multiple_of(c * rows, rows), rows), :] = stage_ref[slot].astype(BF16)
        return carry

    lax.fori_loop(0, n, body, 0)


def _fetch_ffn_weights(layer, wg_hbm, wu_hbm, wd_hbm, wg_ref, wu_ref, wd_ref, wide_stage, tall_stage, sem):
    _fetch_bf16(wg_hbm, layer, wg_ref, wide_stage, sem)
    _fetch_bf16(wu_hbm, layer, wu_ref, wide_stage, sem)
    _fetch_bf16(wd_hbm, layer, wd_ref, tall_stage, sem)


def _ffn_weight_scratch(d, f):
    assert d % FFN_STAGE_ROWS_WIDE == 0 and f % FFN_STAGE_ROWS_TALL == 0
    return [pltpu.VMEM((d, f), BF16), pltpu.VMEM((d, f), BF16), pltpu.VMEM((f, d), BF16),
            pltpu.VMEM((2, FFN_STAGE_ROWS_WIDE, f), F32), pltpu.VMEM((2, FFN_STAGE_ROWS_TALL, d), F32),
            pltpu.SemaphoreType.DMA((2,))]


def _swiglu_residual(x, gain, wg_ref, wu_ref, wd_ref):
    h = _rms_norm(x, gain).astype(BF16)
    gate = _dot(h, wg_ref[...])
    up = _dot(h, wu_ref[...])
    act = (gate * _sigmoid(gate) * up).astype(BF16)
    return x + 0.5 * _dot(act, wd_ref[...])


def _ffn_kernel(x_ref, g_ref, wg_hbm, wu_hbm, wd_hbm, o_ref, wg_ref, wu_ref, wd_ref, wide_stage, tall_stage,
                sem, *, layer):
    @pl.when(pl.program_id(0) == 0)
    def _():
        _fetch_ffn_weights(layer, wg_hbm, wu_hbm, wd_hbm, wg_ref, wu_ref, wd_ref, wide_stage, tall_stage, sem)

    half = x_ref.shape[0] // FFN_SPLIT
    for part in range(FFN_SPLIT):
        rows = slice(part * half, (part + 1) * half)
        o_ref[rows, :] = _swiglu_residual(x_ref[rows, :], g_ref[...], wg_ref, wu_ref, wd_ref)


def _ffn(x2d, gain, wg, wu, wd, layer):
    t, d = x2d.shape
    f = wg.shape[2]
    tm = min(FFN_ROWS, t)
    hbm = pl.BlockSpec(memory_space=pl.ANY)
    est = (3 * d * f * 2 + 2 * 4 * (FFN_STAGE_ROWS_WIDE * f + FFN_STAGE_ROWS_TALL * d) + 4 * tm * d * 4
           + tm * f * (4 + 4 + 2 + 4) + tm * d * 4)
    return pl.pallas_call(
        functools.partial(_ffn_kernel, layer=layer),
        grid=(t // tm,),
        in_specs=[
            pl.BlockSpec((tm, d), lambda i: (i, 0)),
            pl.BlockSpec((1, d), lambda i: (0, 0)),
            hbm, hbm, hbm,
        ],
        out_specs=pl.BlockSpec((tm, d), lambda i: (i, 0)),
        out_shape=jax.ShapeDtypeStruct((t, d), F32),
        scratch_shapes=_ffn_weight_scratch(d, f),
        compiler_params=pltpu.CompilerParams(
            dimension_semantics=("arbitrary",), vmem_limit_bytes=_vmem_limit(est)),
        name="ffn",
    )(x2d, gain, wg, wu, wd)


def _proj_kernel(x_ref, g_ref, wqk_ref, wvt_ref, wf_ref, wh_ref, wu_ref, qg_ref, kg_ref,
                 fb_ref, lb_ref, q_ref, k_ref, vt_ref, cs_ref, hg_ref, u_ref, carry_ref):
    tm = x_ref.shape[0]
    tk = vt_ref.shape[2]

    @pl.when(pl.program_id(1) == 0)
    def _():
        carry_ref[...] = jnp.zeros_like(carry_ref)

    h = _rms_norm(x_ref[...], g_ref[...]).astype(BF16)

    li = lax.broadcasted_iota(jnp.int32, (FOX_WIDTH, FOX_WIDTH), 0) // FOX_HEAD_DIM
    lj = lax.broadcasted_iota(jnp.int32, (FOX_WIDTH, FOX_WIDTH), 1) // FOX_HEAD_DIM
    group = (li == lj).astype(BF16)

    def head_norm(a, gain):
        ms = _dot_exact_rhs(a * a, group, terms=2) * (1.0 / FOX_HEAD_DIM)
        return a * lax.rsqrt(ms + EPS) * gain

    qk = _dot(h, wqk_ref[...])
    q_ref[...] = (head_norm(qk[:, :FOX_WIDTH], qg_ref[...]) * (LOG2E / math.sqrt(FOX_HEAD_DIM))).astype(BF16)
    k_ref[...] = head_norm(qk[:, FOX_WIDTH:], kg_ref[...]).astype(BF16)
    vt = _dot_nt(wvt_ref[...], h).astype(BF16)
    for blk in range(tm // tk):
        vt_ref[blk] = vt[:, blk * tk:(blk + 1) * tk]

    fl = _dot(h, wf_ref[...]) + fb_ref[...]
    ls = jnp.minimum(fl, 0.0) - jnp.log1p(jnp.exp(-jnp.abs(fl)))
    ri = lax.broadcasted_iota(jnp.int32, (tm, tm), 0)
    ci = lax.broadcasted_iota(jnp.int32, (tm, tm), 1)
    lower = (ci <= ri).astype(BF16)
    c = _dot_exact_lhs(lower, ls) + carry_ref[0:1, :]
    carry_ref[...] = jnp.broadcast_to(c[tm - 1:tm, :], carry_ref.shape)
    c1, c2, c3 = _split3(c * LOG2E)
    lane = lax.broadcasted_iota(jnp.int32, c.shape, 1)
    cs_ref[...] = jnp.where(lane < FOX_HEADS, c1, jnp.where(lane < 2 * FOX_HEADS, c2, c3))

    hg = _dot(h, wh_ref[...])
    w = HGRN_WIDTH
    lb = lb_ref[...]
    f = lb + (1.0 - lb) * _sigmoid(hg[:, w:2 * w])
    g = hg[:, 3 * w:4 * w]
    hg_ref[:, 0:w] = hg[:, 0:w]
    hg_ref[:, w:2 * w] = jnp.log(f)
    hg_ref[:, 2 * w:3 * w] = f
    hg_ref[:, 3 * w:4 * w] = hg[:, 2 * w:3 * w]
    hg_ref[:, 4 * w:5 * w] = g * (0.5 * (jnp.tanh(0.5 * g) + 1.0))
    u_ref[...] = _dot(h, wu_ref[...]).reshape(u_ref.shape)


def _proj(x, gain, wqk, wvt, wf, wh, wu, qg, kg, fb, lb):
    b, l, d = x.shape
    tm = min(PROJ_ROWS, l)
    tk = min(ATTN_BLOCK, l)
    hw = 5 * HGRN_WIDTH
    full = lambda shape: pl.BlockSpec(shape, lambda bi, i: tuple(0 for _ in shape))
    est = 2 * (d * (512 + 256 + 128 + hw + 256) * 2) + 2 * tm * d * 4 + 2 * tm * (hw + 256 + 512) * 4 + 2 * tm * hw * 4
    return pl.pallas_call(
        _proj_kernel,
        grid=(b, l // tm),
        in_specs=[
            pl.BlockSpec((None, tm, d), lambda bi, i: (bi, i, 0)),
            full((1, d)), full(wqk.shape), full(wvt.shape), full(wf.shape), full(wh.shape),
            full(wu.shape), full((1, FOX_WIDTH)), full((1, FOX_WIDTH)), full((1, 128)),
            full((1, HGRN_WIDTH)),
        ],
        out_specs=[
            pl.BlockSpec((None, tm, FOX_WIDTH), lambda bi, i: (bi, i, 0)),
            pl.BlockSpec((None, tm, FOX_WIDTH), lambda bi, i: (bi, i, 0)),
            pl.BlockSpec((None, tm // tk, FOX_WIDTH, tk), lambda bi, i: (bi, i, 0, 0)),
            pl.BlockSpec((None, tm, 128), lambda bi, i: (bi, i, 0)),
            pl.BlockSpec((None, tm, hw), lambda bi, i: (bi, i, 0)),
            pl.BlockSpec((tm // S5_CHUNK, None, S5_CHUNK, S5_WIDTH), lambda bi, i: (i, bi, 0, 0)),
        ],
        out_shape=[
            jax.ShapeDtypeStruct((b, l, FOX_WIDTH), BF16),
            jax.ShapeDtypeStruct((b, l, FOX_WIDTH), BF16),
            jax.ShapeDtypeStruct((b, l // tk, FOX_WIDTH, tk), BF16),
            jax.ShapeDtypeStruct((b, l, 128), BF16),
            jax.ShapeDtypeStruct((b, l, hw), F32),
            jax.ShapeDtypeStruct((l // S5_CHUNK, b, S5_CHUNK, S5_WIDTH), F32),
        ],
        scratch_shapes=[pltpu.VMEM((V7X_SUBLANES, 128), F32)],
        compiler_params=pltpu.CompilerParams(
            dimension_semantics=("arbitrary", "arbitrary"), vmem_limit_bytes=_vmem_limit(est)),
        name="proj",
    )(x, gain, wqk, wvt, wf, wh, wu, qg, kg, fb, lb)


def _attn_kernel(bounded_ref, q_ref, k_ref, vt_ref, cs_ref, e_ref, pq_ref, pk_ref, oq_ref, ok_ref,
                 o_ref, ka_ref, qa_ref, m_ref, l_ref, acc_ref, s_ref, p_ref):
    tq = o_ref.shape[0]
    i = pl.program_id(1)

    @pl.when(i == 0)
    def _():
        cs = cs_ref[...]
        ka_ref[...] = (_dot(k_ref[...], e_ref[...]) + _dot(cs, pk_ref[...]) + ok_ref[...]).astype(BF16)
        qa_ref[...] = (_dot(q_ref[...], e_ref[...]) + _dot(cs, pq_ref[...]) + oq_ref[...]).astype(BF16)

    l_ref[...] = jnp.zeros_like(l_ref)
    acc_ref[...] = jnp.zeros_like(acc_ref)
    nblk = ATTN_KV_BLOCKS
    tk = nblk * tq
    nfull = i // nblk
    qrows = pl.ds(pl.multiple_of(i * tq, tq), tq)
    key_minus_qry = (lax.broadcasted_iota(jnp.int32, (tk, tq), 0)
                     - lax.broadcasted_iota(jnp.int32, (tk, tq), 1))

    def causal(s, jj):
        return jnp.where(key_minus_qry <= (i - jj * nblk) * tq, s, -jnp.inf)

    def pv_dot(jj, hd, p):
        head_rows = slice(hd * FOX_HEAD_DIM, (hd + 1) * FOX_HEAD_DIM)
        pv = _dot(vt_ref[jj * nblk, head_rows, :], p[0:tq, :])
        for bk in range(1, nblk):
            pv = pv + _dot(vt_ref[jj * nblk + bk, head_rows, :], p[bk * tq:(bk + 1) * tq, :])
        return pv

    def bounded_steps(steps):
        for slot, jj in enumerate(steps):
            rows = pl.ds(pl.multiple_of(jj * tk, tk), tk)
            for hd in range(FOX_HEADS):
                slab = slice(hd * 128, (hd + 1) * 128)
                p = jnp.exp2(causal(_dot_nt(ka_ref[rows, slab], qa_ref[qrows, slab]), jj))
                l_ref[hd] = l_ref[hd] + jnp.sum(p, axis=0, keepdims=True)
                p_ref[slot, hd] = p.astype(BF16)
        for hd in range(FOX_HEADS):
            pv = pv_dot(steps[0], hd, p_ref[0, hd])
            for slot, jj in enumerate(steps[1:], 1):
                pv = pv + pv_dot(jj, hd, p_ref[slot, hd])
            acc_ref[hd] = acc_ref[hd] + pv

    @pl.when(bounded_ref[0] == 1)
    def _():
        def pair(kk, carry):
            bounded_steps([2 * kk, 2 * kk + 1])
            return carry

        lax.fori_loop(0, (nfull + 1) // 2, pair, 0)

        @pl.when(nfull % 2 == 0)
        def _():
            bounded_steps([nfull])

    @pl.when(bounded_ref[0] == 0)
    def _():
        _attn_online(i, nfull, tk, qrows, causal, pv_dot, ka_ref, qa_ref, m_ref, l_ref, acc_ref,
                     s_ref, p_ref.at[0])

    outs = [acc_ref[hd] / l_ref[hd] for hd in range(FOX_HEADS)]
    o_ref[...] = jnp.concatenate(outs, axis=0).T.astype(o_ref.dtype)


def _attn_online(i, nfull, tk, qrows, causal, pv_dot, ka_ref, qa_ref, m_ref, l_ref, acc_ref, s_ref, p_ref):
    m_ref[...] = jnp.full(m_ref.shape, -jnp.inf, F32)

    def logits(jj, slot):
        rows = pl.ds(pl.multiple_of(jj * tk, tk), tk)
        for hd in range(FOX_HEADS):
            slab = slice(hd * 128, (hd + 1) * 128)
            s_ref[slot, hd] = _dot_nt(ka_ref[rows, slab], qa_ref[qrows, slab])

    def update(jj, slot, last):
        alphas = []
        for hd in range(FOX_HEADS):
            s = s_ref[slot, hd]
            if last:
                s = causal(s, jj)
            m_old = m_ref[hd]
            m_new = jnp.maximum(m_old, jnp.max(s, axis=0, keepdims=True))
            alpha = jnp.exp2(m_old - m_new)
            p = jnp.exp2(s - m_new)
            m_ref[hd] = m_new
            l_ref[hd] = alpha * l_ref[hd] + jnp.sum(p, axis=0, keepdims=True)
            p_ref[hd] = p.astype(BF16)
            alphas.append(alpha)
        for hd in range(FOX_HEADS):
            acc_ref[hd] = alphas[hd] * acc_ref[hd] + pv_dot(jj, hd, p_ref[hd])

    def pair(kk, carry):
        logits(2 * kk + 1, 1)
        update(2 * kk, 0, False)
        logits(2 * kk + 2, 0)
        update(2 * kk + 1, 1, False)
        return carry

    logits(0, 0)
    npair = nfull // 2
    lax.fori_loop(0, npair, pair, 0)

    @pl.when(nfull % 2 == 0)
    def _():
        update(2 * npair, 0, True)

    @pl.when(nfull % 2 == 1)
    def _():
        logits(2 * npair + 1, 1)
        update(2 * npair, 0, False)
        update(2 * npair + 1, 1, True)


def _attn_placement():
    w = FOX_HEADS * 128
    e = np.zeros((FOX_WIDTH, w), np.float32)
    cols = np.arange(FOX_WIDTH)
    e[cols, (cols // FOX_HEAD_DIM) * 128 + cols % FOX_HEAD_DIM] = 1.0
    pq, pk = np.zeros((128, w), np.float32), np.zeros((128, w), np.float32)
    oq, ok = np.zeros((1, w), np.float32), np.zeros((1, w), np.float32)
    for hd in range(FOX_HEADS):
        base = hd * 128 + FOX_HEAD_DIM
        for term in range(3):
            pk[term * FOX_HEADS + hd, base + term] = -1.0
            oq[0, base + term] = 1.0
            pq[term * FOX_HEADS + hd, base + 3 + term] = 1.0
            ok[0, base + 3 + term] = 1.0
    return (jnp.asarray(e, BF16), jnp.asarray(pq, BF16), jnp.asarray(pk, BF16),
            jnp.asarray(oq), jnp.asarray(ok))


def _attn(q, k, vt, cs, q_gain, k_gain):
    b, l, w = q.shape
    nkv, tk = vt.shape[1], vt.shape[3]
    tq = tk
    wa = FOX_HEADS * 128
    e, pq, pk, oq, ok = _attn_placement()
    full = lambda shape: pl.BlockSpec(shape, lambda bi, i: tuple(0 for _ in shape))
    seq = lambda width: pl.BlockSpec((None, l, width), lambda bi, i: (bi, 0, 0))
    assert (l // tq) % ATTN_KV_BLOCKS == 0
    logit_bound = (1.01 * FOX_HEAD_DIM * LOG2E / math.sqrt(FOX_HEAD_DIM)
                   * jnp.max(jnp.abs(q_gain.astype(F32))) * jnp.max(jnp.abs(k_gain.astype(F32))))
    bounded = (logit_bound < ATTN_MAX_FREE_LOGIT).astype(jnp.int32).reshape(1)
    est = (2 * (l * w * 2 * 2 + l * 128 * 2) + 2 * l * wa * (2 + 4 + 4)
           + FOX_HEADS * ATTN_KV_BLOCKS * tq * tq * (2 * 4 + 2 * 2 + 2 * 4) + FOX_HEADS * 128 * tq * 4)
    return pl.pallas_call(
        _attn_kernel,
        grid=(b, l // tq),
        in_specs=[
            pl.BlockSpec(memory_space=pltpu.SMEM),
            seq(w), seq(w),
            pl.BlockSpec((None, nkv, w, tk), lambda bi, i: (bi, 0, 0, 0)),
            seq(128),
            full(e.shape), full(pq.shape), full(pk.shape), full(oq.shape), full(ok.shape),
        ],
        out_specs=pl.BlockSpec((None, tq, w), lambda bi, i: (bi, i, 0)),
        out_shape=jax.ShapeDtypeStruct((b, l, w), BF16),
        scratch_shapes=[pltpu.VMEM((l, wa), BF16), pltpu.VMEM((l, wa), BF16),
                        pltpu.VMEM((FOX_HEADS, 1, tq), F32), pltpu.VMEM((FOX_HEADS, 1, tq), F32),
                        pltpu.VMEM((FOX_HEADS, FOX_HEAD_DIM, tq), F32),
                        pltpu.VMEM((2, FOX_HEADS, ATTN_KV_BLOCKS * tq, tq), F32),
                        pltpu.VMEM((2, FOX_HEADS, ATTN_KV_BLOCKS * tq, tq), BF16)],
        compiler_params=pltpu.CompilerParams(
            dimension_semantics=("arbitrary", "arbitrary"), vmem_limit_bytes=_vmem_limit(est)),
        name="fox_attn",
    )(bounded, q, k, vt, cs, e, pq, pk, oq, ok)


def _hgrn_levels(c):
    return [c >> (s + 1) for s in range(int(math.log2(c)))]


def _hgrn_level_ids(c):
    t = np.arange(c)[:, None]
    s = np.arange(c)[None, :]
    lvl = np.floor(np.log2(np.maximum(t ^ s, 1))).astype(np.int32) + 1
    return jnp.asarray(np.where(s > t, -1, np.where(s == t, 0, lvl)).astype(np.int32))


def _hgrn_kernel(q_ref, lf_ref, f_ref, i_ref, gate_ref, og_ref, lvl_ref, o_ref, st_ref, b_ref):
    c = q_ref.shape[0]

    @pl.when(pl.program_id(1) == 0)
    def _():
        st_ref[...] = jnp.zeros_like(st_ref)

    f = f_ref[...]
    key = 1.0 - f
    q = q_ref[...]
    ti = lax.broadcasted_iota(jnp.int32, (c, c), 0)
    si = lax.broadcasted_iota(jnp.int32, (c, c), 1)
    lower = (si <= ti).astype(BF16)
    b = _dot_exact_lhs(lower, lf_ref[...], terms=2)
    b_ref[...] = b
    row = lax.broadcasted_iota(jnp.int32, b.shape, 0)

    def ref_rows(m):
        parts = [jnp.broadcast_to(b_ref[g0 + m - 1:g0 + m, :], (2 * m, b.shape[1]))
                 for g0 in range(0, c, 2 * m)]
        return jnp.concatenate(parts, axis=0) if len(parts) > 1 else parts[0]

    def level_decay(m):
        if m == 1:
            return jnp.where(row % 2 == 1, f, 1.0)
        if m == 2:
            off = row % 4
            nxt, prv = pltpu.roll(f, c - 1, axis=0), pltpu.roll(f, 1, axis=0)
            return jnp.where(off == 0, nxt, jnp.where(off == 1, 1.0, jnp.where(off == 2, f, f * prv)))
        return jnp.exp2(jnp.abs(b - ref_rows(m)) * (-LOG2E))

    lvl = lvl_ref[...]
    qb, kb = q.astype(BF16), key.astype(BF16)
    scores = [jnp.zeros((c, c), F32) for _ in range(HGRN_HEADS)]
    for idx, m in enumerate([0] + _hgrn_levels(c)[::-1]):
        if m == 0:
            z, y = qb, kb
        else:
            e = level_decay(m).astype(BF16)
            z, y = qb * e, kb * e
        mask = lvl == idx
        for hd in range(HGRN_HEADS):
            sl = slice(hd * HGRN_DIM, (hd + 1) * HGRN_DIM)
            scores[hd] = jnp.where(mask, _dot_nt(z[:, sl], y[:, sl]), scores[hd])

    b_last = b[c - 1:c, :]
    q_in = (q * jnp.exp(b)).astype(BF16)
    k_out = (key * jnp.exp(b_last - b)).astype(BF16)
    v = i_ref[...].astype(BF16)
    gate = gate_ref[...]
    og = og_ref[...]
    decay = jnp.exp(b_last)
    for hd in range(HGRN_HEADS):
        sl = slice(hd * HGRN_DIM, (hd + 1) * HGRN_DIM)
        st = st_ref[hd]
        o = _dot(scores[hd].astype(BF16), v[:, sl]) + _dot_nt(q_in[:, sl], st.astype(BF16))
        st_ref[hd] = st * decay[:, sl] + _dot_tn(v[:, sl], k_out[:, sl])
        o_ref[:, sl] = (_rms_norm(o, og[:, sl]) * gate[:, sl]).astype(o_ref.dtype)


def _hgrn(hg, og):
    b, l, _ = hg.shape
    c = min(HGRN_CHUNK, l)
    w = HGRN_WIDTH
    part = lambda p: pl.BlockSpec((None, c, w), lambda bi, i, p=p: (bi, i, p))
    est = 2 * 6 * c * w * 4 + 16 * c * w * 4 + HGRN_HEADS * HGRN_DIM * HGRN_DIM * 4
    return pl.pallas_call(
        _hgrn_kernel,
        grid=(b, l // c),
        in_specs=[part(0), part(1), part(2), part(3), part(4),
                  pl.BlockSpec((1, w), lambda bi, i: (0, 0)),
                  pl.BlockSpec((c, c), lambda bi, i: (0, 0))],
        out_specs=pl.BlockSpec((None, c, w), lambda bi, i: (bi, i, 0)),
        out_shape=jax.ShapeDtypeStruct((b, l, w), BF16),
        scratch_shapes=[pltpu.VMEM((HGRN_HEADS, HGRN_DIM, HGRN_DIM), F32),
                        pltpu.VMEM((c, w), F32)],
        compiler_params=pltpu.CompilerParams(
            dimension_semantics=("arbitrary", "arbitrary"), vmem_limit_bytes=_vmem_limit(est)),
        name="hgrn2",
    )(hg, hg, hg, hg, hg, og, _hgrn_level_ids(c))


def _s5_kernel(u_ref, kd_ref, pin_ref, pout_ref, ar_ref, ai_ref, y_ref, xr_ref, xi_ref):
    nc, nb, cs, w = u_ref.shape
    ns = ar_ref.shape[1]

    @pl.when(pl.program_id(0) == 0)
    def _():
        xr_ref[...] = jnp.zeros_like(xr_ref)
        xi_ref[...] = jnp.zeros_like(xi_ref)

    us = [u_ref[:, :, s, :].reshape(nc * nb, w).astype(BF16) for s in range(cs)]
    win = _dot(us[0], pin_ref[0])
    for s in range(1, cs):
        win = win + _dot(us[s], pin_ref[s])

    ar, ai = ar_ref[...], ai_ref[...]
    xr, xi = xr_ref[...], xi_ref[...]
    prev_r, prev_i = [], []
    for c in range(nc):
        prev_r.append(xr)
        prev_i.append(xi)
        wr = win[c * nb:(c + 1) * nb, :ns]
        wi = win[c * nb:(c + 1) * nb, ns:]
        xr, xi = ar * xr - ai * xi + wr, ar * xi + ai * xr + wi
    xr_ref[...] = xr
    xi_ref[...] = xi
    xp = jnp.concatenate([jnp.concatenate(prev_r, axis=0), jnp.concatenate(prev_i, axis=0)],
                         axis=1).astype(BF16)

    for t in range(cs):
        lags = jnp.concatenate([us[t - d] for d in range(t + 1)], axis=1) if t else us[0]
        yt = _dot(xp, pout_ref[t]) + _dot(lags, kd_ref[0:(t + 1) * w, :])
        y_ref[:, :, t, :] = yt.reshape(nc, nb, w)


def _s5(u4, kd, pin, pout, ar, ai):
    nchunk, nb, cs, w = u4.shape
    nc = min(S5_TILE // cs, nchunk)
    const = dict(pipeline_mode=pl.Buffered(1))
    full = lambda shape: pl.BlockSpec(shape, lambda i: tuple(0 for _ in shape), **const)
    rows = nc * cs * nb
    chunked = pl.BlockSpec((nc, nb, cs, w), lambda i: (i, 0, 0, 0))
    est = (4 * rows * w * 4 + (kd.size + pin.size + pout.size) * 2 + rows // cs * 2 * S5_NSTATE * (4 + 4 + 2)
           + rows * w * (2 + 2 + 4))
    return pl.pallas_call(
        _s5_kernel,
        grid=(nchunk // nc,),
        in_specs=[chunked, full(kd.shape), full(pin.shape), full(pout.shape), full(ar.shape), full(ai.shape)],
        out_specs=chunked,
        out_shape=jax.ShapeDtypeStruct(u4.shape, F32),
        scratch_shapes=[pltpu.VMEM((nb, S5_NSTATE), F32), pltpu.VMEM((nb, S5_NSTATE), F32)],
        compiler_params=pltpu.CompilerParams(
            dimension_semantics=("arbitrary",), vmem_limit_bytes=_vmem_limit(est)),
        name="s5",
    )(u4, kd, pin, pout, ar, ai)


def _s5_operators(lam_re, lam_im, log_step, b_re, b_im, c_re, c_im):
    lr, li = lam_re.astype(F32), lam_im.astype(F32)
    dt = jnp.exp(log_step.astype(F32))[:, None]
    mag = jnp.exp(lr * dt)
    lbar_re, lbar_im = mag * jnp.cos(li * dt), mag * jnp.sin(li * dt)
    nr, ni = lbar_re - 1.0, lbar_im
    den = lr * lr + li * li
    coef_re = (nr * lr + ni * li) / den
    coef_im = (ni * lr - nr * li) / den
    br, bi = b_re.astype(F32), b_im.astype(F32)
    bbar_re = coef_re[..., None] * br - coef_im[..., None] * bi
    bbar_im = coef_re[..., None] * bi + coef_im[..., None] * br
    cr, ci = c_re.astype(F32), c_im.astype(F32)

    pr, pi = [jnp.ones_like(lbar_re)], [jnp.zeros_like(lbar_re)]
    for _ in range(S5_CHUNK):
        pr, pi = (pr + [pr[-1] * lbar_re - pi[-1] * lbar_im],
                  pi + [pr[-1] * lbar_im + pi[-1] * lbar_re])
    pw_re, pw_im = jnp.stack(pr), jnp.stack(pi)
    cs = S5_CHUNK

    def block_diag(compact, rows_group, cols_group, reps):
        tiled = jnp.tile(compact, reps)
        r = lax.broadcasted_iota(jnp.int32, tiled.shape, tiled.ndim - 2)
        c = lax.broadcasted_iota(jnp.int32, tiled.shape, tiled.ndim - 1)
        return jnp.where(rows_group(r) == cols_group(c), tiled, 0.0).astype(BF16)

    lane_group = lambda c: c // S5_GROUP
    state_group = lambda c: (c % S5_NSTATE) // S5_STATE

    ab_re = pw_re[:cs, :, :, None] * bbar_re - pw_im[:cs, :, :, None] * bbar_im
    ab_im = pw_re[:cs, :, :, None] * bbar_im + pw_im[:cs, :, :, None] * bbar_re
    taps = jnp.sum(cr[None, :, :, :, None] * ab_re[:, :, None] - ci[None, :, :, :, None] * ab_im[:, :, None],
                   axis=3)
    kd = block_diag(taps.transpose(0, 1, 3, 2).reshape(cs * S5_WIDTH, S5_GROUP),
                    lambda r: (r // S5_GROUP) % S5_GROUPS, lane_group, (1, S5_GROUPS))

    rev = cs - 1 - jnp.arange(cs)
    ab = jnp.stack([ab_re[rev], ab_im[rev]], axis=1)
    pin = block_diag(ab.transpose(0, 4, 1, 2, 3).reshape(cs, S5_GROUP, 2 * S5_NSTATE),
                     lane_group, state_group, (1, S5_GROUPS, 1))

    qr, qi = pw_re[1:], pw_im[1:]
    po_re = cr[None] * qr[:, :, None, :] - ci[None] * qi[:, :, None, :]
    po_im = -(cr[None] * qi[:, :, None, :] + ci[None] * qr[:, :, None, :])
    po = jnp.stack([po_re, po_im], axis=1)
    pout = block_diag(po.transpose(0, 1, 2, 4, 3).reshape(cs, 2 * S5_NSTATE, S5_GROUP),
                      state_group, lane_group, (1, 1, S5_GROUPS))
    a_re = pw_re[cs].reshape(1, S5_NSTATE)
    a_im = pw_im[cs].reshape(1, S5_NSTATE)
    return kd, pin, pout, a_re, a_im


def _out_ffn_kernel(x_ref, oa_ref, ob_ref, y_ref, u_ref, d_ref, gw_ref, gb_ref, wa_ref, wb_ref, wc_ref,
                    g_ref, wg_hbm, wu_hbm, wd_hbm, o_ref, wg_ref, wu_ref, wd_ref, wide_stage, tall_stage,
                    sem, *, layer):
    @pl.when((pl.program_id(0) == 0) & (pl.program_id(1) == 0))
    def _():
        _fetch_ffn_weights(layer, wg_hbm, wu_hbm, wd_hbm, wg_ref, wu_ref, wd_ref, wide_stage, tall_stage, sem)

    tm = x_ref.shape[0]
    y = y_ref[...].reshape(tm, S5_WIDTH) + d_ref[...] * u_ref[...].reshape(tm, S5_WIDTH)
    z = 0.5 * y * (1.0 + jnp.tanh(math.sqrt(2.0 / math.pi) * (y + 0.044715 * (y * y * y))))
    oc = z * _sigmoid(_dot(z.astype(BF16), gw_ref[...]) + gb_ref[...])
    x = (x_ref[...] + _dot(oa_ref[...], wa_ref[...]) + _dot(ob_ref[...], wb_ref[...])
         + _dot(oc.astype(BF16), wc_ref[...]))
    o_ref[...] = _swiglu_residual(x, g_ref[...], wg_ref, wu_ref, wd_ref)


def _out_ffn(x, oa, ob, y_lb, u_lb, dvec, gw, gb, wa, wb, wc, gain, wg, wu, wd, layer):
    b, l, d = x.shape
    f = wg.shape[2]
    hbm = pl.BlockSpec(memory_space=pl.ANY)
    tm = min(OUT_ROWS, l)
    rows = lambda w: pl.BlockSpec((None, tm, w), lambda bi, i: (bi, i, 0))
    token_major = pl.BlockSpec((tm // S5_CHUNK, None, S5_CHUNK, S5_WIDTH), lambda bi, i: (i, bi, 0, 0))
    full = lambda shape: pl.BlockSpec(shape, lambda bi, i: tuple(0 for _ in shape),
                                      pipeline_mode=pl.Buffered(1))
    est = (2 * (2 * tm * d * 4 + tm * (256 * 2 + 512 * 2 + 256 * 8)) + 2 * (d * d + 256 * 256) + 3 * d * f * 2
           + 2 * 4 * (FFN_STAGE_ROWS_WIDE * f + FFN_STAGE_ROWS_TALL * d) + tm * f * (4 + 4 + 2 + 4) + 4 * tm * d * 4)
    return pl.pallas_call(
        functools.partial(_out_ffn_kernel, layer=layer),
        grid=(b, l // tm),
        in_specs=[rows(d), rows(FOX_WIDTH), rows(HGRN_WIDTH), token_major, token_major,
                  full((1, S5_WIDTH)), full(gw.shape), full((1, S5_WIDTH)),
                  full(wa.shape), full(wb.shape), full(wc.shape),
                  full((1, d)), hbm, hbm, hbm],
        out_specs=rows(d),
        out_shape=jax.ShapeDtypeStruct((b, l, d), F32),
        scratch_shapes=_ffn_weight_scratch(d, f),
        compiler_params=pltpu.CompilerParams(
            dimension_semantics=("arbitrary", "arbitrary"), vmem_limit_bytes=_vmem_limit(est)),
        name="out_ffn",
    )(x, oa, ob, y_lb, u_lb, dvec, gw, gb, wa, wb, wc, gain, wg, wu, wd)


def kernel(x, ffn1_norm, ffn1_w_gate, ffn1_w_up, ffn1_w_down, mix_norm, w_in, fox_f_bias, fox_q_gain, fox_k_gain, hgrn_lb, hgrn_o_gain, s5_lambda_re, s5_lambda_im, s5_log_step, s5_B_re, s5_B_im, s5_C_re, s5_C_im, s5_D, s5_glu_w, s5_glu_b, w_out, ffn2_norm, ffn2_w_gate, ffn2_w_up, ffn2_w_down):
    bsz, seq, d = x.shape
    depth = w_in.shape[0]
    t = bsz * seq

    lb_all = jnp.cumsum(jax.nn.softmax(hgrn_lb.astype(F32), axis=0), axis=0)
    lb_all = lb_all - lb_all[0:1]

    o_q, o_k, o_v = 0, FOX_WIDTH, 2 * FOX_WIDTH
    o_f = 3 * FOX_WIDTH
    o_h = o_f + FOX_HEADS
    o_u = o_h + 4 * HGRN_WIDTH

    ffn1 = [w.astype(F32) for w in (ffn1_w_gate, ffn1_w_up, ffn1_w_down)]
    ffn2 = [w.astype(F32) for w in (ffn2_w_gate, ffn2_w_up, ffn2_w_down)]

    for l in range(depth):
        x = _ffn(x.reshape(t, d), ffn1_norm[l][None], *ffn1, layer=l).reshape(bsz, seq, d)

        w = w_in[l]
        wf = jnp.zeros((d, 128), F32).at[:, :3 * FOX_HEADS].set(jnp.tile(w[:, o_f:o_h], (1, 3))).astype(BF16)
        fb = jnp.zeros((1, 128), F32).at[0, :3 * FOX_HEADS].set(jnp.tile(fox_f_bias[l].astype(F32), 3))
        q, k, vt, cs, hg, u_lb = _proj(
            x, mix_norm[l][None], w[:, o_q:o_v].astype(BF16), w[:, o_v:o_f].T.astype(BF16), wf,
            w[:, o_h:o_u].astype(BF16), w[:, o_u:].astype(BF16),
            jnp.tile(fox_q_gain[l].astype(F32), FOX_HEADS)[None],
            jnp.tile(fox_k_gain[l].astype(F32), FOX_HEADS)[None], fb, lb_all[l][None])

        o_a = _attn(q, k, vt, cs, fox_q_gain[l], fox_k_gain[l])
        o_b = _hgrn(hg, hgrn_o_gain[l].astype(F32)[None])
        ops = _s5_operators(s5_lambda_re[l], s5_lambda_im[l], s5_log_step[l],
                            s5_B_re[l], s5_B_im[l], s5_C_re[l], s5_C_im[l])
        y_lb = _s5(u_lb, *ops)

        wo = w_out[l].astype(BF16)
        x = _out_ffn(x, o_a, o_b, y_lb, u_lb, s5_D[l].astype(F32)[None], s5_glu_w[l].astype(BF16),
                     s5_glu_b[l].astype(F32)[None], wo[:FOX_WIDTH], wo[FOX_WIDTH:FOX_WIDTH + HGRN_WIDTH],
                     wo[FOX_WIDTH + HGRN_WIDTH:], ffn2_norm[l][None], *ffn2, layer=l)
    return x
```

```python
import functools
import math

import jax
import jax.numpy as jnp
import numpy as np
from jax import lax
from jax.experimental import pallas as pl
from jax.experimental.pallas import tpu as pltpu

F32 = jnp.float32
BF16 = jnp.bfloat16

EPS = 1e-6
D_MODEL = 1024
D_FF = 2816
FOX_HEADS = 4
FOX_HEAD_DIM = 64
FOX_WIDTH = FOX_HEADS * FOX_HEAD_DIM
HGRN_HEADS = 4
HGRN_DIM = 128
HGRN_WIDTH = HGRN_HEADS * HGRN_DIM
S5_GROUPS = 16
S5_GROUP = 16
S5_STATE = 64
S5_WIDTH = S5_GROUPS * S5_GROUP
S5_NSTATE = S5_GROUPS * S5_STATE
S5_CHUNK = 8
S5_TILE = 128
CAST_ROWS = 256
LOG2E = math.log2(math.e)

V7X_VMEM_BYTES = 64 * 1024 * 1024
V7X_SUBLANES = 8

ATTN_BLOCK = 256
ATTN_KV_BLOCKS = 2
ATTN_MAX_FREE_LOGIT = 40.0
HGRN_CHUNK = 128
FFN_ROWS = 512
FFN_SPLIT = 2
PROJ_ROWS = 512
OUT_ROWS = 512


def _vmem_limit(estimate_bytes):
    return int(min(estimate_bytes * 5 // 4 + (4 << 20), V7X_VMEM_BYTES - (4 << 20)))


def _rms_norm(x, gain):
    ms = jnp.mean(x * x, axis=-1, keepdims=True)
    return x * lax.rsqrt(ms + EPS) * gain


def _dot(a, b):
    return jnp.dot(a, b, preferred_element_type=F32)


def _dot_nt(a, b):
    return lax.dot_general(a, b, (((1,), (1,)), ((), ())), preferred_element_type=F32)


def _dot_tn(a, b):
    return lax.dot_general(a, b, (((0,), (0,)), ((), ())), preferred_element_type=F32)


def _split3(a):
    a1 = a.astype(BF16)
    r1 = a - a1.astype(F32)
    a2 = r1.astype(BF16)
    a3 = (r1 - a2.astype(F32)).astype(BF16)
    return a1, a2, a3


def _dot_exact_rhs(a, rhs, terms=3):
    parts = _split3(a)[:terms]
    out = _dot(parts[0], rhs)
    for part in parts[1:]:
        out = out + _dot(part, rhs)
    return out


def _dot_exact_lhs(lhs, a, terms=3):
    parts = _split3(a)[:terms]
    out = _dot(lhs, parts[0])
    for part in parts[1:]:
        out = out + _dot(lhs, part)
    return out


def _sigmoid(x):
    return 1.0 / (1.0 + jnp.exp(-x))


def _cumsum_rows(a):
    row = lax.broadcasted_iota(jnp.int32, a.shape, 0)
    shift = 1
    while shift < a.shape[0]:
        a = a + jnp.where(row >= shift, pltpu.roll(a, shift, axis=0), 0.0)
        shift *= 2
    return a


def _cast_kernel(w_ref, o_ref):
    o_ref[...] = w_ref[...].astype(o_ref.dtype)


def _to_bf16(w):
    depth, rows, cols = w.shape
    tr = CAST_ROWS
    assert rows % tr == 0
    spec = pl.BlockSpec((None, tr, cols), lambda l, i: (l, i, 0))
    return pl.pallas_call(
        _cast_kernel,
        grid=(depth, rows // tr),
        in_specs=[spec],
        out_specs=spec,
        out_shape=jax.ShapeDtypeStruct(w.shape, BF16),
        compiler_params=pltpu.CompilerParams(
            dimension_semantics=("arbitrary", "arbitrary"),
            vmem_limit_bytes=_vmem_limit(2 * tr * cols * (4 + 2))),
        name="cast_bf16",
    )(w)


def _swiglu_residual(x, gain, wg_ref, wu_ref, wd_ref):
    h = _rms_norm(x, gain).astype(BF16)
    gate = _dot(h, wg_ref[...])
    up = _dot(h, wu_ref[...])
    act = (gate * _sigmoid(gate) * up).astype(BF16)
    return x + 0.5 * _dot(act, wd_ref[...])


def _ffn_kernel(x_ref, g_ref, wg_ref, wu_ref, wd_ref, o_ref):
    part_rows = x_ref.shape[0] // FFN_SPLIT
    for part in range(FFN_SPLIT):
        rows = slice(part * part_rows, (part + 1) * part_rows)
        o_ref[rows, :] = _swiglu_residual(x_ref[rows, :], g_ref[...], wg_ref, wu_ref, wd_ref)


def _ffn(x2d, gain, wg, wu, wd, layer):
    t, d = x2d.shape
    f = wg.shape[2]
    tm = min(FFN_ROWS, t)
    const = dict(pipeline_mode=pl.Buffered(1))
    est = 3 * d * f * 2 + 4 * tm * d * 4 + tm * f * (4 + 4 + 2 + 4) + tm * d * 4
    return pl.pallas_call(
        _ffn_kernel,
        grid=(t // tm,),
        in_specs=[
            pl.BlockSpec((tm, d), lambda i: (i, 0)),
            pl.BlockSpec((1, d), lambda i: (0, 0)),
            pl.BlockSpec((None, d, f), lambda i: (layer, 0, 0), **const),
            pl.BlockSpec((None, d, f), lambda i: (layer, 0, 0), **const),
            pl.BlockSpec((None, f, d), lambda i: (layer, 0, 0), **const),
        ],
        out_specs=pl.BlockSpec((tm, d), lambda i: (i, 0)),
        out_shape=jax.ShapeDtypeStruct((t, d), F32),
        compiler_params=pltpu.CompilerParams(
            dimension_semantics=("arbitrary",), vmem_limit_bytes=_vmem_limit(est)),
        name="ffn",
    )(x2d, gain, wg, wu, wd)


def _proj_kernel(x_ref, g_ref, wqk_ref, wvt_ref, wf_ref, wh_ref, wu_ref, qg_ref, kg_ref,
                 fb_ref, lb_ref, q_ref, k_ref, vt_ref, cs_ref, hg_ref, u_ref, carry_ref):
    tm = x_ref.shape[0]
    tk = vt_ref.shape[2]

    @pl.when(pl.program_id(1) == 0)
    def _():
        carry_ref[...] = jnp.zeros_like(carry_ref)

    h = _rms_norm(x_ref[...], g_ref[...]).astype(BF16)

    li = lax.broadcasted_iota(jnp.int32, (FOX_WIDTH, FOX_WIDTH), 0) // FOX_HEAD_DIM
    lj = lax.broadcasted_iota(jnp.int32, (FOX_WIDTH, FOX_WIDTH), 1) // FOX_HEAD_DIM
    group = (li == lj).astype(BF16)

    def head_norm(a, gain):
        ms = _dot_exact_rhs(a * a, group, terms=1) * (1.0 / FOX_HEAD_DIM)
        return a * lax.rsqrt(ms + EPS) * gain

    qk = _dot(h, wqk_ref[...])
    q_ref[...] = (head_norm(qk[:, :FOX_WIDTH], qg_ref[...]) * (LOG2E / math.sqrt(FOX_HEAD_DIM))).astype(BF16)
    k_ref[...] = head_norm(qk[:, FOX_WIDTH:], kg_ref[...]).astype(BF16)
    vt = _dot_nt(wvt_ref[...], h).astype(BF16)
    for blk in range(tm // tk):
        vt_ref[blk] = vt[:, blk * tk:(blk + 1) * tk]

    fl = _dot(h, wf_ref[...]) + fb_ref[...]
    ls = jnp.minimum(fl, 0.0) - jnp.log1p(jnp.exp(-jnp.abs(fl)))
    c = _cumsum_rows(ls) + carry_ref[0:1, :]
    carry_ref[...] = jnp.broadcast_to(c[tm - 1:tm, :], carry_ref.shape)
    c1, c2, c3 = _split3(c * LOG2E)
    lane = lax.broadcasted_iota(jnp.int32, c.shape, 1)
    cs_ref[...] = jnp.where(lane < FOX_HEADS, c1, jnp.where(lane < 2 * FOX_HEADS, c2, c3))

    hg = _dot(h, wh_ref[...])
    w = HGRN_WIDTH
    lb = lb_ref[...]
    f = lb + (1.0 - lb) * _sigmoid(hg[:, w:2 * w])
    g = hg[:, 3 * w:4 * w]
    hg_ref[:, 0:w] = hg[:, 0:w]
    hg_ref[:, w:2 * w] = jnp.log(f)
    hg_ref[:, 2 * w:3 * w] = f
    hg_ref[:, 3 * w:4 * w] = hg[:, 2 * w:3 * w]
    hg_ref[:, 4 * w:5 * w] = g * (0.5 * (jnp.tanh(0.5 * g) + 1.0))
    u_ref[...] = _dot(h, wu_ref[...]).reshape(u_ref.shape)


def _proj(x, gain, wqk, wvt, wf, wh, wu, qg, kg, fb, lb):
    b, l, d = x.shape
    tm = min(PROJ_ROWS, l)
    tk = min(ATTN_BLOCK, l)
    hw = 5 * HGRN_WIDTH
    full = lambda shape: pl.BlockSpec(shape, lambda bi, i: tuple(0 for _ in shape))
    est = 2 * (d * (512 + 256 + 128 + hw + 256) * 2) + 2 * tm * d * 4 + 2 * tm * (hw + 256 + 512) * 4 + 2 * tm * hw * 4
    return pl.pallas_call(
        _proj_kernel,
        grid=(b, l // tm),
        in_specs=[
            pl.BlockSpec((None, tm, d), lambda bi, i: (bi, i, 0)),
            full((1, d)), full(wqk.shape), full(wvt.shape), full(wf.shape), full(wh.shape),
            full(wu.shape), full((1, FOX_WIDTH)), full((1, FOX_WIDTH)), full((1, 128)),
            full((1, HGRN_WIDTH)),
        ],
        out_specs=[
            pl.BlockSpec((None, tm, FOX_WIDTH), lambda bi, i: (bi, i, 0)),
            pl.BlockSpec((None, tm, FOX_WIDTH), lambda bi, i: (bi, i, 0)),
            pl.BlockSpec((None, tm // tk, FOX_WIDTH, tk), lambda bi, i: (bi, i, 0, 0)),
            pl.BlockSpec((None, tm, 128), lambda bi, i: (bi, i, 0)),
            pl.BlockSpec((None, tm, hw), lambda bi, i: (bi, i, 0)),
            pl.BlockSpec((tm // S5_CHUNK, None, S5_CHUNK, S5_WIDTH), lambda bi, i: (i, bi, 0, 0)),
        ],
        out_shape=[
            jax.ShapeDtypeStruct((b, l, FOX_WIDTH), BF16),
            jax.ShapeDtypeStruct((b, l, FOX_WIDTH), BF16),
            jax.ShapeDtypeStruct((b, l // tk, FOX_WIDTH, tk), BF16),
            jax.ShapeDtypeStruct((b, l, 128), BF16),
            jax.ShapeDtypeStruct((b, l, hw), F32),
            jax.ShapeDtypeStruct((l // S5_CHUNK, b, S5_CHUNK, S5_WIDTH), F32),
        ],
        scratch_shapes=[pltpu.VMEM((V7X_SUBLANES, 128), F32)],
        compiler_params=pltpu.CompilerParams(
            dimension_semantics=("arbitrary", "arbitrary"), vmem_limit_bytes=_vmem_limit(est)),
        name="proj",
    )(x, gain, wqk, wvt, wf, wh, wu, qg, kg, fb, lb)


def _attn_kernel(bounded_ref, q_ref, k_ref, vt_ref, cs_ref, e_ref, pq_ref, pk_ref, oq_ref, ok_ref,
                 o_ref, ka_ref, qa_ref, m_ref, l_ref, acc_ref, s_ref, p_ref):
    tq = o_ref.shape[0]
    i = pl.program_id(1)

    @pl.when(i == 0)
    def _():
        cs = cs_ref[...]
        ka_ref[...] = (_dot(k_ref[...], e_ref[...]) + _dot(cs, pk_ref[...]) + ok_ref[...]).astype(BF16)
        qa_ref[...] = (_dot(q_ref[...], e_ref[...]) + _dot(cs, pq_ref[...]) + oq_ref[...]).astype(BF16)

    l_ref[...] = jnp.zeros_like(l_ref)
    acc_ref[...] = jnp.zeros_like(acc_ref)
    nblk = ATTN_KV_BLOCKS
    tk = nblk * tq
    nfull = i // nblk
    qrows = pl.ds(pl.multiple_of(i * tq, tq), tq)
    key_minus_qry = (lax.broadcasted_iota(jnp.int32, (tk, tq), 0)
                     - lax.broadcasted_iota(jnp.int32, (tk, tq), 1))

    def causal(s, jj):
        return jnp.where(key_minus_qry <= (i - jj * nblk) * tq, s, -jnp.inf)

    def pv_dot(jj, hd, p):
        head_rows = slice(hd * FOX_HEAD_DIM, (hd + 1) * FOX_HEAD_DIM)
        pv = _dot(vt_ref[jj * nblk, head_rows, :], p[0:tq, :])
        for bk in range(1, nblk):
            pv = pv + _dot(vt_ref[jj * nblk + bk, head_rows, :], p[bk * tq:(bk + 1) * tq, :])
        return pv

    def bounded_steps(steps):
        for slot, jj in enumerate(steps):
            rows = pl.ds(pl.multiple_of(jj * tk, tk), tk)
            for hd in range(FOX_HEADS):
                slab = slice(hd * 128, (hd + 1) * 128)
                p = jnp.exp2(causal(_dot_nt(ka_ref[rows, slab], qa_ref[qrows, slab]), jj))
                l_ref[hd] = l_ref[hd] + jnp.sum(p, axis=0, keepdims=True)
                p_ref[slot, hd] = p.astype(BF16)
        for hd in range(FOX_HEADS):
            pv = pv_dot(steps[0], hd, p_ref[0, hd])
            for slot, jj in enumerate(steps[1:], 1):
                pv = pv + pv_dot(jj, hd, p_ref[slot, hd])
            acc_ref[hd] = acc_ref[hd] + pv

    @pl.when(bounded_ref[0] == 1)
    def _():
        def pair(kk, carry):
            bounded_steps([2 * kk, 2 * kk + 1])
            return carry

        lax.fori_loop(0, (nfull + 1) // 2, pair, 0)

        @pl.when(nfull % 2 == 0)
        def _():
            bounded_steps([nfull])

    @pl.when(bounded_ref[0] == 0)
    def _():
        _attn_online(i, nfull, tk, qrows, causal, pv_dot, ka_ref, qa_ref, m_ref, l_ref, acc_ref,
                     s_ref, p_ref.at[0])

    outs = [acc_ref[hd] / l_ref[hd] for hd in range(FOX_HEADS)]
    o_ref[...] = jnp.concatenate(outs, axis=0).T.astype(o_ref.dtype)


def _attn_online(i, nfull, tk, qrows, causal, pv_dot, ka_ref, qa_ref, m_ref, l_ref, acc_ref, s_ref, p_ref):
    m_ref[...] = jnp.full(m_ref.shape, -jnp.inf, F32)

    def logits(jj, slot):
        rows = pl.ds(pl.multiple_of(jj * tk, tk), tk)
        for hd in range(FOX_HEADS):
            slab = slice(hd * 128, (hd + 1) * 128)
            s_ref[slot, hd] = _dot_nt(ka_ref[rows, slab], qa_ref[qrows, slab])

    def update(jj, slot, last):
        alphas = []
        for hd in range(FOX_HEADS):
            s = s_ref[slot, hd]
            if last:
                s = causal(s, jj)
            m_old = m_ref[hd]
            m_new = jnp.maximum(m_old, jnp.max(s, axis=0, keepdims=True))
            alpha = jnp.exp2(m_old - m_new)
            p = jnp.exp2(s - m_new)
            m_ref[hd] = m_new
            l_ref[hd] = alpha * l_ref[hd] + jnp.sum(p, axis=0, keepdims=True)
            p_ref[hd] = p.astype(BF16)
            alphas.append(alpha)
        for hd in range(FOX_HEADS):
            acc_ref[hd] = alphas[hd] * acc_ref[hd] + pv_dot(jj, hd, p_ref[hd])

    def pair(kk, carry):
        logits(2 * kk + 1, 1)
        update(2 * kk, 0, False)
        logits(2 * kk + 2, 0)
        update(2 * kk + 1, 1, False)
        return carry

    logits(0, 0)
    npair = nfull // 2
    lax.fori_loop(0, npair, pair, 0)

    @pl.when(nfull % 2 == 0)
    def _():
        update(2 * npair, 0, True)

    @pl.when(nfull % 2 == 1)
    def _():
        logits(2 * npair + 1, 1)
        update(2 * npair, 0, False)
        update(2 * npair + 1, 1, True)


def _attn_placement():
    w = FOX_HEADS * 128
    e = np.zeros((FOX_WIDTH, w), np.float32)
    cols = np.arange(FOX_WIDTH)
    e[cols, (cols // FOX_HEAD_DIM) * 128 + cols % FOX_HEAD_DIM] = 1.0
    pq, pk = np.zeros((128, w), np.float32), np.zeros((128, w), np.float32)
    oq, ok = np.zeros((1, w), np.float32), np.zeros((1, w), np.float32)
    for hd in range(FOX_HEADS):
        base = hd * 128 + FOX_HEAD_DIM
        for term in range(3):
            pk[term * FOX_HEADS + hd, base + term] = -1.0
            oq[0, base + term] = 1.0
            pq[term * FOX_HEADS + hd, base + 3 + term] = 1.0
            ok[0, base + 3 + term] = 1.0
    return (jnp.asarray(e, BF16), jnp.asarray(pq, BF16), jnp.asarray(pk, BF16),
            jnp.asarray(oq), jnp.asarray(ok))


def _attn(q, k, vt, cs, q_gain, k_gain):
    b, l, w = q.shape
    nkv, tk = vt.shape[1], vt.shape[3]
    tq = tk
    wa = FOX_HEADS * 128
    e, pq, pk, oq, ok = _attn_placement()
    full = lambda shape: pl.BlockSpec(shape, lambda bi, i: tuple(0 for _ in shape))
    seq = lambda width: pl.BlockSpec((None, l, width), lambda bi, i: (bi, 0, 0))
    assert (l // tq) % ATTN_KV_BLOCKS == 0
    logit_bound = (1.01 * FOX_HEAD_DIM * LOG2E / math.sqrt(FOX_HEAD_DIM)
                   * jnp.max(jnp.abs(q_gain.astype(F32))) * jnp.max(jnp.abs(k_gain.astype(F32))))
    bounded = (logit_bound < ATTN_MAX_FREE_LOGIT).astype(jnp.int32).reshape(1)
    est = (2 * (l * w * 2 * 2 + l * 128 * 2) + 2 * l * wa * (2 + 4 + 4)
           + FOX_HEADS * ATTN_KV_BLOCKS * tq * tq * (2 * 4 + 2 * 2 + 2 * 4) + FOX_HEADS * 128 * tq * 4)
    return pl.pallas_call(
        _attn_kernel,
        grid=(b, l // tq),
        in_specs=[
            pl.BlockSpec(memory_space=pltpu.SMEM),
            seq(w), seq(w),
            pl.BlockSpec((None, nkv, w, tk), lambda bi, i: (bi, 0, 0, 0)),
            seq(128),
            full(e.shape), full(pq.shape), full(pk.shape), full(oq.shape), full(ok.shape),
        ],
        out_specs=pl.BlockSpec((None, tq, w), lambda bi, i: (bi, i, 0)),
        out_shape=jax.ShapeDtypeStruct((b, l, w), BF16),
        scratch_shapes=[pltpu.VMEM((l, wa), BF16), pltpu.VMEM((l, wa), BF16),
                        pltpu.VMEM((FOX_HEADS, 1, tq), F32), pltpu.VMEM((FOX_HEADS, 1, tq), F32),
                        pltpu.VMEM((FOX_HEADS, FOX_HEAD_DIM, tq), F32),
                        pltpu.VMEM((2, FOX_HEADS, ATTN_KV_BLOCKS * tq, tq), F32),
                        pltpu.VMEM((2, FOX_HEADS, ATTN_KV_BLOCKS * tq, tq), BF16)],
        compiler_params=pltpu.CompilerParams(
            dimension_semantics=("arbitrary", "arbitrary"), vmem_limit_bytes=_vmem_limit(est)),
        name="fox_attn",
    )(bounded, q, k, vt, cs, e, pq, pk, oq, ok)


def _hgrn_levels(c):
    return [c >> (s + 1) for s in range(int(math.log2(c)))]


def _hgrn_level_ids(c):
    t = np.arange(c)[:, None]
    s = np.arange(c)[None, :]
    lvl = np.floor(np.log2(np.maximum(t ^ s, 1))).astype(np.int32) + 1
    return jnp.asarray(np.where(s > t, -1, np.where(s == t, 0, lvl)).astype(np.int32))


def _hgrn_kernel(q_ref, lf_ref, f_ref, i_ref, gate_ref, og_ref, lvl_ref, o_ref, st_ref, b_ref):
    c = q_ref.shape[0]

    @pl.when(pl.program_id(1) == 0)
    def _():
        st_ref[...] = jnp.zeros_like(st_ref)

    f = f_ref[...]
    key = 1.0 - f
    q = q_ref[...]
    ti = lax.broadcasted_iota(jnp.int32, (c, c), 0)
    si = lax.broadcasted_iota(jnp.int32, (c, c), 1)
    lower = (si <= ti).astype(BF16)
    b = _dot_exact_lhs(lower, lf_ref[...], terms=2)
    b_ref[...] = b
    row = lax.broadcasted_iota(jnp.int32, b.shape, 0)

    def ref_rows(m):
        parts = [jnp.broadcast_to(b_ref[g0 + m - 1:g0 + m, :], (2 * m, b.shape[1]))
                 for g0 in range(0, c, 2 * m)]
        return jnp.concatenate(parts, axis=0) if len(parts) > 1 else parts[0]

    def level_decay(m):
        if m == 1:
            return jnp.where(row % 2 == 1, f, 1.0)
        if m == 2:
            off = row % 4
            nxt, prv = pltpu.roll(f, c - 1, axis=0), pltpu.roll(f, 1, axis=0)
            return jnp.where(off == 0, nxt, jnp.where(off == 1, 1.0, jnp.where(off == 2, f, f * prv)))
        return jnp.exp2(jnp.abs(b - ref_rows(m)) * (-LOG2E))

    lvl = lvl_ref[...]
    qb, kb = q.astype(BF16), key.astype(BF16)
    scores = [jnp.zeros((c, c), F32) for _ in range(HGRN_HEADS)]
    for idx, m in enumerate([0] + _hgrn_levels(c)[::-1]):
        if m == 0:
            z, y = qb, kb
        else:
            e = level_decay(m).astype(BF16)
            z, y = qb * e, kb * e
        mask = lvl == idx
        for hd in range(HGRN_HEADS):
            sl = slice(hd * HGRN_DIM, (hd + 1) * HGRN_DIM)
            scores[hd] = jnp.where(mask, _dot_nt(z[:, sl], y[:, sl]), scores[hd])

    b_last = b[c - 1:c, :]
    q_in = (q * jnp.exp(b)).astype(BF16)
    k_out = (key * jnp.exp(b_last - b)).astype(BF16)
    v = i_ref[...].astype(BF16)
    gate = gate_ref[...]
    og = og_ref[...]
    decay = jnp.exp(b_last)
    for hd in range(HGRN_HEADS):
        sl = slice(hd * HGRN_DIM, (hd + 1) * HGRN_DIM)
        st = st_ref[hd]
        o = _dot(scores[hd].astype(BF16), v[:, sl]) + _dot_nt(q_in[:, sl], st.astype(BF16))
        st_ref[hd] = st * decay[:, sl] + _dot_tn(v[:, sl], k_out[:, sl])
        o_ref[:, sl] = (_rms_norm(o, og[:, sl]) * gate[:, sl]).astype(o_ref.dtype)


def _hgrn(hg, og):
    b, l, _ = hg.shape
    c = min(HGRN_CHUNK, l)
    w = HGRN_WIDTH
    part = lambda p: pl.BlockSpec((None, c, w), lambda bi, i, p=p: (bi, i, p))
    est = 2 * 6 * c * w * 4 + 16 * c * w * 4 + HGRN_HEADS * HGRN_DIM * HGRN_DIM * 4
    return pl.pallas_call(
        _hgrn_kernel,
        grid=(b, l // c),
        in_specs=[part(0), part(1), part(2), part(3), part(4),
                  pl.BlockSpec((1, w), lambda bi, i: (0, 0)),
                  pl.BlockSpec((c, c), lambda bi, i: (0, 0))],
        out_specs=pl.BlockSpec((None, c, w), lambda bi, i: (bi, i, 0)),
        out_shape=jax.ShapeDtypeStruct((b, l, w), BF16),
        scratch_shapes=[pltpu.VMEM((HGRN_HEADS, HGRN_DIM, HGRN_DIM), F32),
                        pltpu.VMEM((c, w), F32)],
        compiler_params=pltpu.CompilerParams(
            dimension_semantics=("arbitrary", "arbitrary"), vmem_limit_bytes=_vmem_limit(est)),
        name="hgrn2",
    )(hg, hg, hg, hg, hg, og, _hgrn_level_ids(c))


def _s5_kernel(u_ref, kd_ref, pin_ref, pout_ref, ar_ref, ai_ref, y_ref, xr_ref, xi_ref):
    nc, nb, cs, w = u_ref.shape
    ns = ar_ref.shape[1]

    @pl.when(pl.program_id(0) == 0)
    def _():
        xr_ref[...] = jnp.zeros_like(xr_ref)
        xi_ref[...] = jnp.zeros_like(xi_ref)

    us = [u_ref[:, :, s, :].reshape(nc * nb, w).astype(BF16) for s in range(cs)]
    win = _dot(us[0], pin_ref[0])
    for s in range(1, cs):
        win = win + _dot(us[s], pin_ref[s])

    ar, ai = ar_ref[...], ai_ref[...]
    xr, xi = xr_ref[...], xi_ref[...]
    prev_r, prev_i = [], []
    for c in range(nc):
        prev_r.append(xr)
        prev_i.append(xi)
        wr = win[c * nb:(c + 1) * nb, :ns]
        wi = win[c * nb:(c + 1) * nb, ns:]
        xr, xi = ar * xr - ai * xi + wr, ar * xi + ai * xr + wi
    xr_ref[...] = xr
    xi_ref[...] = xi
    xp = jnp.concatenate([jnp.concatenate(prev_r, axis=0), jnp.concatenate(prev_i, axis=0)],
                         axis=1).astype(BF16)

    for t in range(cs):
        lags = jnp.concatenate([us[t - d] for d in range(t + 1)], axis=1) if t else us[0]
        yt = _dot(xp, pout_ref[t]) + _dot(lags, kd_ref[0:(t + 1) * w, :])
        y_ref[:, :, t, :] = yt.reshape(nc, nb, w)


def _s5(u4, kd, pin, pout, ar, ai):
    nchunk, nb, cs, w = u4.shape
    nc = min(S5_TILE // cs, nchunk)
    const = dict(pipeline_mode=pl.Buffered(1))
    full = lambda shape: pl.BlockSpec(shape, lambda i: tuple(0 for _ in shape), **const)
    rows = nc * cs * nb
    chunked = pl.BlockSpec((nc, nb, cs, w), lambda i: (i, 0, 0, 0))
    est = (4 * rows * w * 4 + (kd.size + pin.size + pout.size) * 2 + rows // cs * 2 * S5_NSTATE * (4 + 4 + 2)
           + rows * w * (2 + 2 + 4))
    return pl.pallas_call(
        _s5_kernel,
        grid=(nchunk // nc,),
        in_specs=[chunked, full(kd.shape), full(pin.shape), full(pout.shape), full(ar.shape), full(ai.shape)],
        out_specs=chunked,
        out_shape=jax.ShapeDtypeStruct(u4.shape, F32),
        scratch_shapes=[pltpu.VMEM((nb, S5_NSTATE), F32), pltpu.VMEM((nb, S5_NSTATE), F32)],
        compiler_params=pltpu.CompilerParams(
            dimension_semantics=("arbitrary",), vmem_limit_bytes=_vmem_limit(est)),
        name="s5",
    )(u4, kd, pin, pout, ar, ai)


def _s5_operators(lam_re, lam_im, log_step, b_re, b_im, c_re, c_im):
    lr, li = lam_re.astype(F32), lam_im.astype(F32)
    dt = jnp.exp(log_step.astype(F32))[:, None]
    mag = jnp.exp(lr * dt)
    lbar_re, lbar_im = mag * jnp.cos(li * dt), mag * jnp.sin(li * dt)
    nr, ni = lbar_re - 1.0, lbar_im
    den = lr * lr + li * li
    coef_re = (nr * lr + ni * li) / den
    coef_im = (ni * lr - nr * li) / den
    br, bi = b_re.astype(F32), b_im.astype(F32)
    bbar_re = coef_re[..., None] * br - coef_im[..., None] * bi
    bbar_im = coef_re[..., None] * bi + coef_im[..., None] * br
    cr, ci = c_re.astype(F32), c_im.astype(F32)

    pr, pi = [jnp.ones_like(lbar_re)], [jnp.zeros_like(lbar_re)]
    for _ in range(S5_CHUNK):
        pr, pi = (pr + [pr[-1] * lbar_re - pi[-1] * lbar_im],
                  pi + [pr[-1] * lbar_im + pi[-1] * lbar_re])
    pw_re, pw_im = jnp.stack(pr), jnp.stack(pi)
    cs = S5_CHUNK

    def block_diag(compact, rows_group, cols_group, reps):
        tiled = jnp.tile(compact, reps)
        r = lax.broadcasted_iota(jnp.int32, tiled.shape, tiled.ndim - 2)
        c = lax.broadcasted_iota(jnp.int32, tiled.shape, tiled.ndim - 1)
        return jnp.where(rows_group(r) == cols_group(c), tiled, 0.0).astype(BF16)

    lane_group = lambda c: c // S5_GROUP
    state_group = lambda c: (c % S5_NSTATE) // S5_STATE

    ab_re = pw_re[:cs, :, :, None] * bbar_re - pw_im[:cs, :, :, None] * bbar_im
    ab_im = pw_re[:cs, :, :, None] * bbar_im + pw_im[:cs, :, :, None] * bbar_re
    taps = jnp.sum(cr[None, :, :, :, None] * ab_re[:, :, None] - ci[None, :, :, :, None] * ab_im[:, :, None],
                   axis=3)
    kd = block_diag(taps.transpose(0, 1, 3, 2).reshape(cs * S5_WIDTH, S5_GROUP),
                    lambda r: (r // S5_GROUP) % S5_GROUPS, lane_group, (1, S5_GROUPS))

    rev = cs - 1 - jnp.arange(cs)
    ab = jnp.stack([ab_re[rev], ab_im[rev]], axis=1)
    pin = block_diag(ab.transpose(0, 4, 1, 2, 3).reshape(cs, S5_GROUP, 2 * S5_NSTATE),
                     lane_group, state_group, (1, S5_GROUPS, 1))

    qr, qi = pw_re[1:], pw_im[1:]
    po_re = cr[None] * qr[:, :, None, :] - ci[None] * qi[:, :, None, :]
    po_im = -(cr[None] * qi[:, :, None, :] + ci[None] * qr[:, :, None, :])
    po = jnp.stack([po_re, po_im], axis=1)
    pout = block_diag(po.transpose(0, 1, 2, 4, 3).reshape(cs, 2 * S5_NSTATE, S5_GROUP),
                      state_group, lane_group, (1, 1, S5_GROUPS))
    a_re = pw_re[cs].reshape(1, S5_NSTATE)
    a_im = pw_im[cs].reshape(1, S5_NSTATE)
    return kd, pin, pout, a_re, a_im


def _out_ffn_kernel(x_ref, oa_ref, ob_ref, y_ref, u_ref, d_ref, gw_ref, gb_ref, wa_ref, wb_ref, wc_ref,
                    g_ref, wg_ref, wu_ref, wd_ref, o_ref):
    tm = x_ref.shape[0]
    y = y_ref[...].reshape(tm, S5_WIDTH) + d_ref[...] * u_ref[...].reshape(tm, S5_WIDTH)
    z = 0.5 * y * (1.0 + jnp.tanh(math.sqrt(2.0 / math.pi) * (y + 0.044715 * (y * y * y))))
    oc = z * _sigmoid(_dot(z.astype(BF16), gw_ref[...]) + gb_ref[...])
    x = (x_ref[...] + _dot(oa_ref[...], wa_ref[...]) + _dot(ob_ref[...], wb_ref[...])
         + _dot(oc.astype(BF16), wc_ref[...]))
    o_ref[...] = _swiglu_residual(x, g_ref[...], wg_ref, wu_ref, wd_ref)


def _out_ffn(x, oa, ob, y_lb, u_lb, dvec, gw, gb, wa, wb, wc, gain, wg, wu, wd, layer):
    b, l, d = x.shape
    f = wg.shape[2]
    stacked = lambda shape: pl.BlockSpec((None,) + shape[1:], lambda bi, i: (layer, 0, 0),
                                         pipeline_mode=pl.Buffered(1))
    tm = min(OUT_ROWS, l)
    rows = lambda w: pl.BlockSpec((None, tm, w), lambda bi, i: (bi, i, 0))
    token_major = pl.BlockSpec((tm // S5_CHUNK, None, S5_CHUNK, S5_WIDTH), lambda bi, i: (i, bi, 0, 0))
    full = lambda shape: pl.BlockSpec(shape, lambda bi, i: tuple(0 for _ in shape),
                                      pipeline_mode=pl.Buffered(1))
    est = (2 * (2 * tm * d * 4 + tm * (256 * 2 + 512 * 2 + 256 * 8)) + 2 * (d * d + 256 * 256) + 3 * d * f * 2
           + tm * f * (4 + 4 + 2 + 4) + 4 * tm * d * 4)
    return pl.pallas_call(
        _out_ffn_kernel,
        grid=(b, l // tm),
        in_specs=[rows(d), rows(FOX_WIDTH), rows(HGRN_WIDTH), token_major, token_major,
                  full((1, S5_WIDTH)), full(gw.shape), full((1, S5_WIDTH)),
                  full(wa.shape), full(wb.shape), full(wc.shape),
                  full((1, d)), stacked(wg.shape), stacked(wu.shape), stacked(wd.shape)],
        out_specs=rows(d),
        out_shape=jax.ShapeDtypeStruct((b, l, d), F32),
        compiler_params=pltpu.CompilerParams(
            dimension_semantics=("arbitrary", "arbitrary"), vmem_limit_bytes=_vmem_limit(est)),
        name="out_ffn",
    )(x, oa, ob, y_lb, u_lb, dvec, gw, gb, wa, wb, wc, gain, wg, wu, wd)


def kernel(x, ffn1_norm, ffn1_w_gate, ffn1_w_up, ffn1_w_down, mix_norm, w_in, fox_f_bias, fox_q_gain, fox_k_gain, hgrn_lb, hgrn_o_gain, s5_lambda_re, s5_lambda_im, s5_log_step, s5_B_re, s5_B_im, s5_C_re, s5_C_im, s5_D, s5_glu_w, s5_glu_b, w_out, ffn2_norm, ffn2_w_gate, ffn2_w_up, ffn2_w_down):
    bsz, seq, d = x.shape
    depth = w_in.shape[0]
    t = bsz * seq

    lb_all = jnp.cumsum(jax.nn.softmax(hgrn_lb.astype(F32), axis=0), axis=0)
    lb_all = lb_all - lb_all[0:1]

    o_q, o_k, o_v = 0, FOX_WIDTH, 2 * FOX_WIDTH
    o_f = 3 * FOX_WIDTH
    o_h = o_f + FOX_HEADS
    o_u = o_h + 4 * HGRN_WIDTH

    ffn1 = [_to_bf16(w) for w in (ffn1_w_gate, ffn1_w_up, ffn1_w_down)]
    ffn2 = [_to_bf16(w) for w in (ffn2_w_gate, ffn2_w_up, ffn2_w_down)]

    for l in range(depth):
        x = _ffn(x.reshape(t, d), ffn1_norm[l][None], *ffn1, layer=l).reshape(bsz, seq, d)

        w = w_in[l]
        wf = jnp.zeros((d, 128), F32).at[:, :3 * FOX_HEADS].set(jnp.tile(w[:, o_f:o_h], (1, 3))).astype(BF16)
        fb = jnp.zeros((1, 128), F32).at[0, :3 * FOX_HEADS].set(jnp.tile(fox_f_bias[l].astype(F32), 3))
        q, k, vt, cs, hg, u_lb = _proj(
            x, mix_norm[l][None], w[:, o_q:o_v].astype(BF16), w[:, o_v:o_f].T.astype(BF16), wf,
            w[:, o_h:o_u].astype(BF16), w[:, o_u:].astype(BF16),
            jnp.tile(fox_q_gain[l].astype(F32), FOX_HEADS)[None],
            jnp.tile(fox_k_gain[l].astype(F32), FOX_HEADS)[None], fb, lb_all[l][None])

        o_a = _attn(q, k, vt, cs, fox_q_gain[l], fox_k_gain[l])
        o_b = _hgrn(hg, hgrn_o_gain[l].astype(F32)[None])
        ops = _s5_operators(s5_lambda_re[l], s5_lambda_im[l], s5_log_step[l],
                            s5_B_re[l], s5_B_im[l], s5_C_re[l], s5_C_im[l])
        y_lb = _s5(u_lb, *ops)

        wo = w_out[l].astype(BF16)
        x = _out_ffn(x, o_a, o_b, y_lb, u_lb, s5_D[l].astype(F32)[None], s5_glu_w[l].astype(BF16),
                     s5_glu_b[l].astype(F32)[None], wo[:FOX_WIDTH], wo[FOX_WIDTH:FOX_WIDTH + HGRN_WIDTH],
                     wo[FOX_WIDTH + HGRN_WIDTH:], ffn2_norm[l][None], *ffn2, layer=l)
    return x
```

```python
import functools
import math

import jax
import jax.numpy as jnp
import numpy as np
from jax import lax
from jax.experimental import pallas as pl
from jax.experimental.pallas import tpu as pltpu

F32 = jnp.float32
BF16 = jnp.bfloat16

EPS = 1e-6
D_MODEL = 1024
D_FF = 2816
FOX_HEADS = 4
FOX_HEAD_DIM = 64
FOX_WIDTH = FOX_HEADS * FOX_HEAD_DIM
HGRN_HEADS = 4
HGRN_DIM = 128
HGRN_WIDTH = HGRN_HEADS * HGRN_DIM
S5_GROUPS = 16
S5_GROUP = 16
S5_STATE = 64
S5_WIDTH = S5_GROUPS * S5_GROUP
S5_NSTATE = S5_GROUPS * S5_STATE
S5_CHUNK = 8
S5_TILE = 128
CAST_ROWS = 256
LOG2E = math.log2(math.e)

V7X_VMEM_BYTES = 64 * 1024 * 1024
V7X_SUBLANES = 8

ATTN_BLOCK = 256
ATTN_KV_BLOCKS = 2
ATTN_MAX_FREE_LOGIT = 40.0
HGRN_CHUNK = 128
HGRN_CHUNKS_PER_STEP = 4
FFN_ROWS = 512
FFN_SPLIT = 2
PROJ_ROWS = 512
OUT_ROWS = 512


def _vmem_limit(estimate_bytes):
    return int(min(estimate_bytes * 5 // 4 + (4 << 20), V7X_VMEM_BYTES - (4 << 20)))


def _rms_norm(x, gain):
    ms = jnp.mean(x * x, axis=-1, keepdims=True)
    return x * lax.rsqrt(ms + EPS) * gain


def _dot(a, b):
    return jnp.dot(a, b, preferred_element_type=F32)


def _dot_nt(a, b):
    return lax.dot_general(a, b, (((1,), (1,)), ((), ())), preferred_element_type=F32)


def _dot_tn(a, b):
    return lax.dot_general(a, b, (((0,), (0,)), ((), ())), preferred_element_type=F32)


def _split3(a):
    a1 = a.astype(BF16)
    r1 = a - a1.astype(F32)
    a2 = r1.astype(BF16)
    a3 = (r1 - a2.astype(F32)).astype(BF16)
    return a1, a2, a3


def _dot_exact_rhs(a, rhs, terms=3):
    parts = _split3(a)[:terms]
    out = _dot(parts[0], rhs)
    for part in parts[1:]:
        out = out + _dot(part, rhs)
    return out


def _dot_exact_lhs(lhs, a, terms=3):
    parts = _split3(a)[:terms]
    out = _dot(lhs, parts[0])
    for part in parts[1:]:
        out = out + _dot(lhs, part)
    return out


def _sigmoid(x):
    return 1.0 / (1.0 + jnp.exp(-x))


def _cumsum_rows(a):
    row = lax.broadcasted_iota(jnp.int32, a.shape, 0)
    shift = 1
    while shift < a.shape[0]:
        a = a + jnp.where(row >= shift, pltpu.roll(a, shift, axis=0), 0.0)
        shift *= 2
    return a


def _cast_kernel(w_ref, o_ref):
    o_ref[...] = w_ref[...].astype(o_ref.dtype)


def _to_bf16(w):
    depth, rows, cols = w.shape
    tr = CAST_ROWS
    assert rows % tr == 0
    spec = pl.BlockSpec((None, tr, cols), lambda l, i: (l, i, 0))
    return pl.pallas_call(
        _cast_kernel,
        grid=(depth, rows // tr),
        in_specs=[spec],
        out_specs=spec,
        out_shape=jax.ShapeDtypeStruct(w.shape, BF16),
        compiler_params=pltpu.CompilerParams(
            dimension_semantics=("arbitrary", "arbitrary"),
            vmem_limit_bytes=_vmem_limit(2 * tr * cols * (4 + 2))),
        name="cast_bf16",
    )(w)


def _swiglu_residual(x, gain, wg_ref, wu_ref, wd_ref):
    h = _rms_norm(x, gain).astype(BF16)
    gate = _dot(h, wg_ref[...])
    up = _dot(h, wu_ref[...])
    act = (gate * _sigmoid(gate) * up).astype(BF16)
    return x + 0.5 * _dot(act, wd_ref[...])


def _ffn_kernel(x_ref, g_ref, wg_ref, wu_ref, wd_ref, o_ref):
    part_rows = x_ref.shape[0] // FFN_SPLIT
    for part in range(FFN_SPLIT):
        rows = slice(part * part_rows, (part + 1) * part_rows)
        o_ref[rows, :] = _swiglu_residual(x_ref[rows, :], g_ref[...], wg_ref, wu_ref, wd_ref)


def _ffn(x2d, gain, wg, wu, wd, layer):
    t, d = x2d.shape
    f = wg.shape[2]
    tm = min(FFN_ROWS, t)
    const = dict(pipeline_mode=pl.Buffered(1))
    est = 3 * d * f * 2 + 4 * tm * d * 4 + tm * f * (4 + 4 + 2 + 4) + tm * d * 4
    return pl.pallas_call(
        _ffn_kernel,
        grid=(t // tm,),
        in_specs=[
            pl.BlockSpec((tm, d), lambda i: (i, 0)),
            pl.BlockSpec((1, d), lambda i: (0, 0)),
            pl.BlockSpec((None, d, f), lambda i: (layer, 0, 0), **const),
            pl.BlockSpec((None, d, f), lambda i: (layer, 0, 0), **const),
            pl.BlockSpec((None, f, d), lambda i: (layer, 0, 0), **const),
        ],
        out_specs=pl.BlockSpec((tm, d), lambda i: (i, 0)),
        out_shape=jax.ShapeDtypeStruct((t, d), F32),
        compiler_params=pltpu.CompilerParams(
            dimension_semantics=("arbitrary",), vmem_limit_bytes=_vmem_limit(est)),
        name="ffn",
    )(x2d, gain, wg, wu, wd)


def _proj_kernel(x_ref, g_ref, wqk_ref, wvt_ref, wf_ref, wh_ref, wu_ref, qg_ref, kg_ref,
                 fb_ref, lb_ref, q_ref, k_ref, vt_ref, cs_ref, hg_ref, u_ref, carry_ref):
    tm = x_ref.shape[0]
    tk = vt_ref.shape[2]

    @pl.when(pl.program_id(1) == 0)
    def _():
        carry_ref[...] = jnp.zeros_like(carry_ref)

    h = _rms_norm(x_ref[...], g_ref[...]).astype(BF16)

    li = lax.broadcasted_iota(jnp.int32, (FOX_WIDTH, FOX_WIDTH), 0) // FOX_HEAD_DIM
    lj = lax.broadcasted_iota(jnp.int32, (FOX_WIDTH, FOX_WIDTH), 1) // FOX_HEAD_DIM
    group = (li == lj).astype(BF16)

    def head_norm(a, gain):
        ms = _dot_exact_rhs(a * a, group, terms=1) * (1.0 / FOX_HEAD_DIM)
        return a * lax.rsqrt(ms + EPS) * gain

    qk = _dot(h, wqk_ref[...])
    q_ref[...] = (head_norm(qk[:, :FOX_WIDTH], qg_ref[...]) * (LOG2E / math.sqrt(FOX_HEAD_DIM))).astype(BF16)
    k_ref[...] = head_norm(qk[:, FOX_WIDTH:], kg_ref[...]).astype(BF16)
    vt = _dot_nt(wvt_ref[...], h).astype(BF16)
    for blk in range(tm // tk):
        vt_ref[blk] = vt[:, blk * tk:(blk + 1) * tk]

    fl = _dot(h, wf_ref[...]) + fb_ref[...]
    ls = jnp.minimum(fl, 0.0) - jnp.log1p(jnp.exp(-jnp.abs(fl)))
    c = _cumsum_rows(ls) + carry_ref[0:1, :]
    carry_ref[...] = jnp.broadcast_to(c[tm - 1:tm, :], carry_ref.shape)
    c1, c2, c3 = _split3(c * LOG2E)
    lane = lax.broadcasted_iota(jnp.int32, c.shape, 1)
    cs_ref[...] = jnp.where(lane < FOX_HEADS, c1, jnp.where(lane < 2 * FOX_HEADS, c2, c3))

    hg = _dot(h, wh_ref[...])
    w = HGRN_WIDTH
    lb = lb_ref[...]
    f = lb + (1.0 - lb) * _sigmoid(hg[:, w:2 * w])
    g = hg[:, 3 * w:4 * w]
    hg_ref[:, 0:w] = hg[:, 0:w]
    hg_ref[:, w:2 * w] = jnp.log(f)
    hg_ref[:, 2 * w:3 * w] = f
    hg_ref[:, 3 * w:4 * w] = hg[:, 2 * w:3 * w]
    hg_ref[:, 4 * w:5 * w] = g * (0.5 * (jnp.tanh(0.5 * g) + 1.0))
    u_ref[...] = _dot(h, wu_ref[...]).reshape(u_ref.shape)


def _proj(x, gain, wqk, wvt, wf, wh, wu, qg, kg, fb, lb):
    b, l, d = x.shape
    tm = min(PROJ_ROWS, l)
    tk = min(ATTN_BLOCK, l)
    hw = 5 * HGRN_WIDTH
    full = lambda shape: pl.BlockSpec(shape, lambda bi, i: tuple(0 for _ in shape))
    est = 2 * (d * (512 + 256 + 128 + hw + 256) * 2) + 2 * tm * d * 4 + 2 * tm * (hw + 256 + 512) * 4 + 2 * tm * hw * 4
    return pl.pallas_call(
        _proj_kernel,
        grid=(b, l // tm),
        in_specs=[
            pl.BlockSpec((None, tm, d), lambda bi, i: (bi, i, 0)),
            full((1, d)), full(wqk.shape), full(wvt.shape), full(wf.shape), full(wh.shape),
            full(wu.shape), full((1, FOX_WIDTH)), full((1, FOX_WIDTH)), full((1, 128)),
            full((1, HGRN_WIDTH)),
        ],
        out_specs=[
            pl.BlockSpec((None, tm, FOX_WIDTH), lambda bi, i: (bi, i, 0)),
            pl.BlockSpec((None, tm, FOX_WIDTH), lambda bi, i: (bi, i, 0)),
            pl.BlockSpec((None, tm // tk, FOX_WIDTH, tk), lambda bi, i: (bi, i, 0, 0)),
            pl.BlockSpec((None, tm, 128), lambda bi, i: (bi, i, 0)),
            pl.BlockSpec((None, tm, hw), lambda bi, i: (bi, i, 0)),
            pl.BlockSpec((tm // S5_CHUNK, None, S5_CHUNK, S5_WIDTH), lambda bi, i: (i, bi, 0, 0)),
        ],
        out_shape=[
            jax.ShapeDtypeStruct((b, l, FOX_WIDTH), BF16),
            jax.ShapeDtypeStruct((b, l, FOX_WIDTH), BF16),
            jax.ShapeDtypeStruct((b, l // tk, FOX_WIDTH, tk), BF16),
            jax.ShapeDtypeStruct((b, l, 128), BF16),
            jax.ShapeDtypeStruct((b, l, hw), F32),
            jax.ShapeDtypeStruct((l // S5_CHUNK, b, S5_CHUNK, S5_WIDTH), F32),
        ],
        scratch_shapes=[pltpu.VMEM((V7X_SUBLANES, 128), F32)],
        compiler_params=pltpu.CompilerParams(
            dimension_semantics=("arbitrary", "arbitrary"), vmem_limit_bytes=_vmem_limit(est)),
        name="proj",
    )(x, gain, wqk, wvt, wf, wh, wu, qg, kg, fb, lb)


def _attn_kernel(bounded_ref, q_ref, k_ref, vt_ref, cs_ref, e_ref, pq_ref, pk_ref, oq_ref, ok_ref,
                 o_ref, ka_ref, qa_ref, m_ref, l_ref, acc_ref, s_ref, p_ref):
    tq = o_ref.shape[0]
    i = pl.program_id(1)

    @pl.when(i == 0)
    def _():
        cs = cs_ref[...]
        ka_ref[...] = (_dot(k_ref[...], e_ref[...]) + _dot(cs, pk_ref[...]) + ok_ref[...]).astype(BF16)
        qa_ref[...] = (_dot(q_ref[...], e_ref[...]) + _dot(cs, pq_ref[...]) + oq_ref[...]).astype(BF16)

    l_ref[...] = jnp.zeros_like(l_ref)
    acc_ref[...] = jnp.zeros_like(acc_ref)
    nblk = ATTN_KV_BLOCKS
    tk = nblk * tq
    nfull = i // nblk
    qrows = pl.ds(pl.multiple_of(i * tq, tq), tq)
    key_minus_qry = (lax.broadcasted_iota(jnp.int32, (tk, tq), 0)
                     - lax.broadcasted_iota(jnp.int32, (tk, tq), 1))

    def causal(s, jj):
        return jnp.where(key_minus_qry <= (i - jj * nblk) * tq, s, -jnp.inf)

    def pv_dot(jj, hd, p):
        head_rows = slice(hd * FOX_HEAD_DIM, (hd + 1) * FOX_HEAD_DIM)
        pv = _dot(vt_ref[jj * nblk, head_rows, :], p[0:tq, :])
        for bk in range(1, nblk):
            pv = pv + _dot(vt_ref[jj * nblk + bk, head_rows, :], p[bk * tq:(bk + 1) * tq, :])
        return pv

    def bounded_steps(steps):
        for slot, jj in enumerate(steps):
            rows = pl.ds(pl.multiple_of(jj * tk, tk), tk)
            for hd in range(FOX_HEADS):
                slab = slice(hd * 128, (hd + 1) * 128)
                p = jnp.exp2(causal(_dot_nt(ka_ref[rows, slab], qa_ref[qrows, slab]), jj))
                l_ref[hd] = l_ref[hd] + jnp.sum(p, axis=0, keepdims=True)
                p_ref[slot, hd] = p.astype(BF16)
        for hd in range(FOX_HEADS):
            pv = pv_dot(steps[0], hd, p_ref[0, hd])
            for slot, jj in enumerate(steps[1:], 1):
                pv = pv + pv_dot(jj, hd, p_ref[slot, hd])
            acc_ref[hd] = acc_ref[hd] + pv

    @pl.when(bounded_ref[0] == 1)
    def _():
        def pair(kk, carry):
            bounded_steps([2 * kk, 2 * kk + 1])
            return carry

        lax.fori_loop(0, (nfull + 1) // 2, pair, 0)

        @pl.when(nfull % 2 == 0)
        def _():
            bounded_steps([nfull])

    @pl.when(bounded_ref[0] == 0)
    def _():
        _attn_online(i, nfull, tk, qrows, causal, pv_dot, ka_ref, qa_ref, m_ref, l_ref, acc_ref,
                     s_ref, p_ref.at[0])

    outs = [acc_ref[hd] / l_ref[hd] for hd in range(FOX_HEADS)]
    o_ref[...] = jnp.concatenate(outs, axis=0).T.astype(o_ref.dtype)


def _attn_online(i, nfull, tk, qrows, causal, pv_dot, ka_ref, qa_ref, m_ref, l_ref, acc_ref, s_ref, p_ref):
    m_ref[...] = jnp.full(m_ref.shape, -jnp.inf, F32)

    def logits(jj, slot):
        rows = pl.ds(pl.multiple_of(jj * tk, tk), tk)
        for hd in range(FOX_HEADS):
            slab = slice(hd * 128, (hd + 1) * 128)
            s_ref[slot, hd] = _dot_nt(ka_ref[rows, slab], qa_ref[qrows, slab])

    def update(jj, slot, last):
        alphas = []
        for hd in range(FOX_HEADS):
            s = s_ref[slot, hd]
            if last:
                s = causal(s, jj)
            m_old = m_ref[hd]
            m_new = jnp.maximum(m_old, jnp.max(s, axis=0, keepdims=True))
            alpha = jnp.exp2(m_old - m_new)
            p = jnp.exp2(s - m_new)
            m_ref[hd] = m_new
            l_ref[hd] = alpha * l_ref[hd] + jnp.sum(p, axis=0, keepdims=True)
            p_ref[hd] = p.astype(BF16)
            alphas.append(alpha)
        for hd in range(FOX_HEADS):
            acc_ref[hd] = alphas[hd] * acc_ref[hd] + pv_dot(jj, hd, p_ref[hd])

    def pair(kk, carry):
        logits(2 * kk + 1, 1)
        update(2 * kk, 0, False)
        logits(2 * kk + 2, 0)
        update(2 * kk + 1, 1, False)
        return carry

    logits(0, 0)
    npair = nfull // 2
    lax.fori_loop(0, npair, pair, 0)

    @pl.when(nfull % 2 == 0)
    def _():
        update(2 * npair, 0, True)

    @pl.when(nfull % 2 == 1)
    def _():
        logits(2 * npair + 1, 1)
        update(2 * npair, 0, False)
        update(2 * npair + 1, 1, True)


def _attn_placement():
    w = FOX_HEADS * 128
    e = np.zeros((FOX_WIDTH, w), np.float32)
    cols = np.arange(FOX_WIDTH)
    e[cols, (cols // FOX_HEAD_DIM) * 128 + cols % FOX_HEAD_DIM] = 1.0
    pq, pk = np.zeros((128, w), np.float32), np.zeros((128, w), np.float32)
    oq, ok = np.zeros((1, w), np.float32), np.zeros((1, w), np.float32)
    for hd in range(FOX_HEADS):
        base = hd * 128 + FOX_HEAD_DIM
        for term in range(3):
            pk[term * FOX_HEADS + hd, base + term] = -1.0
            oq[0, base + term] = 1.0
            pq[term * FOX_HEADS + hd, base + 3 + term] = 1.0
            ok[0, base + 3 + term] = 1.0
    return (jnp.asarray(e, BF16), jnp.asarray(pq, BF16), jnp.asarray(pk, BF16),
            jnp.asarray(oq), jnp.asarray(ok))


def _attn(q, k, vt, cs, q_gain, k_gain):
    b, l, w = q.shape
    nkv, tk = vt.shape[1], vt.shape[3]
    tq = tk
    wa = FOX_HEADS * 128
    e, pq, pk, oq, ok = _attn_placement()
    full = lambda shape: pl.BlockSpec(shape, lambda bi, i: tuple(0 for _ in shape))
    seq = lambda width: pl.BlockSpec((None, l, width), lambda bi, i: (bi, 0, 0))
    assert (l // tq) % ATTN_KV_BLOCKS == 0
    logit_bound = (1.01 * FOX_HEAD_DIM * LOG2E / math.sqrt(FOX_HEAD_DIM)
                   * jnp.max(jnp.abs(q_gain.astype(F32))) * jnp.max(jnp.abs(k_gain.astype(F32))))
    bounded = (logit_bound < ATTN_MAX_FREE_LOGIT).astype(jnp.int32).reshape(1)
    est = (2 * (l * w * 2 * 2 + l * 128 * 2) + 2 * l * wa * (2 + 4 + 4)
           + FOX_HEADS * ATTN_KV_BLOCKS * tq * tq * (2 * 4 + 2 * 2 + 2 * 4) + FOX_HEADS * 128 * tq * 4)
    return pl.pallas_call(
        _attn_kernel,
        grid=(b, l // tq),
        in_specs=[
            pl.BlockSpec(memory_space=pltpu.SMEM),
            seq(w), seq(w),
            pl.BlockSpec((None, nkv, w, tk), lambda bi, i: (bi, 0, 0, 0)),
            seq(128),
            full(e.shape), full(pq.shape), full(pk.shape), full(oq.shape), full(ok.shape),
        ],
        out_specs=pl.BlockSpec((None, tq, w), lambda bi, i: (bi, i, 0)),
        out_shape=jax.ShapeDtypeStruct((b, l, w), BF16),
        scratch_shapes=[pltpu.VMEM((l, wa), BF16), pltpu.VMEM((l, wa), BF16),
                        pltpu.VMEM((FOX_HEADS, 1, tq), F32), pltpu.VMEM((FOX_HEADS, 1, tq), F32),
                        pltpu.VMEM((FOX_HEADS, FOX_HEAD_DIM, tq), F32),
                        pltpu.VMEM((2, FOX_HEADS, ATTN_KV_BLOCKS * tq, tq), F32),
                        pltpu.VMEM((2, FOX_HEADS, ATTN_KV_BLOCKS * tq, tq), BF16)],
        compiler_params=pltpu.CompilerParams(
            dimension_semantics=("arbitrary", "arbitrary"), vmem_limit_bytes=_vmem_limit(est)),
        name="fox_attn",
    )(bounded, q, k, vt, cs, e, pq, pk, oq, ok)


def _hgrn_levels(c):
    return [c >> (s + 1) for s in range(int(math.log2(c)))]


def _hgrn_level_ids(c):
    t = np.arange(c)[:, None]
    s = np.arange(c)[None, :]
    lvl = np.floor(np.log2(np.maximum(t ^ s, 1))).astype(np.int32) + 1
    return jnp.asarray(np.where(s > t, -1, np.where(s == t, 0, lvl)).astype(np.int32))


def _hgrn_kernel(q_ref, lf_ref, f_ref, i_ref, gate_ref, og_ref, lvl_ref, o_ref, st_ref, b_ref):
    @pl.when(pl.program_id(1) == 0)
    def _():
        st_ref[...] = jnp.zeros_like(st_ref)

    for sub in range(q_ref.shape[0] // HGRN_CHUNK):
        rows = pl.ds(sub * HGRN_CHUNK, HGRN_CHUNK)
        _hgrn_chunk(q_ref.at[rows], lf_ref.at[rows], f_ref.at[rows], i_ref.at[rows], gate_ref.at[rows],
                    og_ref, lvl_ref, o_ref.at[rows], st_ref, b_ref.at[rows])


def _hgrn_chunk(q_ref, lf_ref, f_ref, i_ref, gate_ref, og_ref, lvl_ref, o_ref, st_ref, b_ref):
    c = q_ref.shape[0]
    f = f_ref[...]
    key = 1.0 - f
    q = q_ref[...]
    ti = lax.broadcasted_iota(jnp.int32, (c, c), 0)
    si = lax.broadcasted_iota(jnp.int32, (c, c), 1)
    lower = (si <= ti).astype(BF16)
    b = _dot_exact_lhs(lower, lf_ref[...], terms=2)
    b_ref[...] = b
    row = lax.broadcasted_iota(jnp.int32, b.shape, 0)

    def ref_rows(m):
        parts = [jnp.broadcast_to(b_ref[g0 + m - 1:g0 + m, :], (2 * m, b.shape[1]))
                 for g0 in range(0, c, 2 * m)]
        return jnp.concatenate(parts, axis=0) if len(parts) > 1 else parts[0]

    def level_decay(m):
        if m == 1:
            return jnp.where(row % 2 == 1, f, 1.0)
        if m == 2:
            off = row % 4
            nxt, prv = pltpu.roll(f, c - 1, axis=0), pltpu.roll(f, 1, axis=0)
            return jnp.where(off == 0, nxt, jnp.where(off == 1, 1.0, jnp.where(off == 2, f, f * prv)))
        return jnp.exp2(jnp.abs(b - ref_rows(m)) * (-LOG2E))

    lvl = lvl_ref[...]
    qb, kb = q.astype(BF16), key.astype(BF16)
    scores = [jnp.zeros((c, c), F32) for _ in range(HGRN_HEADS)]
    for idx, m in enumerate([0] + _hgrn_levels(c)[::-1]):
        if m == 0:
            z, y = qb, kb
        else:
            e = level_decay(m).astype(BF16)
            z, y = qb * e, kb * e
        mask = lvl == idx
        for hd in range(HGRN_HEADS):
            sl = slice(hd * HGRN_DIM, (hd + 1) * HGRN_DIM)
            scores[hd] = jnp.where(mask, _dot_nt(z[:, sl], y[:, sl]), scores[hd])

    b_last = b[c - 1:c, :]
    q_in = (q * jnp.exp(b)).astype(BF16)
    k_out = (key * jnp.exp(b_last - b)).astype(BF16)
    v = i_ref[...].astype(BF16)
    gate = gate_ref[...]
    og = og_ref[...]
    decay = jnp.exp(b_last)
    for hd in range(HGRN_HEADS):
        sl = slice(hd * HGRN_DIM, (hd + 1) * HGRN_DIM)
        st = st_ref[hd]
        o = _dot(scores[hd].astype(BF16), v[:, sl]) + _dot_nt(q_in[:, sl], st.astype(BF16))
        st_ref[hd] = st * decay[:, sl] + _dot_tn(v[:, sl], k_out[:, sl])
        o_ref[:, sl] = (_rms_norm(o, og[:, sl]) * gate[:, sl]).astype(o_ref.dtype)


def _hgrn(hg, og):
    b, l, _ = hg.shape
    c = HGRN_CHUNK
    tl = HGRN_CHUNK * HGRN_CHUNKS_PER_STEP
    assert l % tl == 0
    w = HGRN_WIDTH
    part = lambda p: pl.BlockSpec((None, tl, w), lambda bi, i, p=p: (bi, i, p))
    est = 2 * 6 * tl * w * 4 + 16 * tl * w * 4 + HGRN_HEADS * HGRN_DIM * HGRN_DIM * 4
    return pl.pallas_call(
        _hgrn_kernel,
        grid=(b, l // tl),
        in_specs=[part(0), part(1), part(2), part(3), part(4),
                  pl.BlockSpec((1, w), lambda bi, i: (0, 0)),
                  pl.BlockSpec((c, c), lambda bi, i: (0, 0))],
        out_specs=pl.BlockSpec((None, tl, w), lambda bi, i: (bi, i, 0)),
        out_shape=jax.ShapeDtypeStruct((b, l, w), BF16),
        scratch_shapes=[pltpu.VMEM((HGRN_HEADS, HGRN_DIM, HGRN_DIM), F32),
                        pltpu.VMEM((tl, w), F32)],
        compiler_params=pltpu.CompilerParams(
            dimension_semantics=("arbitrary", "arbitrary"), vmem_limit_bytes=_vmem_limit(est)),
        name="hgrn2",
    )(hg, hg, hg, hg, hg, og, _hgrn_level_ids(c))


def _s5_kernel(u_ref, kd_ref, pin_ref, pout_ref, ar_ref, ai_ref, y_ref, xr_ref, xi_ref):
    nc, nb, cs, w = u_ref.shape
    ns = ar_ref.shape[1]

    @pl.when(pl.program_id(0) == 0)
    def _():
        xr_ref[...] = jnp.zeros_like(xr_ref)
        xi_ref[...] = jnp.zeros_like(xi_ref)

    us = [u_ref[:, :, s, :].reshape(nc * nb, w).astype(BF16) for s in range(cs)]
    win = _dot(us[0], pin_ref[0])
    for s in range(1, cs):
        win = win + _dot(us[s], pin_ref[s])

    ar, ai = ar_ref[...], ai_ref[...]
    xr, xi = xr_ref[...], xi_ref[...]
    prev_r, prev_i = [], []
    for c in range(nc):
        prev_r.append(xr)
        prev_i.append(xi)
        wr = win[c * nb:(c + 1) * nb, :ns]
        wi = win[c * nb:(c + 1) * nb, ns:]
        xr, xi = ar * xr - ai * xi + wr, ar * xi + ai * xr + wi
    xr_ref[...] = xr
    xi_ref[...] = xi
    xp = jnp.concatenate([jnp.concatenate(prev_r, axis=0), jnp.concatenate(prev_i, axis=0)],
                         axis=1).astype(BF16)

    for t in range(cs):
        lags = jnp.concatenate([us[t - d] for d in range(t + 1)], axis=1) if t else us[0]
        yt = _dot(xp, pout_ref[t]) + _dot(lags, kd_ref[0:(t + 1) * w, :])
        y_ref[:, :, t, :] = yt.reshape(nc, nb, w)


def _s5(u4, kd, pin, pout, ar, ai):
    nchunk, nb, cs, w = u4.shape
    nc = min(S5_TILE // cs, nchunk)
    const = dict(pipeline_mode=pl.Buffered(1))
    full = lambda shape: pl.BlockSpec(shape, lambda i: tuple(0 for _ in shape), **const)
    rows = nc * cs * nb
    chunked = pl.BlockSpec((nc, nb, cs, w), lambda i: (i, 0, 0, 0))
    est = (4 * rows * w * 4 + (kd.size + pin.size + pout.size) * 2 + rows // cs * 2 * S5_NSTATE * (4 + 4 + 2)
           + rows * w * (2 + 2 + 4))
    return pl.pallas_call(
        _s5_kernel,
        grid=(nchunk // nc,),
        in_specs=[chunked, full(kd.shape), full(pin.shape), full(pout.shape), full(ar.shape), full(ai.shape)],
        out_specs=chunked,
        out_shape=jax.ShapeDtypeStruct(u4.shape, F32),
        scratch_shapes=[pltpu.VMEM((nb, S5_NSTATE), F32), pltpu.VMEM((nb, S5_NSTATE), F32)],
        compiler_params=pltpu.CompilerParams(
            dimension_semantics=("arbitrary",), vmem_limit_bytes=_vmem_limit(est)),
        name="s5",
    )(u4, kd, pin, pout, ar, ai)


def _s5_operators(lam_re, lam_im, log_step, b_re, b_im, c_re, c_im):
    lr, li = lam_re.astype(F32), lam_im.astype(F32)
    dt = jnp.exp(log_step.astype(F32))[:, None]
    mag = jnp.exp(lr * dt)
    lbar_re, lbar_im = mag * jnp.cos(li * dt), mag * jnp.sin(li * dt)
    nr, ni = lbar_re - 1.0, lbar_im
    den = lr * lr + li * li
    coef_re = (nr * lr + ni * li) / den
    coef_im = (ni * lr - nr * li) / den
    br, bi = b_re.astype(F32), b_im.astype(F32)
    bbar_re = coef_re[..., None] * br - coef_im[..., None] * bi
    bbar_im = coef_re[..., None] * bi + coef_im[..., None] * br
    cr, ci = c_re.astype(F32), c_im.astype(F32)

    pr, pi = [jnp.ones_like(lbar_re)], [jnp.zeros_like(lbar_re)]
    for _ in range(S5_CHUNK):
        pr, pi = (pr + [pr[-1] * lbar_re - pi[-1] * lbar_im],
                  pi + [pr[-1] * lbar_im + pi[-1] * lbar_re])
    pw_re, pw_im = jnp.stack(pr), jnp.stack(pi)
    cs = S5_CHUNK

    def block_diag(compact, rows_group, cols_group, reps):
        tiled = jnp.tile(compact, reps)
        r = lax.broadcasted_iota(jnp.int32, tiled.shape, tiled.ndim - 2)
        c = lax.broadcasted_iota(jnp.int32, tiled.shape, tiled.ndim - 1)
        return jnp.where(rows_group(r) == cols_group(c), tiled, 0.0).astype(BF16)

    lane_group = lambda c: c // S5_GROUP
    state_group = lambda c: (c % S5_NSTATE) // S5_STATE

    ab_re = pw_re[:cs, :, :, None] * bbar_re - pw_im[:cs, :, :, None] * bbar_im
    ab_im = pw_re[:cs, :, :, None] * bbar_im + pw_im[:cs, :, :, None] * bbar_re
    taps = jnp.sum(cr[None, :, :, :, None] * ab_re[:, :, None] - ci[None, :, :, :, None] * ab_im[:, :, None],
                   axis=3)
    kd = block_diag(taps.transpose(0, 1, 3, 2).reshape(cs * S5_WIDTH, S5_GROUP),
                    lambda r: (r // S5_GROUP) % S5_GROUPS, lane_group, (1, S5_GROUPS))

    rev = cs - 1 - jnp.arange(cs)
    ab = jnp.stack([ab_re[rev], ab_im[rev]], axis=1)
    pin = block_diag(ab.transpose(0, 4, 1, 2, 3).reshape(cs, S5_GROUP, 2 * S5_NSTATE),
                     lane_group, state_group, (1, S5_GROUPS, 1))

    qr, qi = pw_re[1:], pw_im[1:]
    po_re = cr[None] * qr[:, :, None, :] - ci[None] * qi[:, :, None, :]
    po_im = -(cr[None] * qi[:, :, None, :] + ci[None] * qr[:, :, None, :])
    po = jnp.stack([po_re, po_im], axis=1)
    pout = block_diag(po.transpose(0, 1, 2, 4, 3).reshape(cs, 2 * S5_NSTATE, S5_GROUP),
                      state_group, lane_group, (1, 1, S5_GROUPS))
    a_re = pw_re[cs].reshape(1, S5_NSTATE)
    a_im = pw_im[cs].reshape(1, S5_NSTATE)
    return kd, pin, pout, a_re, a_im


def _out_ffn_kernel(x_ref, oa_ref, ob_ref, y_ref, u_ref, d_ref, gw_ref, gb_ref, wa_ref, wb_ref, wc_ref,
                    g_ref, wg_ref, wu_ref, wd_ref, o_ref):
    tm = x_ref.shape[0]
    y = y_ref[...].reshape(tm, S5_WIDTH) + d_ref[...] * u_ref[...].reshape(tm, S5_WIDTH)
    z = 0.5 * y * (1.0 + jnp.tanh(math.sqrt(2.0 / math.pi) * (y + 0.044715 * (y * y * y))))
    oc = z * _sigmoid(_dot(z.astype(BF16), gw_ref[...]) + gb_ref[...])
    x = (x_ref[...] + _dot(oa_ref[...], wa_ref[...]) + _dot(ob_ref[...], wb_ref[...])
         + _dot(oc.astype(BF16), wc_ref[...]))
    o_ref[...] = _swiglu_residual(x, g_ref[...], wg_ref, wu_ref, wd_ref)


def _out_ffn(x, oa, ob, y_lb, u_lb, dvec, gw, gb, wa, wb, wc, gain, wg, wu, wd, layer):
    b, l, d = x.shape
    f = wg.shape[2]
    stacked = lambda shape: pl.BlockSpec((None,) + shape[1:], lambda bi, i: (layer, 0, 0),
                                         pipeline_mode=pl.Buffered(1))
    tm = min(OUT_ROWS, l)
    rows = lambda w: pl.BlockSpec((None, tm, w), lambda bi, i: (bi, i, 0))
    token_major = pl.BlockSpec((tm // S5_CHUNK, None, S5_CHUNK, S5_WIDTH), lambda bi, i: (i, bi, 0, 0))
    full = lambda shape: pl.BlockSpec(shape, lambda bi, i: tuple(0 for _ in shape),
                                      pipeline_mode=pl.Buffered(1))
    est = (2 * (2 * tm * d * 4 + tm * (256 * 2 + 512 * 2 + 256 * 8)) + 2 * (d * d + 256 * 256) + 3 * d * f * 2
           + tm * f * (4 + 4 + 2 + 4) + 4 * tm * d * 4)
    return pl.pallas_call(
        _out_ffn_kernel,
        grid=(b, l // tm),
        in_specs=[rows(d), rows(FOX_WIDTH), rows(HGRN_WIDTH), token_major, token_major,
                  full((1, S5_WIDTH)), full(gw.shape), full((1, S5_WIDTH)),
                  full(wa.shape), full(wb.shape), full(wc.shape),
                  full((1, d)), stacked(wg.shape), stacked(wu.shape), stacked(wd.shape)],
        out_specs=rows(d),
        out_shape=jax.ShapeDtypeStruct((b, l, d), F32),
        compiler_params=pltpu.CompilerParams(
            dimension_semantics=("arbitrary", "arbitrary"), vmem_limit_bytes=_vmem_limit(est)),
        name="out_ffn",
    )(x, oa, ob, y_lb, u_lb, dvec, gw, gb, wa, wb, wc, gain, wg, wu, wd)


def kernel(x, ffn1_norm, ffn1_w_gate, ffn1_w_up, ffn1_w_down, mix_norm, w_in, fox_f_bias, fox_q_gain, fox_k_gain, hgrn_lb, hgrn_o_gain, s5_lambda_re, s5_lambda_im, s5_log_step, s5_B_re, s5_B_im, s5_C_re, s5_C_im, s5_D, s5_glu_w, s5_glu_b, w_out, ffn2_norm, ffn2_w_gate, ffn2_w_up, ffn2_w_down):
    bsz, seq, d = x.shape
    depth = w_in.shape[0]
    t = bsz * seq

    lb_all = jnp.cumsum(jax.nn.softmax(hgrn_lb.astype(F32), axis=0), axis=0)
    lb_all = lb_all - lb_all[0:1]

    o_q, o_k, o_v = 0, FOX_WIDTH, 2 * FOX_WIDTH
    o_f = 3 * FOX_WIDTH
    o_h = o_f + FOX_HEADS
    o_u = o_h + 4 * HGRN_WIDTH

    ffn1 = [_to_bf16(w) for w in (ffn1_w_gate, ffn1_w_up, ffn1_w_down)]
    ffn2 = [_to_bf16(w) for w in (ffn2_w_gate, ffn2_w_up, ffn2_w_down)]

    for l in range(depth):
        x = _ffn(x.reshape(t, d), ffn1_norm[l][None], *ffn1, layer=l).reshape(bsz, seq, d)

        w = w_in[l]
        wf = jnp.zeros((d, 128), F32).at[:, :3 * FOX_HEADS].set(jnp.tile(w[:, o_f:o_h], (1, 3))).astype(BF16)
        fb = jnp.zeros((1, 128), F32).at[0, :3 * FOX_HEADS].set(jnp.tile(fox_f_bias[l].astype(F32), 3))
        q, k, vt, cs, hg, u_lb = _proj(
            x, mix_norm[l][None], w[:, o_q:o_v].astype(BF16), w[:, o_v:o_f].T.astype(BF16), wf,
            w[:, o_h:o_u].astype(BF16), w[:, o_u:].astype(BF16),
            jnp.tile(fox_q_gain[l].astype(F32), FOX_HEADS)[None],
            jnp.tile(fox_k_gain[l].astype(F32), FOX_HEADS)[None], fb, lb_all[l][None])

        o_a = _attn(q, k, vt, cs, fox_q_gain[l], fox_k_gain[l])
        o_b = _hgrn(hg, hgrn_o_gain[l].astype(F32)[None])
        ops = _s5_operators(s5_lambda_re[l], s5_lambda_im[l], s5_log_step[l],
                            s5_B_re[l], s5_B_im[l], s5_C_re[l], s5_C_im[l])
        y_lb = _s5(u_lb, *ops)

        wo = w_out[l].astype(BF16)
        x = _out_ffn(x, o_a, o_b, y_lb, u_lb, s5_D[l].astype(F32)[None], s5_glu_w[l].astype(BF16),
                     s5_glu_b[l].astype(F32)[None], wo[:FOX_WIDTH], wo[FOX_WIDTH:FOX_WIDTH + HGRN_WIDTH],
                     wo[FOX_WIDTH + HGRN_WIDTH:], ffn2_norm[l][None], *ffn2, layer=l)
    return x
```

```python
import functools
import math

import jax
import jax.numpy as jnp
import numpy as np
from jax import lax
from jax.experimental import pallas as pl
from jax.experimental.pallas import tpu as pltpu

F32 = jnp.float32
BF16 = jnp.bfloat16

EPS = 1e-6
D_MODEL = 1024
D_FF = 2816
FOX_HEADS = 4
FOX_HEAD_DIM = 64
FOX_WIDTH = FOX_HEADS * FOX_HEAD_DIM
HGRN_HEADS = 4
HGRN_DIM = 128
HGRN_WIDTH = HGRN_HEADS * HGRN_DIM
S5_GROUPS = 16
S5_GROUP = 16
S5_STATE = 64
S5_WIDTH = S5_GROUPS * S5_GROUP
S5_NSTATE = S5_GROUPS * S5_STATE
S5_CHUNK = 8
S5_TILE = 128
CAST_ROWS = 256
LOG2E = math.log2(math.e)

V7X_VMEM_BYTES = 64 * 1024 * 1024
V7X_SUBLANES = 8

ATTN_BLOCK = 512
ATTN_KV_BLOCKS = 1
ATTN_MAX_FREE_LOGIT = 40.0
HGRN_CHUNK = 128
HGRN_CHUNKS_PER_STEP = 4
FFN_ROWS = 512
FFN_SPLIT = 4
PROJ_ROWS = 512
OUT_ROWS = 512


def _vmem_limit(estimate_bytes):
    return int(min(estimate_bytes * 5 // 4 + (4 << 20), V7X_VMEM_BYTES - (4 << 20)))


def _rms_norm(x, gain):
    ms = jnp.mean(x * x, axis=-1, keepdims=True)
    return x * lax.rsqrt(ms + EPS) * gain


def _dot(a, b):
    return jnp.dot(a, b, preferred_element_type=F32)


def _dot_nt(a, b):
    return lax.dot_general(a, b, (((1,), (1,)), ((), ())), preferred_element_type=F32)


def _dot_tn(a, b):
    return lax.dot_general(a, b, (((0,), (0,)), ((), ())), preferred_element_type=F32)


def _split3(a):
    a1 = a.astype(BF16)
    r1 = a - a1.astype(F32)
    a2 = r1.astype(BF16)
    a3 = (r1 - a2.astype(F32)).astype(BF16)
    return a1, a2, a3


def _dot_exact_rhs(a, rhs, terms=3):
    parts = _split3(a)[:terms]
    out = _dot(parts[0], rhs)
    for part in parts[1:]:
        out = out + _dot(part, rhs)
    return out


def _dot_exact_lhs(lhs, a, terms=3):
    parts = _split3(a)[:terms]
    out = _dot(lhs, parts[0])
    for part in parts[1:]:
        out = out + _dot(lhs, part)
    return out


def _sigmoid(x):
    return 1.0 / (1.0 + jnp.exp(-x))


def _cumsum_rows(a):
    row = lax.broadcasted_iota(jnp.int32, a.shape, 0)
    shift = 1
    while shift < a.shape[0]:
        a = a + jnp.where(row >= shift, pltpu.roll(a, shift, axis=0), 0.0)
        shift *= 2
    return a


def _cast_kernel(w_ref, o_ref):
    o_ref[...] = w_ref[...].astype(o_ref.dtype)


def _to_bf16(w):
    depth, rows, cols = w.shape
    tr = CAST_ROWS
    assert rows % tr == 0
    spec = pl.BlockSpec((None, tr, cols), lambda l, i: (l, i, 0))
    return pl.pallas_call(
        _cast_kernel,
        grid=(depth, rows // tr),
        in_specs=[spec],
        out_specs=spec,
        out_shape=jax.ShapeDtypeStruct(w.shape, BF16),
        compiler_params=pltpu.CompilerParams(
            dimension_semantics=("arbitrary", "arbitrary"),
            vmem_limit_bytes=_vmem_limit(2 * tr * cols * (4 + 2))),
        name="cast_bf16",
    )(w)


def _swiglu_residual(x, gain, wg_ref, wu_ref, wd_ref):
    h = _rms_norm(x, gain).astype(BF16)
    gate = _dot(h, wg_ref[...])
    up = _dot(h, wu_ref[...])
    act = (gate * _sigmoid(gate) * up).astype(BF16)
    return x + 0.5 * _dot(act, wd_ref[...])


def _ffn_kernel(x_ref, g_ref, wg_ref, wu_ref, wd_ref, o_ref):
    part_rows = x_ref.shape[0] // FFN_SPLIT
    for part in range(FFN_SPLIT):
        rows = slice(part * part_rows, (part + 1) * part_rows)
        o_ref[rows, :] = _swiglu_residual(x_ref[rows, :], g_ref[...], wg_ref, wu_ref, wd_ref)


def _ffn(x2d, gain, wg, wu, wd, layer):
    t, d = x2d.shape
    f = wg.shape[2]
    tm = min(FFN_ROWS, t)
    const = dict(pipeline_mode=pl.Buffered(1))
    est = 3 * d * f * 2 + 4 * tm * d * 4 + tm * f * (4 + 4 + 2 + 4) + tm * d * 4
    return pl.pallas_call(
        _ffn_kernel,
        grid=(t // tm,),
        in_specs=[
            pl.BlockSpec((tm, d), lambda i: (i, 0)),
            pl.BlockSpec((1, d), lambda i: (0, 0)),
            pl.BlockSpec((None, d, f), lambda i: (layer, 0, 0), **const),
            pl.BlockSpec((None, d, f), lambda i: (layer, 0, 0), **const),
            pl.BlockSpec((None, f, d), lambda i: (layer, 0, 0), **const),
        ],
        out_specs=pl.BlockSpec((tm, d), lambda i: (i, 0)),
        out_shape=jax.ShapeDtypeStruct((t, d), F32),
        compiler_params=pltpu.CompilerParams(
            dimension_semantics=("arbitrary",), vmem_limit_bytes=_vmem_limit(est)),
        name="ffn",
    )(x2d, gain, wg, wu, wd)


def _proj_kernel(x_ref, g_ref, wqk_ref, wvt_ref, wf_ref, wh_ref, wu_ref, qg_ref, kg_ref,
                 fb_ref, lb_ref, q_ref, k_ref, vt_ref, cs_ref, hg_ref, u_ref, carry_ref):
    tm = x_ref.shape[0]
    tk = vt_ref.shape[2]

    @pl.when(pl.program_id(1) == 0)
    def _():
        carry_ref[...] = jnp.zeros_like(carry_ref)

    h = _rms_norm(x_ref[...], g_ref[...]).astype(BF16)

    li = lax.broadcasted_iota(jnp.int32, (FOX_WIDTH, FOX_WIDTH), 0) // FOX_HEAD_DIM
    lj = lax.broadcasted_iota(jnp.int32, (FOX_WIDTH, FOX_WIDTH), 1) // FOX_HEAD_DIM
    group = (li == lj).astype(BF16)

    def head_norm(a, gain):
        ms = _dot_exact_rhs(a * a, group, terms=1) * (1.0 / FOX_HEAD_DIM)
        return a * lax.rsqrt(ms + EPS) * gain

    qk = _dot(h, wqk_ref[...])
    q_ref[...] = (head_norm(qk[:, :FOX_WIDTH], qg_ref[...]) * (LOG2E / math.sqrt(FOX_HEAD_DIM))).astype(BF16)
    k_ref[...] = head_norm(qk[:, FOX_WIDTH:], kg_ref[...]).astype(BF16)
    vt = _dot_nt(wvt_ref[...], h).astype(BF16)
    for blk in range(tm // tk):
        vt_ref[blk] = vt[:, blk * tk:(blk + 1) * tk]

    fl = _dot(h, wf_ref[...]) + fb_ref[...]
    ls = jnp.minimum(fl, 0.0) - jnp.log1p(jnp.exp(-jnp.abs(fl)))
    c = _cumsum_rows(ls) + carry_ref[0:1, :]
    carry_ref[...] = jnp.broadcast_to(c[tm - 1:tm, :], carry_ref.shape)
    c1, c2, c3 = _split3(c * LOG2E)
    lane = lax.broadcasted_iota(jnp.int32, c.shape, 1)
    cs_ref[...] = jnp.where(lane < FOX_HEADS, c1, jnp.where(lane < 2 * FOX_HEADS, c2, c3))

    hg = _dot(h, wh_ref[...])
    w = HGRN_WIDTH
    lb = lb_ref[...]
    f = lb + (1.0 - lb) * _sigmoid(hg[:, w:2 * w])
    g = hg[:, 3 * w:4 * w]
    hg_ref[:, 0:w] = hg[:, 0:w]
    hg_ref[:, w:2 * w] = jnp.log(f)
    hg_ref[:, 2 * w:3 * w] = f
    hg_ref[:, 3 * w:4 * w] = hg[:, 2 * w:3 * w]
    hg_ref[:, 4 * w:5 * w] = g * (0.5 * (jnp.tanh(0.5 * g) + 1.0))
    u_ref[...] = _dot(h, wu_ref[...]).reshape(u_ref.shape)


def _proj(x, gain, wqk, wvt, wf, wh, wu, qg, kg, fb, lb):
    b, l, d = x.shape
    tm = min(PROJ_ROWS, l)
    tk = min(ATTN_BLOCK, l)
    hw = 5 * HGRN_WIDTH
    full = lambda shape: pl.BlockSpec(shape, lambda bi, i: tuple(0 for _ in shape))
    est = 2 * (d * (512 + 256 + 128 + hw + 256) * 2) + 2 * tm * d * 4 + 2 * tm * (hw + 256 + 512) * 4 + 2 * tm * hw * 4
    return pl.pallas_call(
        _proj_kernel,
        grid=(b, l // tm),
        in_specs=[
            pl.BlockSpec((None, tm, d), lambda bi, i: (bi, i, 0)),
            full((1, d)), full(wqk.shape), full(wvt.shape), full(wf.shape), full(wh.shape),
            full(wu.shape), full((1, FOX_WIDTH)), full((1, FOX_WIDTH)), full((1, 128)),
            full((1, HGRN_WIDTH)),
        ],
        out_specs=[
            pl.BlockSpec((None, tm, FOX_WIDTH), lambda bi, i: (bi, i, 0)),
            pl.BlockSpec((None, tm, FOX_WIDTH), lambda bi, i: (bi, i, 0)),
            pl.BlockSpec((None, tm // tk, FOX_WIDTH, tk), lambda bi, i: (bi, i, 0, 0)),
            pl.BlockSpec((None, tm, 128), lambda bi, i: (bi, i, 0)),
            pl.BlockSpec((None, tm, hw), lambda bi, i: (bi, i, 0)),
            pl.BlockSpec((tm // S5_CHUNK, None, S5_CHUNK, S5_WIDTH), lambda bi, i: (i, bi, 0, 0)),
        ],
        out_shape=[
            jax.ShapeDtypeStruct((b, l, FOX_WIDTH), BF16),
            jax.ShapeDtypeStruct((b, l, FOX_WIDTH), BF16),
            jax.ShapeDtypeStruct((b, l // tk, FOX_WIDTH, tk), BF16),
            jax.ShapeDtypeStruct((b, l, 128), BF16),
            jax.ShapeDtypeStruct((b, l, hw), F32),
            jax.ShapeDtypeStruct((l // S5_CHUNK, b, S5_CHUNK, S5_WIDTH), F32),
        ],
        scratch_shapes=[pltpu.VMEM((V7X_SUBLANES, 128), F32)],
        compiler_params=pltpu.CompilerParams(
            dimension_semantics=("arbitrary", "arbitrary"), vmem_limit_bytes=_vmem_limit(est)),
        name="proj",
    )(x, gain, wqk, wvt, wf, wh, wu, qg, kg, fb, lb)


def _attn_kernel(bounded_ref, q_ref, k_ref, vt_ref, cs_ref, e_ref, pq_ref, pk_ref, oq_ref, ok_ref,
                 o_ref, ka_ref, qa_ref, m_ref, l_ref, acc_ref, s_ref, p_ref):
    tq = o_ref.shape[0]
    i = pl.program_id(1)

    @pl.when(i == 0)
    def _():
        cs = cs_ref[...]
        ka_ref[...] = (_dot(k_ref[...], e_ref[...]) + _dot(cs, pk_ref[...]) + ok_ref[...]).astype(BF16)
        qa_ref[...] = (_dot(q_ref[...], e_ref[...]) + _dot(cs, pq_ref[...]) + oq_ref[...]).astype(BF16)

    l_ref[...] = jnp.zeros_like(l_ref)
    acc_ref[...] = jnp.zeros_like(acc_ref)
    nblk = ATTN_KV_BLOCKS
    tk = nblk * tq
    nfull = i // nblk
    qrows = pl.ds(pl.multiple_of(i * tq, tq), tq)
    key_minus_qry = (lax.broadcasted_iota(jnp.int32, (tk, tq), 0)
                     - lax.broadcasted_iota(jnp.int32, (tk, tq), 1))

    def causal(s, jj):
        return jnp.where(key_minus_qry <= (i - jj * nblk) * tq, s, -jnp.inf)

    def pv_dot(jj, hd, p):
        head_rows = slice(hd * FOX_HEAD_DIM, (hd + 1) * FOX_HEAD_DIM)
        pv = _dot(vt_ref[jj * nblk, head_rows, :], p[0:tq, :])
        for bk in range(1, nblk):
            pv = pv + _dot(vt_ref[jj * nblk + bk, head_rows, :], p[bk * tq:(bk + 1) * tq, :])
        return pv

    def bounded_steps(steps):
        for slot, jj in enumerate(steps):
            rows = pl.ds(pl.multiple_of(jj * tk, tk), tk)
            for hd in range(FOX_HEADS):
                slab = slice(hd * 128, (hd + 1) * 128)
                p = jnp.exp2(causal(_dot_nt(ka_ref[rows, slab], qa_ref[qrows, slab]), jj))
                l_ref[hd] = l_ref[hd] + jnp.sum(p, axis=0, keepdims=True)
                p_ref[slot, hd] = p.astype(BF16)
        for hd in range(FOX_HEADS):
            pv = pv_dot(steps[0], hd, p_ref[0, hd])
            for slot, jj in enumerate(steps[1:], 1):
                pv = pv + pv_dot(jj, hd, p_ref[slot, hd])
            acc_ref[hd] = acc_ref[hd] + pv

    @pl.when(bounded_ref[0] == 1)
    def _():
        def pair(kk, carry):
            bounded_steps([2 * kk, 2 * kk + 1])
            return carry

        lax.fori_loop(0, (nfull + 1) // 2, pair, 0)

        @pl.when(nfull % 2 == 0)
        def _():
            bounded_steps([nfull])

    @pl.when(bounded_ref[0] == 0)
    def _():
        _attn_online(i, nfull, tk, qrows, causal, pv_dot, ka_ref, qa_ref, m_ref, l_ref, acc_ref,
                     s_ref, p_ref.at[0])

    outs = [acc_ref[hd] / l_ref[hd] for hd in range(FOX_HEADS)]
    o_ref[...] = jnp.concatenate(outs, axis=0).T.astype(o_ref.dtype)


def _attn_online(i, nfull, tk, qrows, causal, pv_dot, ka_ref, qa_ref, m_ref, l_ref, acc_ref, s_ref, p_ref):
    m_ref[...] = jnp.full(m_ref.shape, -jnp.inf, F32)

    def logits(jj, slot):
        rows = pl.ds(pl.multiple_of(jj * tk, tk), tk)
        for hd in range(FOX_HEADS):
            slab = slice(hd * 128, (hd + 1) * 128)
            s_ref[slot, hd] = _dot_nt(ka_ref[rows, slab], qa_ref[qrows, slab])

    def update(jj, slot, last):
        alphas = []
        for hd in range(FOX_HEADS):
            s = s_ref[slot, hd]
            if last:
                s = causal(s, jj)
            m_old = m_ref[hd]
            m_new = jnp.maximum(m_old, jnp.max(s, axis=0, keepdims=True))
            alpha = jnp.exp2(m_old - m_new)
            p = jnp.exp2(s - m_new)
            m_ref[hd] = m_new
            l_ref[hd] = alpha * l_ref[hd] + jnp.sum(p, axis=0, keepdims=True)
            p_ref[hd] = p.astype(BF16)
            alphas.append(alpha)
        for hd in range(FOX_HEADS):
            acc_ref[hd] = alphas[hd] * acc_ref[hd] + pv_dot(jj, hd, p_ref[hd])

    def pair(kk, carry):
        logits(2 * kk + 1, 1)
        update(2 * kk, 0, False)
        logits(2 * kk + 2, 0)
        update(2 * kk + 1, 1, False)
        return carry

    logits(0, 0)
    npair = nfull // 2
    lax.fori_loop(0, npair, pair, 0)

    @pl.when(nfull % 2 == 0)
    def _():
        update(2 * npair, 0, True)

    @pl.when(nfull % 2 == 1)
    def _():
        logits(2 * npair + 1, 1)
        update(2 * npair, 0, False)
        update(2 * npair + 1, 1, True)


def _attn_placement():
    w = FOX_HEADS * 128
    e = np.zeros((FOX_WIDTH, w), np.float32)
    cols = np.arange(FOX_WIDTH)
    e[cols, (cols // FOX_HEAD_DIM) * 128 + cols % FOX_HEAD_DIM] = 1.0
    pq, pk = np.zeros((128, w), np.float32), np.zeros((128, w), np.float32)
    oq, ok = np.zeros((1, w), np.float32), np.zeros((1, w), np.float32)
    for hd in range(FOX_HEADS):
        base = hd * 128 + FOX_HEAD_DIM
        for term in range(3):
            pk[term * FOX_HEADS + hd, base + term] = -1.0
            oq[0, base + term] = 1.0
            pq[term * FOX_HEADS + hd, base + 3 + term] = 1.0
            ok[0, base + 3 + term] = 1.0
    return (jnp.asarray(e, BF16), jnp.asarray(pq, BF16), jnp.asarray(pk, BF16),
            jnp.asarray(oq), jnp.asarray(ok))


def _attn(q, k, vt, cs, q_gain, k_gain):
    b, l, w = q.shape
    nkv, tk = vt.shape[1], vt.shape[3]
    tq = tk
    wa = FOX_HEADS * 128
    e, pq, pk, oq, ok = _attn_placement()
    full = lambda shape: pl.BlockSpec(shape, lambda bi, i: tuple(0 for _ in shape))
    seq = lambda width: pl.BlockSpec((None, l, width), lambda bi, i: (bi, 0, 0))
    assert (l // tq) % ATTN_KV_BLOCKS == 0
    logit_bound = (1.01 * FOX_HEAD_DIM * LOG2E / math.sqrt(FOX_HEAD_DIM)
                   * jnp.max(jnp.abs(q_gain.astype(F32))) * jnp.max(jnp.abs(k_gain.astype(F32))))
    bounded = (logit_bound < ATTN_MAX_FREE_LOGIT).astype(jnp.int32).reshape(1)
    est = (2 * (l * w * 2 * 2 + l * 128 * 2) + 2 * l * wa * (2 + 4 + 4)
           + FOX_HEADS * ATTN_KV_BLOCKS * tq * tq * (2 * 4 + 2 * 2 + 2 * 4) + FOX_HEADS * 128 * tq * 4)
    return pl.pallas_call(
        _attn_kernel,
        grid=(b, l // tq),
        in_specs=[
            pl.BlockSpec(memory_space=pltpu.SMEM),
            seq(w), seq(w),
            pl.BlockSpec((None, nkv, w, tk), lambda bi, i: (bi, 0, 0, 0)),
            seq(128),
            full(e.shape), full(pq.shape), full(pk.shape), full(oq.shape), full(ok.shape),
        ],
        out_specs=pl.BlockSpec((None, tq, w), lambda bi, i: (bi, i, 0)),
        out_shape=jax.ShapeDtypeStruct((b, l, w), BF16),
        scratch_shapes=[pltpu.VMEM((l, wa), BF16), pltpu.VMEM((l, wa), BF16),
                        pltpu.VMEM((FOX_HEADS, 1, tq), F32), pltpu.VMEM((FOX_HEADS, 1, tq), F32),
                        pltpu.VMEM((FOX_HEADS, FOX_HEAD_DIM, tq), F32),
                        pltpu.VMEM((2, FOX_HEADS, ATTN_KV_BLOCKS * tq, tq), F32),
                        pltpu.VMEM((2, FOX_HEADS, ATTN_KV_BLOCKS * tq, tq), BF16)],
        compiler_params=pltpu.CompilerParams(
            dimension_semantics=("arbitrary", "arbitrary"), vmem_limit_bytes=_vmem_limit(est)),
        name="fox_attn",
    )(bounded, q, k, vt, cs, e, pq, pk, oq, ok)


def _hgrn_levels(c):
    return [c >> (s + 1) for s in range(int(math.log2(c)))]


def _hgrn_level_ids(c):
    t = np.arange(c)[:, None]
    s = np.arange(c)[None, :]
    lvl = np.floor(np.log2(np.maximum(t ^ s, 1))).astype(np.int32) + 1
    return jnp.asarray(np.where(s > t, -1, np.where(s == t, 0, lvl)).astype(np.int32))


def _hgrn_kernel(q_ref, lf_ref, f_ref, i_ref, gate_ref, og_ref, lvl_ref, o_ref, st_ref, b_ref):
    @pl.when(pl.program_id(1) == 0)
    def _():
        st_ref[...] = jnp.zeros_like(st_ref)

    for sub in range(q_ref.shape[0] // HGRN_CHUNK):
        rows = pl.ds(sub * HGRN_CHUNK, HGRN_CHUNK)
        _hgrn_chunk(q_ref.at[rows], lf_ref.at[rows], f_ref.at[rows], i_ref.at[rows], gate_ref.at[rows],
                    og_ref, lvl_ref, o_ref.at[rows], st_ref, b_ref.at[rows])


def _hgrn_chunk(q_ref, lf_ref, f_ref, i_ref, gate_ref, og_ref, lvl_ref, o_ref, st_ref, b_ref):
    c = q_ref.shape[0]
    f = f_ref[...]
    key = 1.0 - f
    q = q_ref[...]
    ti = lax.broadcasted_iota(jnp.int32, (c, c), 0)
    si = lax.broadcasted_iota(jnp.int32, (c, c), 1)
    lower = (si <= ti).astype(BF16)
    b = _dot_exact_lhs(lower, lf_ref[...], terms=2)
    b_ref[...] = b
    row = lax.broadcasted_iota(jnp.int32, b.shape, 0)

    def ref_rows(m):
        parts = [jnp.broadcast_to(b_ref[g0 + m - 1:g0 + m, :], (2 * m, b.shape[1]))
                 for g0 in range(0, c, 2 * m)]
        return jnp.concatenate(parts, axis=0) if len(parts) > 1 else parts[0]

    def level_decay(m):
        if m == 1:
            return jnp.where(row % 2 == 1, f, 1.0)
        if m == 2:
            off = row % 4
            nxt, prv = pltpu.roll(f, c - 1, axis=0), pltpu.roll(f, 1, axis=0)
            return jnp.where(off == 0, nxt, jnp.where(off == 1, 1.0, jnp.where(off == 2, f, f * prv)))
        return jnp.exp2(jnp.abs(b - ref_rows(m)) * (-LOG2E))

    lvl = lvl_ref[...]
    qb, kb = q.astype(BF16), key.astype(BF16)
    scores = [jnp.zeros((c, c), F32) for _ in range(HGRN_HEADS)]
    for idx, m in enumerate([0] + _hgrn_levels(c)[::-1]):
        if m == 0:
            z, y = qb, kb
        else:
            e = level_decay(m).astype(BF16)
            z, y = qb * e, kb * e
        mask = lvl == idx
        for hd in range(HGRN_HEADS):
            sl = slice(hd * HGRN_DIM, (hd + 1) * HGRN_DIM)
            scores[hd] = jnp.where(mask, _dot_nt(z[:, sl], y[:, sl]), scores[hd])

    b_last = b[c - 1:c, :]
    q_in = (q * jnp.exp(b)).astype(BF16)
    k_out = (key * jnp.exp(b_last - b)).astype(BF16)
    v = i_ref[...].astype(BF16)
    gate = gate_ref[...]
    og = og_ref[...]
    decay = jnp.exp(b_last)
    for hd in range(HGRN_HEADS):
        sl = slice(hd * HGRN_DIM, (hd + 1) * HGRN_DIM)
        st = st_ref[hd]
        o = _dot(scores[hd].astype(BF16), v[:, sl]) + _dot_nt(q_in[:, sl], st.astype(BF16))
        st_ref[hd] = st * decay[:, sl] + _dot_tn(v[:, sl], k_out[:, sl])
        o_ref[:, sl] = (_rms_norm(o, og[:, sl]) * gate[:, sl]).astype(o_ref.dtype)


def _hgrn(hg, og):
    b, l, _ = hg.shape
    c = HGRN_CHUNK
    tl = HGRN_CHUNK * HGRN_CHUNKS_PER_STEP
    assert l % tl == 0
    w = HGRN_WIDTH
    part = lambda p: pl.BlockSpec((None, tl, w), lambda bi, i, p=p: (bi, i, p))
    est = 2 * 6 * tl * w * 4 + 16 * tl * w * 4 + HGRN_HEADS * HGRN_DIM * HGRN_DIM * 4
    return pl.pallas_call(
        _hgrn_kernel,
        grid=(b, l // tl),
        in_specs=[part(0), part(1), part(2), part(3), part(4),
                  pl.BlockSpec((1, w), lambda bi, i: (0, 0)),
                  pl.BlockSpec((c, c), lambda bi, i: (0, 0))],
        out_specs=pl.BlockSpec((None, tl, w), lambda bi, i: (bi, i, 0)),
        out_shape=jax.ShapeDtypeStruct((b, l, w), BF16),
        scratch_shapes=[pltpu.VMEM((HGRN_HEADS, HGRN_DIM, HGRN_DIM), F32),
                        pltpu.VMEM((tl, w), F32)],
        compiler_params=pltpu.CompilerParams(
            dimension_semantics=("arbitrary", "arbitrary"), vmem_limit_bytes=_vmem_limit(est)),
        name="hgrn2",
    )(hg, hg, hg, hg, hg, og, _hgrn_level_ids(c))


def _s5_kernel(u_ref, kd_ref, pin_ref, pout_ref, ar_ref, ai_ref, y_ref, xr_ref, xi_ref):
    nc, nb, cs, w = u_ref.shape
    ns = ar_ref.shape[1]

    @pl.when(pl.program_id(0) == 0)
    def _():
        xr_ref[...] = jnp.zeros_like(xr_ref)
        xi_ref[...] = jnp.zeros_like(xi_ref)

    us = [u_ref[:, :, s, :].reshape(nc * nb, w).astype(BF16) for s in range(cs)]
    win = _dot(us[0], pin_ref[0])
    for s in range(1, cs):
        win = win + _dot(us[s], pin_ref[s])

    ar, ai = ar_ref[...], ai_ref[...]
    xr, xi = xr_ref[...], xi_ref[...]
    prev_r, prev_i = [], []
    for c in range(nc):
        prev_r.append(xr)
        prev_i.append(xi)
        wr = win[c * nb:(c + 1) * nb, :ns]
        wi = win[c * nb:(c + 1) * nb, ns:]
        xr, xi = ar * xr - ai * xi + wr, ar * xi + ai * xr + wi
    xr_ref[...] = xr
    xi_ref[...] = xi
    xp = jnp.concatenate([jnp.concatenate(prev_r, axis=0), jnp.concatenate(prev_i, axis=0)],
                         axis=1).astype(BF16)

    for t in range(cs):
        lags = jnp.concatenate([us[t - d] for d in range(t + 1)], axis=1) if t else us[0]
        yt = _dot(xp, pout_ref[t]) + _dot(lags, kd_ref[0:(t + 1) * w, :])
        y_ref[:, :, t, :] = yt.reshape(nc, nb, w)


def _s5(u4, kd, pin, pout, ar, ai):
    nchunk, nb, cs, w = u4.shape
    nc = min(S5_TILE // cs, nchunk)
    const = dict(pipeline_mode=pl.Buffered(1))
    full = lambda shape: pl.BlockSpec(shape, lambda i: tuple(0 for _ in shape), **const)
    rows = nc * cs * nb
    chunked = pl.BlockSpec((nc, nb, cs, w), lambda i: (i, 0, 0, 0))
    est = (4 * rows * w * 4 + (kd.size + pin.size + pout.size) * 2 + rows // cs * 2 * S5_NSTATE * (4 + 4 + 2)
           + rows * w * (2 + 2 + 4))
    return pl.pallas_call(
        _s5_kernel,
        grid=(nchunk // nc,),
        in_specs=[chunked, full(kd.shape), full(pin.shape), full(pout.shape), full(ar.shape), full(ai.shape)],
        out_specs=chunked,
        out_shape=jax.ShapeDtypeStruct(u4.shape, F32),
        scratch_shapes=[pltpu.VMEM((nb, S5_NSTATE), F32), pltpu.VMEM((nb, S5_NSTATE), F32)],
        compiler_params=pltpu.CompilerParams(
            dimension_semantics=("arbitrary",), vmem_limit_bytes=_vmem_limit(est)),
        name="s5",
    )(u4, kd, pin, pout, ar, ai)


def _s5_operators(lam_re, lam_im, log_step, b_re, b_im, c_re, c_im):
    lr, li = lam_re.astype(F32), lam_im.astype(F32)
    dt = jnp.exp(log_step.astype(F32))[:, None]
    mag = jnp.exp(lr * dt)
    lbar_re, lbar_im = mag * jnp.cos(li * dt), mag * jnp.sin(li * dt)
    nr, ni = lbar_re - 1.0, lbar_im
    den = lr * lr + li * li
    coef_re = (nr * lr + ni * li) / den
    coef_im = (ni * lr - nr * li) / den
    br, bi = b_re.astype(F32), b_im.astype(F32)
    bbar_re = coef_re[..., None] * br - coef_im[..., None] * bi
    bbar_im = coef_re[..., None] * bi + coef_im[..., None] * br
    cr, ci = c_re.astype(F32), c_im.astype(F32)

    pr, pi = [jnp.ones_like(lbar_re)], [jnp.zeros_like(lbar_re)]
    for _ in range(S5_CHUNK):
        pr, pi = (pr + [pr[-1] * lbar_re - pi[-1] * lbar_im],
                  pi + [pr[-1] * lbar_im + pi[-1] * lbar_re])
    pw_re, pw_im = jnp.stack(pr), jnp.stack(pi)
    cs = S5_CHUNK

    def block_diag(compact, rows_group, cols_group, reps):
        tiled = jnp.tile(compact, reps)
        r = lax.broadcasted_iota(jnp.int32, tiled.shape, tiled.ndim - 2)
        c = lax.broadcasted_iota(jnp.int32, tiled.shape, tiled.ndim - 1)
        return jnp.where(rows_group(r) == cols_group(c), tiled, 0.0).astype(BF16)

    lane_group = lambda c: c // S5_GROUP
    state_group = lambda c: (c % S5_NSTATE) // S5_STATE

    ab_re = pw_re[:cs, :, :, None] * bbar_re - pw_im[:cs, :, :, None] * bbar_im
    ab_im = pw_re[:cs, :, :, None] * bbar_im + pw_im[:cs, :, :, None] * bbar_re
    taps = jnp.sum(cr[None, :, :, :, None] * ab_re[:, :, None] - ci[None, :, :, :, None] * ab_im[:, :, None],
                   axis=3)
    kd = block_diag(taps.transpose(0, 1, 3, 2).reshape(cs * S5_WIDTH, S5_GROUP),
                    lambda r: (r // S5_GROUP) % S5_GROUPS, lane_group, (1, S5_GROUPS))

    rev = cs - 1 - jnp.arange(cs)
    ab = jnp.stack([ab_re[rev], ab_im[rev]], axis=1)
    pin = block_diag(ab.transpose(0, 4, 1, 2, 3).reshape(cs, S5_GROUP, 2 * S5_NSTATE),
                     lane_group, state_group, (1, S5_GROUPS, 1))

    qr, qi = pw_re[1:], pw_im[1:]
    po_re = cr[None] * qr[:, :, None, :] - ci[None] * qi[:, :, None, :]
    po_im = -(cr[None] * qi[:, :, None, :] + ci[None] * qr[:, :, None, :])
    po = jnp.stack([po_re, po_im], axis=1)
    pout = block_diag(po.transpose(0, 1, 2, 4, 3).reshape(cs, 2 * S5_NSTATE, S5_GROUP),
                      state_group, lane_group, (1, 1, S5_GROUPS))
    a_re = pw_re[cs].reshape(1, S5_NSTATE)
    a_im = pw_im[cs].reshape(1, S5_NSTATE)
    return kd, pin, pout, a_re, a_im


def _out_ffn_kernel(x_ref, oa_ref, ob_ref, y_ref, u_ref, d_ref, gw_ref, gb_ref, wa_ref, wb_ref, wc_ref,
                    g_ref, wg_ref, wu_ref, wd_ref, o_ref):
    tm = x_ref.shape[0]
    y = y_ref[...].reshape(tm, S5_WIDTH) + d_ref[...] * u_ref[...].reshape(tm, S5_WIDTH)
    z = 0.5 * y * (1.0 + jnp.tanh(math.sqrt(2.0 / math.pi) * (y + 0.044715 * (y * y * y))))
    oc = z * _sigmoid(_dot(z.astype(BF16), gw_ref[...]) + gb_ref[...])
    x = (x_ref[...] + _dot(oa_ref[...], wa_ref[...]) + _dot(ob_ref[...], wb_ref[...])
         + _dot(oc.astype(BF16), wc_ref[...]))
    o_ref[...] = _swiglu_residual(x, g_ref[...], wg_ref, wu_ref, wd_ref)


def _out_ffn(x, oa, ob, y_lb, u_lb, dvec, gw, gb, wa, wb, wc, gain, wg, wu, wd, layer):
    b, l, d = x.shape
    f = wg.shape[2]
    stacked = lambda shape: pl.BlockSpec((None,) + shape[1:], lambda bi, i: (layer, 0, 0),
                                         pipeline_mode=pl.Buffered(1))
    tm = min(OUT_ROWS, l)
    rows = lambda w: pl.BlockSpec((None, tm, w), lambda bi, i: (bi, i, 0))
    token_major = pl.BlockSpec((tm // S5_CHUNK, None, S5_CHUNK, S5_WIDTH), lambda bi, i: (i, bi, 0, 0))
    full = lambda shape: pl.BlockSpec(shape, lambda bi, i: tuple(0 for _ in shape),
                                      pipeline_mode=pl.Buffered(1))
    est = (2 * (2 * tm * d * 4 + tm * (256 * 2 + 512 * 2 + 256 * 8)) + 2 * (d * d + 256 * 256) + 3 * d * f * 2
           + tm * f * (4 + 4 + 2 + 4) + 4 * tm * d * 4)
    return pl.pallas_call(
        _out_ffn_kernel,
        grid=(b, l // tm),
        in_specs=[rows(d), rows(FOX_WIDTH), rows(HGRN_WIDTH), token_major, token_major,
                  full((1, S5_WIDTH)), full(gw.shape), full((1, S5_WIDTH)),
                  full(wa.shape), full(wb.shape), full(wc.shape),
                  full((1, d)), stacked(wg.shape), stacked(wu.shape), stacked(wd.shape)],
        out_specs=rows(d),
        out_shape=jax.ShapeDtypeStruct((b, l, d), F32),
        compiler_params=pltpu.CompilerParams(
            dimension_semantics=("arbitrary", "arbitrary"), vmem_limit_bytes=_vmem_limit(est)),
        name="out_ffn",
    )(x, oa, ob, y_lb, u_lb, dvec, gw, gb, wa, wb, wc, gain, wg, wu, wd)


def kernel(x, ffn1_norm, ffn1_w_gate, ffn1_w_up, ffn1_w_down, mix_norm, w_in, fox_f_bias, fox_q_gain, fox_k_gain, hgrn_lb, hgrn_o_gain, s5_lambda_re, s5_lambda_im, s5_log_step, s5_B_re, s5_B_im, s5_C_re, s5_C_im, s5_D, s5_glu_w, s5_glu_b, w_out, ffn2_norm, ffn2_w_gate, ffn2_w_up, ffn2_w_down):
    bsz, seq, d = x.shape
    depth = w_in.shape[0]
    t = bsz * seq

    lb_all = jnp.cumsum(jax.nn.softmax(hgrn_lb.astype(F32), axis=0), axis=0)
    lb_all = lb_all - lb_all[0:1]

    o_q, o_k, o_v = 0, FOX_WIDTH, 2 * FOX_WIDTH
    o_f = 3 * FOX_WIDTH
    o_h = o_f + FOX_HEADS
    o_u = o_h + 4 * HGRN_WIDTH

    ffn1 = [_to_bf16(w) for w in (ffn1_w_gate, ffn1_w_up, ffn1_w_down)]
    ffn2 = [_to_bf16(w) for w in (ffn2_w_gate, ffn2_w_up, ffn2_w_down)]

    for l in range(depth):
        x = _ffn(x.reshape(t, d), ffn1_norm[l][None], *ffn1, layer=l).reshape(bsz, seq, d)

        w = w_in[l]
        wf = jnp.zeros((d, 128), F32).at[:, :3 * FOX_HEADS].set(jnp.tile(w[:, o_f:o_h], (1, 3))).astype(BF16)
        fb = jnp.zeros((1, 128), F32).at[0, :3 * FOX_HEADS].set(jnp.tile(fox_f_bias[l].astype(F32), 3))
        q, k, vt, cs, hg, u_lb = _proj(
            x, mix_norm[l][None], w[:, o_q:o_v].astype(BF16), w[:, o_v:o_f].T.astype(BF16), wf,
            w[:, o_h:o_u].astype(BF16), w[:, o_u:].astype(BF16),
            jnp.tile(fox_q_gain[l].astype(F32), FOX_HEADS)[None],
            jnp.tile(fox_k_gain[l].astype(F32), FOX_HEADS)[None], fb, lb_all[l][None])

        o_a = _attn(q, k, vt, cs, fox_q_gain[l], fox_k_gain[l])
        o_b = _hgrn(hg, hgrn_o_gain[l].astype(F32)[None])
        ops = _s5_operators(s5_lambda_re[l], s5_lambda_im[l], s5_log_step[l],
                            s5_B_re[l], s5_B_im[l], s5_C_re[l], s5_C_im[l])
        y_lb = _s5(u_lb, *ops)

        wo = w_out[l].astype(BF16)
        x = _out_ffn(x, o_a, o_b, y_lb, u_lb, s5_D[l].astype(F32)[None], s5_glu_w[l].astype(BF16),
                     s5_glu_b[l].astype(F32)[None], wo[:FOX_WIDTH], wo[FOX_WIDTH:FOX_WIDTH + HGRN_WIDTH],
                     wo[FOX_WIDTH + HGRN_WIDTH:], ffn2_norm[l][None], *ffn2, layer=l)
    return x
```

```python
import math

import jax
import jax.numpy as jnp
import numpy as np
from jax import lax
from jax.experimental import pallas as pl
from jax.experimental.pallas import tpu as pltpu

F32 = jnp.float32
BF16 = jnp.bfloat16

EPS = 1e-6
FOX_HEADS = 4
FOX_HEAD_DIM = 64
FOX_WIDTH = FOX_HEADS * FOX_HEAD_DIM
HGRN_HEADS = 4
HGRN_DIM = 128
HGRN_WIDTH = HGRN_HEADS * HGRN_DIM
S5_GROUPS = 16
S5_GROUP = 16
S5_STATE = 64
S5_WIDTH = S5_GROUPS * S5_GROUP
S5_NSTATE = S5_GROUPS * S5_STATE
S5_CHUNK = 8
S5_TILE = 128
CAST_ROWS = 256
LOG2E = math.log2(math.e)

V7X_VMEM_BYTES = 64 * 1024 * 1024
V7X_SUBLANES = 8

ATTN_BLOCK = 512
ATTN_KV_BLOCKS = 1
ATTN_MAX_FREE_LOGIT = 40.0
HGRN_CHUNK = 128
HGRN_CHUNKS_PER_STEP = 8
FFN_ROWS = 512
FFN_SPLIT = 4
PROJ_ROWS = 512
OUT_ROWS = 512


def _vmem_limit(estimate_bytes):
    return int(min(estimate_bytes * 5 // 4 + (4 << 20), V7X_VMEM_BYTES - (4 << 20)))


def _rms_norm(x, gain):
    ms = jnp.mean(x * x, axis=-1, keepdims=True)
    return x * lax.rsqrt(ms + EPS) * gain


def _dot(a, b):
    return jnp.dot(a, b, preferred_element_type=F32)


def _dot_nt(a, b):
    return lax.dot_general(a, b, (((1,), (1,)), ((), ())), preferred_element_type=F32)


def _dot_tn(a, b):
    return lax.dot_general(a, b, (((0,), (0,)), ((), ())), preferred_element_type=F32)


def _split3(a):
    a1 = a.astype(BF16)
    r1 = a - a1.astype(F32)
    a2 = r1.astype(BF16)
    a3 = (r1 - a2.astype(F32)).astype(BF16)
    return a1, a2, a3


def _dot_exact_rhs(a, rhs, terms=3):
    parts = _split3(a)[:terms]
    out = _dot(parts[0], rhs)
    for part in parts[1:]:
        out = out + _dot(part, rhs)
    return out


def _dot_exact_lhs(lhs, a, terms=3):
    parts = _split3(a)[:terms]
    out = _dot(lhs, parts[0])
    for part in parts[1:]:
        out = out + _dot(lhs, part)
    return out


def _sigmoid(x):
    return 1.0 / (1.0 + jnp.exp(-x))


def _cumsum_rows(a):
    row = lax.broadcasted_iota(jnp.int32, a.shape, 0)
    shift = 1
    while shift < a.shape[0]:
        a = a + jnp.where(row >= shift, pltpu.roll(a, shift, axis=0), 0.0)
        shift *= 2
    return a


def _cast_kernel(w_ref, o_ref):
    o_ref[...] = w_ref[...].astype(o_ref.dtype)


def _to_bf16(w):
    depth, rows, cols = w.shape
    tr = CAST_ROWS
    assert rows % tr == 0
    spec = pl.BlockSpec((None, tr, cols), lambda l, i: (l, i, 0))
    return pl.pallas_call(
        _cast_kernel,
        grid=(depth, rows // tr),
        in_specs=[spec],
        out_specs=spec,
        out_shape=jax.ShapeDtypeStruct(w.shape, BF16),
        compiler_params=pltpu.CompilerParams(
            dimension_semantics=("arbitrary", "arbitrary"),
            vmem_limit_bytes=_vmem_limit(2 * tr * cols * (4 + 2))),
        name="cast_bf16",
    )(w)


def _swiglu_residual(x, gain, wg_ref, wu_ref, wd_ref):
    h = _rms_norm(x, gain).astype(BF16)
    gate = _dot(h, wg_ref[...])
    up = _dot(h, wu_ref[...])
    act = (gate * _sigmoid(gate) * up).astype(BF16)
    return x + 0.5 * _dot(act, wd_ref[...])


def _ffn_kernel(x_ref, g_ref, wg_ref, wu_ref, wd_ref, o_ref):
    part_rows = x_ref.shape[0] // FFN_SPLIT
    for part in range(FFN_SPLIT):
        rows = slice(part * part_rows, (part + 1) * part_rows)
        o_ref[rows, :] = _swiglu_residual(x_ref[rows, :], g_ref[...], wg_ref, wu_ref, wd_ref)


def _ffn(x2d, gain, wg, wu, wd, layer):
    t, d = x2d.shape
    f = wg.shape[2]
    tm = min(FFN_ROWS, t)
    const = dict(pipeline_mode=pl.Buffered(1))
    est = 3 * d * f * 2 + 4 * tm * d * 4 + tm * f * (4 + 4 + 2 + 4) + tm * d * 4
    return pl.pallas_call(
        _ffn_kernel,
        grid=(t // tm,),
        in_specs=[
            pl.BlockSpec((tm, d), lambda i: (i, 0)),
            pl.BlockSpec((1, d), lambda i: (0, 0)),
            pl.BlockSpec((None, d, f), lambda i: (layer, 0, 0), **const),
            pl.BlockSpec((None, d, f), lambda i: (layer, 0, 0), **const),
            pl.BlockSpec((None, f, d), lambda i: (layer, 0, 0), **const),
        ],
        out_specs=pl.BlockSpec((tm, d), lambda i: (i, 0)),
        out_shape=jax.ShapeDtypeStruct((t, d), F32),
        compiler_params=pltpu.CompilerParams(
            dimension_semantics=("arbitrary",), vmem_limit_bytes=_vmem_limit(est)),
        name="ffn",
    )(x2d, gain, wg, wu, wd)


def _proj_kernel(x_ref, g_ref, wqk_ref, wvt_ref, wf_ref, wh_ref, wu_ref, qg_ref, kg_ref,
                 fb_ref, lb_ref, q_ref, k_ref, vt_ref, cs_ref, hg_ref, u_ref, carry_ref):
    tm = x_ref.shape[0]
    tk = vt_ref.shape[2]

    @pl.when(pl.program_id(1) == 0)
    def _():
        carry_ref[...] = jnp.zeros_like(carry_ref)

    h = _rms_norm(x_ref[...], g_ref[...]).astype(BF16)

    li = lax.broadcasted_iota(jnp.int32, (FOX_WIDTH, FOX_WIDTH), 0) // FOX_HEAD_DIM
    lj = lax.broadcasted_iota(jnp.int32, (FOX_WIDTH, FOX_WIDTH), 1) // FOX_HEAD_DIM
    group = (li == lj).astype(BF16)

    def head_norm(a, gain):
        ms = _dot_exact_rhs(a * a, group, terms=1) * (1.0 / FOX_HEAD_DIM)
        return a * lax.rsqrt(ms + EPS) * gain

    qk = _dot(h, wqk_ref[...])
    q_ref[...] = (head_norm(qk[:, :FOX_WIDTH], qg_ref[...]) * (LOG2E / math.sqrt(FOX_HEAD_DIM))).astype(BF16)
    k_ref[...] = head_norm(qk[:, FOX_WIDTH:], kg_ref[...]).astype(BF16)
    vt = _dot_nt(wvt_ref[...], h).astype(BF16)
    for blk in range(tm // tk):
        vt_ref[blk] = vt[:, blk * tk:(blk + 1) * tk]

    fl = _dot(h, wf_ref[...]) + fb_ref[...]
    ls = jnp.minimum(fl, 0.0) - jnp.log1p(jnp.exp(-jnp.abs(fl)))
    c = _cumsum_rows(ls) + carry_ref[0:1, :]
    carry_ref[...] = jnp.broadcast_to(c[tm - 1:tm, :], carry_ref.shape)
    c1, c2, c3 = _split3(c * LOG2E)
    lane = lax.broadcasted_iota(jnp.int32, c.shape, 1)
    cs_ref[...] = jnp.where(lane < FOX_HEADS, c1, jnp.where(lane < 2 * FOX_HEADS, c2, c3))

    hg = _dot(h, wh_ref[...])
    w = HGRN_WIDTH
    lb = lb_ref[...]
    f = lb + (1.0 - lb) * _sigmoid(hg[:, w:2 * w])
    g = hg[:, 3 * w:4 * w]
    hg_ref[:, 0:w] = hg[:, 0:w]
    hg_ref[:, w:2 * w] = jnp.log(f)
    hg_ref[:, 2 * w:3 * w] = f
    hg_ref[:, 3 * w:4 * w] = hg[:, 2 * w:3 * w]
    hg_ref[:, 4 * w:5 * w] = g * (0.5 * (jnp.tanh(0.5 * g) + 1.0))
    u_ref[...] = _dot(h, wu_ref[...]).reshape(u_ref.shape)


def _proj(x, gain, wqk, wvt, wf, wh, wu, qg, kg, fb, lb):
    b, l, d = x.shape
    tm = min(PROJ_ROWS, l)
    tk = min(ATTN_BLOCK, l)
    hw = 5 * HGRN_WIDTH
    full = lambda shape: pl.BlockSpec(shape, lambda bi, i: tuple(0 for _ in shape))
    est = 2 * (d * (512 + 256 + 128 + hw + 256) * 2) + 2 * tm * d * 4 + 2 * tm * (hw + 256 + 512) * 4 + 2 * tm * hw * 4
    return pl.pallas_call(
        _proj_kernel,
        grid=(b, l // tm),
        in_specs=[
            pl.BlockSpec((None, tm, d), lambda bi, i: (bi, i, 0)),
            full((1, d)), full(wqk.shape), full(wvt.shape), full(wf.shape), full(wh.shape),
            full(wu.shape), full((1, FOX_WIDTH)), full((1, FOX_WIDTH)), full((1, 128)),
            full((1, HGRN_WIDTH)),
        ],
        out_specs=[
            pl.BlockSpec((None, tm, FOX_WIDTH), lambda bi, i: (bi, i, 0)),
            pl.BlockSpec((None, tm, FOX_WIDTH), lambda bi, i: (bi, i, 0)),
            pl.BlockSpec((None, tm // tk, FOX_WIDTH, tk), lambda bi, i: (bi, i, 0, 0)),
            pl.BlockSpec((None, tm, 128), lambda bi, i: (bi, i, 0)),
            pl.BlockSpec((None, tm, hw), lambda bi, i: (bi, i, 0)),
            pl.BlockSpec((tm // S5_CHUNK, None, S5_CHUNK, S5_WIDTH), lambda bi, i: (i, bi, 0, 0)),
        ],
        out_shape=[
            jax.ShapeDtypeStruct((b, l, FOX_WIDTH), BF16),
            jax.ShapeDtypeStruct((b, l, FOX_WIDTH), BF16),
            jax.ShapeDtypeStruct((b, l // tk, FOX_WIDTH, tk), BF16),
            jax.ShapeDtypeStruct((b, l, 128), BF16),
            jax.ShapeDtypeStruct((b, l, hw), F32),
            jax.ShapeDtypeStruct((l // S5_CHUNK, b, S5_CHUNK, S5_WIDTH), F32),
        ],
        scratch_shapes=[pltpu.VMEM((V7X_SUBLANES, 128), F32)],
        compiler_params=pltpu.CompilerParams(
            dimension_semantics=("arbitrary", "arbitrary"), vmem_limit_bytes=_vmem_limit(est)),
        name="proj",
    )(x, gain, wqk, wvt, wf, wh, wu, qg, kg, fb, lb)


def _attn_kernel(bounded_ref, q_ref, k_ref, vt_ref, cs_ref, e_ref, pq_ref, pk_ref, oq_ref, ok_ref,
                 o_ref, ka_ref, qa_ref, m_ref, l_ref, acc_ref, s_ref, p_ref):
    tq = o_ref.shape[0]
    i = pl.program_id(1)

    @pl.when(i == 0)
    def _():
        cs = cs_ref[...]
        ka_ref[...] = (_dot(k_ref[...], e_ref[...]) + _dot(cs, pk_ref[...]) + ok_ref[...]).astype(BF16)
        qa_ref[...] = (_dot(q_ref[...], e_ref[...]) + _dot(cs, pq_ref[...]) + oq_ref[...]).astype(BF16)

    l_ref[...] = jnp.zeros_like(l_ref)
    acc_ref[...] = jnp.zeros_like(acc_ref)
    nblk = ATTN_KV_BLOCKS
    tk = nblk * tq
    nfull = i // nblk
    qrows = pl.ds(pl.multiple_of(i * tq, tq), tq)
    key_minus_qry = (lax.broadcasted_iota(jnp.int32, (tk, tq), 0)
                     - lax.broadcasted_iota(jnp.int32, (tk, tq), 1))

    def causal(s, jj):
        return jnp.where(key_minus_qry <= (i - jj * nblk) * tq, s, -jnp.inf)

    def pv_dot(jj, hd, p):
        head_rows = slice(hd * FOX_HEAD_DIM, (hd + 1) * FOX_HEAD_DIM)
        pv = _dot(vt_ref[jj * nblk, head_rows, :], p[0:tq, :])
        for bk in range(1, nblk):
            pv = pv + _dot(vt_ref[jj * nblk + bk, head_rows, :], p[bk * tq:(bk + 1) * tq, :])
        return pv

    def bounded_steps(steps):
        for slot, jj in enumerate(steps):
            rows = pl.ds(pl.multiple_of(jj * tk, tk), tk)
            for hd in range(FOX_HEADS):
                slab = slice(hd * 128, (hd + 1) * 128)
                p = jnp.exp2(causal(_dot_nt(ka_ref[rows, slab], qa_ref[qrows, slab]), jj))
                l_ref[hd] = l_ref[hd] + jnp.sum(p, axis=0, keepdims=True)
                p_ref[slot, hd] = p.astype(BF16)
        for hd in range(FOX_HEADS):
            pv = pv_dot(steps[0], hd, p_ref[0, hd])
            for slot, jj in enumerate(steps[1:], 1):
                pv = pv + pv_dot(jj, hd, p_ref[slot, hd])
            acc_ref[hd] = acc_ref[hd] + pv

    @pl.when(bounded_ref[0] == 1)
    def _():
        def pair(kk, carry):
            bounded_steps([2 * kk, 2 * kk + 1])
            return carry

        lax.fori_loop(0, (nfull + 1) // 2, pair, 0)

        @pl.when(nfull % 2 == 0)
        def _():
            bounded_steps([nfull])

    @pl.when(bounded_ref[0] == 0)
    def _():
        _attn_online(i, nfull, tk, qrows, causal, pv_dot, ka_ref, qa_ref, m_ref, l_ref, acc_ref,
                     s_ref, p_ref.at[0])

    outs = [acc_ref[hd] / l_ref[hd] for hd in range(FOX_HEADS)]
    o_ref[...] = jnp.concatenate(outs, axis=0).T.astype(o_ref.dtype)


def _attn_online(i, nfull, tk, qrows, causal, pv_dot, ka_ref, qa_ref, m_ref, l_ref, acc_ref, s_ref, p_ref):
    m_ref[...] = jnp.full(m_ref.shape, -jnp.inf, F32)

    def logits(jj, slot):
        rows = pl.ds(pl.multiple_of(jj * tk, tk), tk)
        for hd in range(FOX_HEADS):
            slab = slice(hd * 128, (hd + 1) * 128)
            s_ref[slot, hd] = _dot_nt(ka_ref[rows, slab], qa_ref[qrows, slab])

    def update(jj, slot, last):
        alphas = []
        for hd in range(FOX_HEADS):
            s = s_ref[slot, hd]
            if last:
                s = causal(s, jj)
            m_old = m_ref[hd]
            m_new = jnp.maximum(m_old, jnp.max(s, axis=0, keepdims=True))
            alpha = jnp.exp2(m_old - m_new)
            p = jnp.exp2(s - m_new)
            m_ref[hd] = m_new
            l_ref[hd] = alpha * l_ref[hd] + jnp.sum(p, axis=0, keepdims=True)
            p_ref[hd] = p.astype(BF16)
            alphas.append(alpha)
        for hd in range(FOX_HEADS):
            acc_ref[hd] = alphas[hd] * acc_ref[hd] + pv_dot(jj, hd, p_ref[hd])

    def pair(kk, carry):
        logits(2 * kk + 1, 1)
        update(2 * kk, 0, False)
        logits(2 * kk + 2, 0)
        update(2 * kk + 1, 1, False)
        return carry

    logits(0, 0)
    npair = nfull // 2
    lax.fori_loop(0, npair, pair, 0)

    @pl.when(nfull % 2 == 0)
    def _():
        update(2 * npair, 0, True)

    @pl.when(nfull % 2 == 1)
    def _():
        logits(2 * npair + 1, 1)
        update(2 * npair, 0, False)
        update(2 * npair + 1, 1, True)


def _attn_placement():
    w = FOX_HEADS * 128
    e = np.zeros((FOX_WIDTH, w), np.float32)
    cols = np.arange(FOX_WIDTH)
    e[cols, (cols // FOX_HEAD_DIM) * 128 + cols % FOX_HEAD_DIM] = 1.0
    pq, pk = np.zeros((128, w), np.float32), np.zeros((128, w), np.float32)
    oq, ok = np.zeros((1, w), np.float32), np.zeros((1, w), np.float32)
    for hd in range(FOX_HEADS):
        base = hd * 128 + FOX_HEAD_DIM
        for term in range(3):
            pk[term * FOX_HEADS + hd, base + term] = -1.0
            oq[0, base + term] = 1.0
            pq[term * FOX_HEADS + hd, base + 3 + term] = 1.0
            ok[0, base + 3 + term] = 1.0
    return (jnp.asarray(e, BF16), jnp.asarray(pq, BF16), jnp.asarray(pk, BF16),
            jnp.asarray(oq), jnp.asarray(ok))


def _attn(q, k, vt, cs, q_gain, k_gain):
    b, l, w = q.shape
    nkv, tk = vt.shape[1], vt.shape[3]
    tq = tk
    wa = FOX_HEADS * 128
    e, pq, pk, oq, ok = _attn_placement()
    full = lambda shape: pl.BlockSpec(shape, lambda bi, i: tuple(0 for _ in shape))
    seq = lambda width: pl.BlockSpec((None, l, width), lambda bi, i: (bi, 0, 0))
    assert (l // tq) % ATTN_KV_BLOCKS == 0
    logit_bound = (1.01 * FOX_HEAD_DIM * LOG2E / math.sqrt(FOX_HEAD_DIM)
                   * jnp.max(jnp.abs(q_gain.astype(F32))) * jnp.max(jnp.abs(k_gain.astype(F32))))
    bounded = (logit_bound < ATTN_MAX_FREE_LOGIT).astype(jnp.int32).reshape(1)
    est = (2 * (l * w * 2 * 2 + l * 128 * 2) + 2 * l * wa * (2 + 4 + 4)
           + FOX_HEADS * ATTN_KV_BLOCKS * tq * tq * (2 * 4 + 2 * 2 + 2 * 4) + FOX_HEADS * 128 * tq * 4)
    return pl.pallas_call(
        _attn_kernel,
        grid=(b, l // tq),
        in_specs=[
            pl.BlockSpec(memory_space=pltpu.SMEM),
            seq(w), seq(w),
            pl.BlockSpec((None, nkv, w, tk), lambda bi, i: (bi, 0, 0, 0)),
            seq(128),
            full(e.shape), full(pq.shape), full(pk.shape), full(oq.shape), full(ok.shape),
        ],
        out_specs=pl.BlockSpec((None, tq, w), lambda bi, i: (bi, i, 0)),
        out_shape=jax.ShapeDtypeStruct((b, l, w), BF16),
        scratch_shapes=[pltpu.VMEM((l, wa), BF16), pltpu.VMEM((l, wa), BF16),
                        pltpu.VMEM((FOX_HEADS, 1, tq), F32), pltpu.VMEM((FOX_HEADS, 1, tq), F32),
                        pltpu.VMEM((FOX_HEADS, FOX_HEAD_DIM, tq), F32),
                        pltpu.VMEM((2, FOX_HEADS, ATTN_KV_BLOCKS * tq, tq), F32),
                        pltpu.VMEM((2, FOX_HEADS, ATTN_KV_BLOCKS * tq, tq), BF16)],
        compiler_params=pltpu.CompilerParams(
            dimension_semantics=("arbitrary", "arbitrary"), vmem_limit_bytes=_vmem_limit(est)),
        name="fox_attn",
    )(bounded, q, k, vt, cs, e, pq, pk, oq, ok)


def _hgrn_levels(c):
    return [c >> (s + 1) for s in range(int(math.log2(c)))]


def _hgrn_level_ids(c):
    t = np.arange(c)[:, None]
    s = np.arange(c)[None, :]
    lvl = np.floor(np.log2(np.maximum(t ^ s, 1))).astype(np.int32) + 1
    return jnp.asarray(np.where(s > t, -1, np.where(s == t, 0, lvl)).astype(np.int32))


def _hgrn_kernel(q_ref, lf_ref, f_ref, i_ref, gate_ref, og_ref, lvl_ref, o_ref, st_ref, b_ref):
    @pl.when(pl.program_id(1) == 0)
    def _():
        st_ref[...] = jnp.zeros_like(st_ref)

    for sub in range(q_ref.shape[0] // HGRN_CHUNK):
        rows = pl.ds(sub * HGRN_CHUNK, HGRN_CHUNK)
        _hgrn_chunk(q_ref.at[rows], lf_ref.at[rows], f_ref.at[rows], i_ref.at[rows], gate_ref.at[rows],
                    og_ref, lvl_ref, o_ref.at[rows], st_ref, b_ref.at[rows])


def _hgrn_chunk(q_ref, lf_ref, f_ref, i_ref, gate_ref, og_ref, lvl_ref, o_ref, st_ref, b_ref):
    c = q_ref.shape[0]
    f = f_ref[...]
    key = 1.0 - f
    q = q_ref[...]
    ti = lax.broadcasted_iota(jnp.int32, (c, c), 0)
    si = lax.broadcasted_iota(jnp.int32, (c, c), 1)
    lower = (si <= ti).astype(BF16)
    b = _dot_exact_lhs(lower, lf_ref[...], terms=2)
    b_ref[...] = b
    row = lax.broadcasted_iota(jnp.int32, b.shape, 0)

    def ref_rows(m):
        parts = [jnp.broadcast_to(b_ref[g0 + m - 1:g0 + m, :], (2 * m, b.shape[1]))
                 for g0 in range(0, c, 2 * m)]
        return jnp.concatenate(parts, axis=0) if len(parts) > 1 else parts[0]

    def level_decay(m):
        if m == 1:
            return jnp.where(row % 2 == 1, f, 1.0)
        if m == 2:
            off = row % 4
            nxt, prv = pltpu.roll(f, c - 1, axis=0), pltpu.roll(f, 1, axis=0)
            return jnp.where(off == 0, nxt, jnp.where(off == 1, 1.0, jnp.where(off == 2, f, f * prv)))
        return jnp.exp2(jnp.abs(b - ref_rows(m)) * (-LOG2E))

    lvl = lvl_ref[...]
    qb, kb = q.astype(BF16), key.astype(BF16)
    scores = [jnp.zeros((c, c), F32) for _ in range(HGRN_HEADS)]
    for idx, m in enumerate([0] + _hgrn_levels(c)[::-1]):
        if m == 0:
            z, y = qb, kb
        else:
            e = level_decay(m).astype(BF16)
            z, y = qb * e, kb * e
        mask = lvl == idx
        for hd in range(HGRN_HEADS):
            sl = slice(hd * HGRN_DIM, (hd + 1) * HGRN_DIM)
            scores[hd] = jnp.where(mask, _dot_nt(z[:, sl], y[:, sl]), scores[hd])

    b_last = b[c - 1:c, :]
    q_in = (q * jnp.exp(b)).astype(BF16)
    k_out = (key * jnp.exp(b_last - b)).astype(BF16)
    v = i_ref[...].astype(BF16)
    gate = gate_ref[...]
    og = og_ref[...]
    decay = jnp.exp(b_last)
    for hd in range(HGRN_HEADS):
        sl = slice(hd * HGRN_DIM, (hd + 1) * HGRN_DIM)
        st = st_ref[hd]
        o = _dot(scores[hd].astype(BF16), v[:, sl]) + _dot_nt(q_in[:, sl], st.astype(BF16))
        st_ref[hd] = st * decay[:, sl] + _dot_tn(v[:, sl], k_out[:, sl])
        o_ref[:, sl] = (_rms_norm(o, og[:, sl]) * gate[:, sl]).astype(o_ref.dtype)


def _hgrn(hg, og):
    b, l, _ = hg.shape
    c = HGRN_CHUNK
    tl = HGRN_CHUNK * HGRN_CHUNKS_PER_STEP
    assert l % tl == 0
    w = HGRN_WIDTH
    part = lambda p: pl.BlockSpec((None, tl, w), lambda bi, i, p=p: (bi, i, p))
    est = 2 * 6 * tl * w * 4 + 16 * tl * w * 4 + HGRN_HEADS * HGRN_DIM * HGRN_DIM * 4
    return pl.pallas_call(
        _hgrn_kernel,
        grid=(b, l // tl),
        in_specs=[part(0), part(1), part(2), part(3), part(4),
                  pl.BlockSpec((1, w), lambda bi, i: (0, 0)),
                  pl.BlockSpec((c, c), lambda bi, i: (0, 0))],
        out_specs=pl.BlockSpec((None, tl, w), lambda bi, i: (bi, i, 0)),
        out_shape=jax.ShapeDtypeStruct((b, l, w), BF16),
        scratch_shapes=[pltpu.VMEM((HGRN_HEADS, HGRN_DIM, HGRN_DIM), F32),
                        pltpu.VMEM((tl, w), F32)],
        compiler_params=pltpu.CompilerParams(
            dimension_semantics=("arbitrary", "arbitrary"), vmem_limit_bytes=_vmem_limit(est)),
        name="hgrn2",
    )(hg, hg, hg, hg, hg, og, _hgrn_level_ids(c))


def _s5_kernel(u_ref, kd_ref, pin_ref, pout_ref, ar_ref, ai_ref, y_ref, xr_ref, xi_ref):
    nc, nb, cs, w = u_ref.shape
    ns = ar_ref.shape[1]

    @pl.when(pl.program_id(0) == 0)
    def _():
        xr_ref[...] = jnp.zeros_like(xr_ref)
        xi_ref[...] = jnp.zeros_like(xi_ref)

    us = [u_ref[:, :, s, :].reshape(nc * nb, w).astype(BF16) for s in range(cs)]
    win = _dot(us[0], pin_ref[0])
    for s in range(1, cs):
        win = win + _dot(us[s], pin_ref[s])

    ar, ai = ar_ref[...], ai_ref[...]
    xr, xi = xr_ref[...], xi_ref[...]
    prev_r, prev_i = [], []
    for c in range(nc):
        prev_r.append(xr)
        prev_i.append(xi)
        wr = win[c * nb:(c + 1) * nb, :ns]
        wi = win[c * nb:(c + 1) * nb, ns:]
        xr, xi = ar * xr - ai * xi + wr, ar * xi + ai * xr + wi
    xr_ref[...] = xr
    xi_ref[...] = xi
    xp = jnp.concatenate([jnp.concatenate(prev_r, axis=0), jnp.concatenate(prev_i, axis=0)],
                         axis=1).astype(BF16)

    for t in range(cs):
        lags = jnp.concatenate([us[t - d] for d in range(t + 1)], axis=1) if t else us[0]
        yt = _dot(xp, pout_ref[t]) + _dot(lags, kd_ref[0:(t + 1) * w, :])
        y_ref[:, :, t, :] = yt.reshape(nc, nb, w)


def _s5(u4, kd, pin, pout, ar, ai):
    nchunk, nb, cs, w = u4.shape
    nc = min(S5_TILE // cs, nchunk)
    const = dict(pipeline_mode=pl.Buffered(1))
    full = lambda shape: pl.BlockSpec(shape, lambda i: tuple(0 for _ in shape), **const)
    rows = nc * cs * nb
    chunked = pl.BlockSpec((nc, nb, cs, w), lambda i: (i, 0, 0, 0))
    est = (4 * rows * w * 4 + (kd.size + pin.size + pout.size) * 2 + rows // cs * 2 * S5_NSTATE * (4 + 4 + 2)
           + rows * w * (2 + 2 + 4))
    return pl.pallas_call(
        _s5_kernel,
        grid=(nchunk // nc,),
        in_specs=[chunked, full(kd.shape), full(pin.shape), full(pout.shape), full(ar.shape), full(ai.shape)],
        out_specs=chunked,
        out_shape=jax.ShapeDtypeStruct(u4.shape, F32),
        scratch_shapes=[pltpu.VMEM((nb, S5_NSTATE), F32), pltpu.VMEM((nb, S5_NSTATE), F32)],
        compiler_params=pltpu.CompilerParams(
            dimension_semantics=("arbitrary",), vmem_limit_bytes=_vmem_limit(est)),
        name="s5",
    )(u4, kd, pin, pout, ar, ai)


def _s5_operators(lam_re, lam_im, log_step, b_re, b_im, c_re, c_im):
    lr, li = lam_re.astype(F32), lam_im.astype(F32)
    dt = jnp.exp(log_step.astype(F32))[:, None]
    mag = jnp.exp(lr * dt)
    lbar_re, lbar_im = mag * jnp.cos(li * dt), mag * jnp.sin(li * dt)
    nr, ni = lbar_re - 1.0, lbar_im
    den = lr * lr + li * li
    coef_re = (nr * lr + ni * li) / den
    coef_im = (ni * lr - nr * li) / den
    br, bi = b_re.astype(F32), b_im.astype(F32)
    bbar_re = coef_re[..., None] * br - coef_im[..., None] * bi
    bbar_im = coef_re[..., None] * bi + coef_im[..., None] * br
    cr, ci = c_re.astype(F32), c_im.astype(F32)

    pr, pi = [jnp.ones_like(lbar_re)], [jnp.zeros_like(lbar_re)]
    for _ in range(S5_CHUNK):
        pr, pi = (pr + [pr[-1] * lbar_re - pi[-1] * lbar_im],
                  pi + [pr[-1] * lbar_im + pi[-1] * lbar_re])
    pw_re, pw_im = jnp.stack(pr), jnp.stack(pi)
    cs = S5_CHUNK

    def block_diag(compact, rows_group, cols_group, reps):
        tiled = jnp.tile(compact, reps)
        r = lax.broadcasted_iota(jnp.int32, tiled.shape, tiled.ndim - 2)
        c = lax.broadcasted_iota(jnp.int32, tiled.shape, tiled.ndim - 1)
        return jnp.where(rows_group(r) == cols_group(c), tiled, 0.0).astype(BF16)

    lane_group = lambda c: c // S5_GROUP
    state_group = lambda c: (c % S5_NSTATE) // S5_STATE

    ab_re = pw_re[:cs, :, :, None] * bbar_re - pw_im[:cs, :, :, None] * bbar_im
    ab_im = pw_re[:cs, :, :, None] * bbar_im + pw_im[:cs, :, :, None] * bbar_re
    taps = jnp.sum(cr[None, :, :, :, None] * ab_re[:, :, None] - ci[None, :, :, :, None] * ab_im[:, :, None],
                   axis=3)
    kd = block_diag(taps.transpose(0, 1, 3, 2).reshape(cs * S5_WIDTH, S5_GROUP),
                    lambda r: (r // S5_GROUP) % S5_GROUPS, lane_group, (1, S5_GROUPS))

    rev = cs - 1 - jnp.arange(cs)
    ab = jnp.stack([ab_re[rev], ab_im[rev]], axis=1)
    pin = block_diag(ab.transpose(0, 4, 1, 2, 3).reshape(cs, S5_GROUP, 2 * S5_NSTATE),
                     lane_group, state_group, (1, S5_GROUPS, 1))

    qr, qi = pw_re[1:], pw_im[1:]
    po_re = cr[None] * qr[:, :, None, :] - ci[None] * qi[:, :, None, :]
    po_im = -(cr[None] * qi[:, :, None, :] + ci[None] * qr[:, :, None, :])
    po = jnp.stack([po_re, po_im], axis=1)
    pout = block_diag(po.transpose(0, 1, 2, 4, 3).reshape(cs, 2 * S5_NSTATE, S5_GROUP),
                      state_group, lane_group, (1, 1, S5_GROUPS))
    a_re = pw_re[cs].reshape(1, S5_NSTATE)
    a_im = pw_im[cs].reshape(1, S5_NSTATE)
    return kd, pin, pout, a_re, a_im


def _out_ffn_kernel(x_ref, oa_ref, ob_ref, y_ref, u_ref, d_ref, gw_ref, gb_ref, wa_ref, wb_ref, wc_ref,
                    g_ref, wg_ref, wu_ref, wd_ref, o_ref):
    tm = x_ref.shape[0]
    y = y_ref[...].reshape(tm, S5_WIDTH) + d_ref[...] * u_ref[...].reshape(tm, S5_WIDTH)
    z = 0.5 * y * (1.0 + jnp.tanh(math.sqrt(2.0 / math.pi) * (y + 0.044715 * (y * y * y))))
    oc = z * _sigmoid(_dot(z.astype(BF16), gw_ref[...]) + gb_ref[...])
    x = (x_ref[...] + _dot(oa_ref[...], wa_ref[...]) + _dot(ob_ref[...], wb_ref[...])
         + _dot(oc.astype(BF16), wc_ref[...]))
    o_ref[...] = _swiglu_residual(x, g_ref[...], wg_ref, wu_ref, wd_ref)


def _out_ffn(x, oa, ob, y_lb, u_lb, dvec, gw, gb, wa, wb, wc, gain, wg, wu, wd, layer):
    b, l, d = x.shape
    f = wg.shape[2]
    stacked = lambda shape: pl.BlockSpec((None,) + shape[1:], lambda bi, i: (layer, 0, 0),
                                         pipeline_mode=pl.Buffered(1))
    tm = min(OUT_ROWS, l)
    rows = lambda w: pl.BlockSpec((None, tm, w), lambda bi, i: (bi, i, 0))
    token_major = pl.BlockSpec((tm // S5_CHUNK, None, S5_CHUNK, S5_WIDTH), lambda bi, i: (i, bi, 0, 0))
    full = lambda shape: pl.BlockSpec(shape, lambda bi, i: tuple(0 for _ in shape),
                                      pipeline_mode=pl.Buffered(1))
    est = (2 * (2 * tm * d * 4 + tm * (256 * 2 + 512 * 2 + 256 * 8)) + 2 * (d * d + 256 * 256) + 3 * d * f * 2
           + tm * f * (4 + 4 + 2 + 4) + 4 * tm * d * 4)
    return pl.pallas_call(
        _out_ffn_kernel,
        grid=(b, l // tm),
        in_specs=[rows(d), rows(FOX_WIDTH), rows(HGRN_WIDTH), token_major, token_major,
                  full((1, S5_WIDTH)), full(gw.shape), full((1, S5_WIDTH)),
                  full(wa.shape), full(wb.shape), full(wc.shape),
                  full((1, d)), stacked(wg.shape), stacked(wu.shape), stacked(wd.shape)],
        out_specs=rows(d),
        out_shape=jax.ShapeDtypeStruct((b, l, d), F32),
        compiler_params=pltpu.CompilerParams(
            dimension_semantics=("arbitrary", "arbitrary"), vmem_limit_bytes=_vmem_limit(est)),
        name="out_ffn",
    )(x, oa, ob, y_lb, u_lb, dvec, gw, gb, wa, wb, wc, gain, wg, wu, wd)


def kernel(x, ffn1_norm, ffn1_w_gate, ffn1_w_up, ffn1_w_down, mix_norm, w_in, fox_f_bias, fox_q_gain, fox_k_gain, hgrn_lb, hgrn_o_gain, s5_lambda_re, s5_lambda_im, s5_log_step, s5_B_re, s5_B_im, s5_C_re, s5_C_im, s5_D, s5_glu_w, s5_glu_b, w_out, ffn2_norm, ffn2_w_gate, ffn2_w_up, ffn2_w_down):
    bsz, seq, d = x.shape
    depth = w_in.shape[0]
    t = bsz * seq

    lb_all = jnp.cumsum(jax.nn.softmax(hgrn_lb.astype(F32), axis=0), axis=0)
    lb_all = lb_all - lb_all[0:1]

    o_q, o_k, o_v = 0, FOX_WIDTH, 2 * FOX_WIDTH
    o_f = 3 * FOX_WIDTH
    o_h = o_f + FOX_HEADS
    o_u = o_h + 4 * HGRN_WIDTH

    ffn1 = [_to_bf16(w) for w in (ffn1_w_gate, ffn1_w_up, ffn1_w_down)]
    ffn2 = [_to_bf16(w) for w in (ffn2_w_gate, ffn2_w_up, ffn2_w_down)]

    for l in range(depth):
        x = _ffn(x.reshape(t, d), ffn1_norm[l][None], *ffn1, layer=l).reshape(bsz, seq, d)

        w = w_in[l]
        wf = jnp.zeros((d, 128), F32).at[:, :3 * FOX_HEADS].set(jnp.tile(w[:, o_f:o_h], (1, 3))).astype(BF16)
        fb = jnp.zeros((1, 128), F32).at[0, :3 * FOX_HEADS].set(jnp.tile(fox_f_bias[l].astype(F32), 3))
        q, k, vt, cs, hg, u_lb = _proj(
            x, mix_norm[l][None], w[:, o_q:o_v].astype(BF16), w[:, o_v:o_f].T.astype(BF16), wf,
            w[:, o_h:o_u].astype(BF16), w[:, o_u:].astype(BF16),
            jnp.tile(fox_q_gain[l].astype(F32), FOX_HEADS)[None],
            jnp.tile(fox_k_gain[l].astype(F32), FOX_HEADS)[None], fb, lb_all[l][None])

        o_a = _attn(q, k, vt, cs, fox_q_gain[l], fox_k_gain[l])
        o_b = _hgrn(hg, hgrn_o_gain[l].astype(F32)[None])
        ops = _s5_operators(s5_lambda_re[l], s5_lambda_im[l], s5_log_step[l],
                            s5_B_re[l], s5_B_im[l], s5_C_re[l], s5_C_im[l])
        y_lb = _s5(u_lb, *ops)

        wo = w_out[l].astype(BF16)
        x = _out_ffn(x, o_a, o_b, y_lb, u_lb, s5_D[l].astype(F32)[None], s5_glu_w[l].astype(BF16),
                     s5_glu_b[l].astype(F32)[None], wo[:FOX_WIDTH], wo[FOX_WIDTH:FOX_WIDTH + HGRN_WIDTH],
                     wo[FOX_WIDTH + HGRN_WIDTH:], ffn2_norm[l][None], *ffn2, layer=l)
    return x
```

```python
import math

import jax
import jax.numpy as jnp
import numpy as np
from jax import lax
from jax.experimental import pallas as pl
from jax.experimental.pallas import tpu as pltpu

F32 = jnp.float32
BF16 = jnp.bfloat16

EPS = 1e-6
FOX_HEADS = 4
FOX_HEAD_DIM = 64
FOX_WIDTH = FOX_HEADS * FOX_HEAD_DIM
HGRN_HEADS = 4
HGRN_DIM = 128
HGRN_WIDTH = HGRN_HEADS * HGRN_DIM
S5_GROUPS = 16
S5_GROUP = 16
S5_STATE = 64
S5_WIDTH = S5_GROUPS * S5_GROUP
S5_NSTATE = S5_GROUPS * S5_STATE
S5_CHUNK = 8
S5_TILE = 128
CAST_ROWS = 256
LOG2E = math.log2(math.e)

V7X_VMEM_BYTES = 64 * 1024 * 1024
V7X_SUBLANES = 8

ATTN_BLOCK = 512
ATTN_KV_BLOCKS = 1
ATTN_MAX_FREE_LOGIT = 40.0
HGRN_CHUNK = 128
HGRN_CHUNKS_PER_STEP = 8
FFN_ROWS = 512
FFN_SPLIT = 4
PROJ_ROWS = 512
PROJ_SPLIT = 2
OUT_ROWS = 512


def _vmem_limit(estimate_bytes):
    return int(min(estimate_bytes * 5 // 4 + (4 << 20), V7X_VMEM_BYTES - (4 << 20)))


def _rms_norm(x, gain):
    ms = jnp.mean(x * x, axis=-1, keepdims=True)
    return x * lax.rsqrt(ms + EPS) * gain


def _dot(a, b):
    return jnp.dot(a, b, preferred_element_type=F32)


def _dot_nt(a, b):
    return lax.dot_general(a, b, (((1,), (1,)), ((), ())), preferred_element_type=F32)


def _dot_tn(a, b):
    return lax.dot_general(a, b, (((0,), (0,)), ((), ())), preferred_element_type=F32)


def _split3(a):
    a1 = a.astype(BF16)
    r1 = a - a1.astype(F32)
    a2 = r1.astype(BF16)
    a3 = (r1 - a2.astype(F32)).astype(BF16)
    return a1, a2, a3


def _dot_exact_rhs(a, rhs, terms=3):
    parts = _split3(a)[:terms]
    out = _dot(parts[0], rhs)
    for part in parts[1:]:
        out = out + _dot(part, rhs)
    return out


def _dot_exact_lhs(lhs, a, terms=3):
    parts = _split3(a)[:terms]
    out = _dot(lhs, parts[0])
    for part in parts[1:]:
        out = out + _dot(lhs, part)
    return out


def _sigmoid(x):
    return 1.0 / (1.0 + jnp.exp(-x))


def _cumsum_rows(a):
    row = lax.broadcasted_iota(jnp.int32, a.shape, 0)
    shift = 1
    while shift < a.shape[0]:
        a = a + jnp.where(row >= shift, pltpu.roll(a, shift, axis=0), 0.0)
        shift *= 2
    return a


def _cast_kernel(w_ref, o_ref):
    o_ref[...] = w_ref[...].astype(o_ref.dtype)


def _to_bf16(w):
    depth, rows, cols = w.shape
    tr = CAST_ROWS
    assert rows % tr == 0
    spec = pl.BlockSpec((None, tr, cols), lambda l, i: (l, i, 0))
    return pl.pallas_call(
        _cast_kernel,
        grid=(depth, rows // tr),
        in_specs=[spec],
        out_specs=spec,
        out_shape=jax.ShapeDtypeStruct(w.shape, BF16),
        compiler_params=pltpu.CompilerParams(
            dimension_semantics=("arbitrary", "arbitrary"),
            vmem_limit_bytes=_vmem_limit(2 * tr * cols * (4 + 2))),
        name="cast_bf16",
    )(w)


def _swiglu_residual(x, gain, wg_ref, wu_ref, wd_ref):
    h = _rms_norm(x, gain).astype(BF16)
    gate = _dot(h, wg_ref[...])
    up = _dot(h, wu_ref[...])
    act = (gate * _sigmoid(gate) * up).astype(BF16)
    return x + 0.5 * _dot(act, wd_ref[...])


def _ffn_kernel(x_ref, g_ref, wg_ref, wu_ref, wd_ref, o_ref):
    part_rows = x_ref.shape[0] // FFN_SPLIT
    for part in range(FFN_SPLIT):
        rows = slice(part * part_rows, (part + 1) * part_rows)
        o_ref[rows, :] = _swiglu_residual(x_ref[rows, :], g_ref[...], wg_ref, wu_ref, wd_ref)


def _ffn(x2d, gain, wg, wu, wd, layer):
    t, d = x2d.shape
    f = wg.shape[2]
    tm = min(FFN_ROWS, t)
    const = dict(pipeline_mode=pl.Buffered(1))
    est = 3 * d * f * 2 + 4 * tm * d * 4 + tm * f * (4 + 4 + 2 + 4) + tm * d * 4
    return pl.pallas_call(
        _ffn_kernel,
        grid=(t // tm,),
        in_specs=[
            pl.BlockSpec((tm, d), lambda i: (i, 0)),
            pl.BlockSpec((1, d), lambda i: (0, 0)),
            pl.BlockSpec((None, d, f), lambda i: (layer, 0, 0), **const),
            pl.BlockSpec((None, d, f), lambda i: (layer, 0, 0), **const),
            pl.BlockSpec((None, f, d), lambda i: (layer, 0, 0), **const),
        ],
        out_specs=pl.BlockSpec((tm, d), lambda i: (i, 0)),
        out_shape=jax.ShapeDtypeStruct((t, d), F32),
        compiler_params=pltpu.CompilerParams(
            dimension_semantics=("arbitrary",), vmem_limit_bytes=_vmem_limit(est)),
        name="ffn",
    )(x2d, gain, wg, wu, wd)


def _proj_kernel(x_ref, g_ref, wqk_ref, wvt_ref, wf_ref, wh_ref, wu_ref, qg_ref, kg_ref,
                 fb_ref, lb_ref, q_ref, k_ref, vt_ref, cs_ref, hg_ref, u_ref, carry_ref):
    @pl.when(pl.program_id(1) == 0)
    def _():
        carry_ref[...] = jnp.zeros_like(carry_ref)

    tk = vt_ref.shape[2]
    pr = x_ref.shape[0] // PROJ_SPLIT
    assert tk % pr == 0
    for part in range(PROJ_SPLIT):
        rows = pl.ds(part * pr, pr)
        blk, off = divmod(part * pr, tk)
        _proj_rows(x_ref.at[rows], g_ref, wqk_ref, wvt_ref, wf_ref, wh_ref, wu_ref, qg_ref, kg_ref, fb_ref,
                   lb_ref, q_ref.at[rows], k_ref.at[rows], vt_ref.at[blk, :, pl.ds(off, pr)], cs_ref.at[rows],
                   hg_ref.at[rows], u_ref.at[pl.ds(part * pr // S5_CHUNK, pr // S5_CHUNK)], carry_ref)


def _proj_rows(x_ref, g_ref, wqk_ref, wvt_ref, wf_ref, wh_ref, wu_ref, qg_ref, kg_ref,
               fb_ref, lb_ref, q_ref, k_ref, vt_ref, cs_ref, hg_ref, u_ref, carry_ref):
    tm = x_ref.shape[0]
    h = _rms_norm(x_ref[...], g_ref[...]).astype(BF16)

    li = lax.broadcasted_iota(jnp.int32, (FOX_WIDTH, FOX_WIDTH), 0) // FOX_HEAD_DIM
    lj = lax.broadcasted_iota(jnp.int32, (FOX_WIDTH, FOX_WIDTH), 1) // FOX_HEAD_DIM
    group = (li == lj).astype(BF16)

    def head_norm(a, gain):
        ms = _dot_exact_rhs(a * a, group, terms=1) * (1.0 / FOX_HEAD_DIM)
        return a * lax.rsqrt(ms + EPS) * gain

    qk = _dot(h, wqk_ref[...])
    q_ref[...] = (head_norm(qk[:, :FOX_WIDTH], qg_ref[...]) * (LOG2E / math.sqrt(FOX_HEAD_DIM))).astype(BF16)
    k_ref[...] = head_norm(qk[:, FOX_WIDTH:], kg_ref[...]).astype(BF16)
    vt_ref[...] = _dot_nt(wvt_ref[...], h).astype(BF16)

    fl = _dot(h, wf_ref[...]) + fb_ref[...]
    ls = jnp.minimum(fl, 0.0) - jnp.log1p(jnp.exp(-jnp.abs(fl)))
    c = _cumsum_rows(ls) + carry_ref[0:1, :]
    carry_ref[...] = jnp.broadcast_to(c[tm - 1:tm, :], carry_ref.shape)
    c1, c2, c3 = _split3(c * LOG2E)
    lane = lax.broadcasted_iota(jnp.int32, c.shape, 1)
    cs_ref[...] = jnp.where(lane < FOX_HEADS, c1, jnp.where(lane < 2 * FOX_HEADS, c2, c3))

    hg = _dot(h, wh_ref[...])
    w = HGRN_WIDTH
    lb = lb_ref[...]
    f = lb + (1.0 - lb) * _sigmoid(hg[:, w:2 * w])
    g = hg[:, 3 * w:4 * w]
    hg_ref[:, 0:w] = hg[:, 0:w]
    hg_ref[:, w:2 * w] = jnp.log(f)
    hg_ref[:, 2 * w:3 * w] = f
    hg_ref[:, 3 * w:4 * w] = hg[:, 2 * w:3 * w]
    hg_ref[:, 4 * w:5 * w] = g * (0.5 * (jnp.tanh(0.5 * g) + 1.0))
    u_ref[...] = _dot(h, wu_ref[...]).reshape(u_ref.shape)


def _proj(x, gain, wqk, wvt, wf, wh, wu, qg, kg, fb, lb):
    b, l, d = x.shape
    tm = min(PROJ_ROWS, l)
    tk = min(ATTN_BLOCK, l)
    hw = 5 * HGRN_WIDTH
    full = lambda shape: pl.BlockSpec(shape, lambda bi, i: tuple(0 for _ in shape))
    est = 2 * (d * (512 + 256 + 128 + hw + 256) * 2) + 2 * tm * d * 4 + 2 * tm * (hw + 256 + 512) * 4 + 2 * tm * hw * 4
    return pl.pallas_call(
        _proj_kernel,
        grid=(b, l // tm),
        in_specs=[
            pl.BlockSpec((None, tm, d), lambda bi, i: (bi, i, 0)),
            full((1, d)), full(wqk.shape), full(wvt.shape), full(wf.shape), full(wh.shape),
            full(wu.shape), full((1, FOX_WIDTH)), full((1, FOX_WIDTH)), full((1, 128)),
            full((1, HGRN_WIDTH)),
        ],
        out_specs=[
            pl.BlockSpec((None, tm, FOX_WIDTH), lambda bi, i: (bi, i, 0)),
            pl.BlockSpec((None, tm, FOX_WIDTH), lambda bi, i: (bi, i, 0)),
            pl.BlockSpec((None, tm // tk, FOX_WIDTH, tk), lambda bi, i: (bi, i, 0, 0)),
            pl.BlockSpec((None, tm, 128), lambda bi, i: (bi, i, 0)),
            pl.BlockSpec((None, tm, hw), lambda bi, i: (bi, i, 0)),
            pl.BlockSpec((tm // S5_CHUNK, None, S5_CHUNK, S5_WIDTH), lambda bi, i: (i, bi, 0, 0)),
        ],
        out_shape=[
            jax.ShapeDtypeStruct((b, l, FOX_WIDTH), BF16),
            jax.ShapeDtypeStruct((b, l, FOX_WIDTH), BF16),
            jax.ShapeDtypeStruct((b, l // tk, FOX_WIDTH, tk), BF16),
            jax.ShapeDtypeStruct((b, l, 128), BF16),
            jax.ShapeDtypeStruct((b, l, hw), F32),
            jax.ShapeDtypeStruct((l // S5_CHUNK, b, S5_CHUNK, S5_WIDTH), F32),
        ],
        scratch_shapes=[pltpu.VMEM((V7X_SUBLANES, 128), F32)],
        compiler_params=pltpu.CompilerParams(
            dimension_semantics=("arbitrary", "arbitrary"), vmem_limit_bytes=_vmem_limit(est)),
        name="proj",
    )(x, gain, wqk, wvt, wf, wh, wu, qg, kg, fb, lb)


def _attn_kernel(bounded_ref, q_ref, k_ref, vt_ref, cs_ref, e_ref, pq_ref, pk_ref, oq_ref, ok_ref,
                 o_ref, ka_ref, qa_ref, m_ref, l_ref, acc_ref, s_ref, p_ref):
    tq = o_ref.shape[0]
    i = pl.program_id(1)

    @pl.when(i == 0)
    def _():
        cs = cs_ref[...]
        ka_ref[...] = (_dot(k_ref[...], e_ref[...]) + _dot(cs, pk_ref[...]) + ok_ref[...]).astype(BF16)
        qa_ref[...] = (_dot(q_ref[...], e_ref[...]) + _dot(cs, pq_ref[...]) + oq_ref[...]).astype(BF16)

    l_ref[...] = jnp.zeros_like(l_ref)
    acc_ref[...] = jnp.zeros_like(acc_ref)
    nblk = ATTN_KV_BLOCKS
    tk = nblk * tq
    nfull = i // nblk
    qrows = pl.ds(pl.multiple_of(i * tq, tq), tq)
    key_minus_qry = (lax.broadcasted_iota(jnp.int32, (tk, tq), 0)
                     - lax.broadcasted_iota(jnp.int32, (tk, tq), 1))

    def causal(s, jj):
        return jnp.where(key_minus_qry <= (i - jj * nblk) * tq, s, -jnp.inf)

    def pv_dot(jj, hd, p):
        head_rows = slice(hd * FOX_HEAD_DIM, (hd + 1) * FOX_HEAD_DIM)
        pv = _dot(vt_ref[jj * nblk, head_rows, :], p[0:tq, :])
        for bk in range(1, nblk):
            pv = pv + _dot(vt_ref[jj * nblk + bk, head_rows, :], p[bk * tq:(bk + 1) * tq, :])
        return pv

    def bounded_steps(steps):
        for slot, jj in enumerate(steps):
            rows = pl.ds(pl.multiple_of(jj * tk, tk), tk)
            for hd in range(FOX_HEADS):
                slab = slice(hd * 128, (hd + 1) * 128)
                p = jnp.exp2(causal(_dot_nt(ka_ref[rows, slab], qa_ref[qrows, slab]), jj))
                l_ref[hd] = l_ref[hd] + jnp.sum(p, axis=0, keepdims=True)
                p_ref[slot, hd] = p.astype(BF16)
        for hd in range(FOX_HEADS):
            pv = pv_dot(steps[0], hd, p_ref[0, hd])
            for slot, jj in enumerate(steps[1:], 1):
                pv = pv + pv_dot(jj, hd, p_ref[slot, hd])
            acc_ref[hd] = acc_ref[hd] + pv

    @pl.when(bounded_ref[0] == 1)
    def _():
        def pair(kk, carry):
            bounded_steps([2 * kk, 2 * kk + 1])
            return carry

        lax.fori_loop(0, (nfull + 1) // 2, pair, 0)

        @pl.when(nfull % 2 == 0)
        def _():
            bounded_steps([nfull])

    @pl.when(bounded_ref[0] == 0)
    def _():
        _attn_online(i, nfull, tk, qrows, causal, pv_dot, ka_ref, qa_ref, m_ref, l_ref, acc_ref,
                     s_ref, p_ref.at[0])

    outs = [acc_ref[hd] / l_ref[hd] for hd in range(FOX_HEADS)]
    o_ref[...] = jnp.concatenate(outs, axis=0).T.astype(o_ref.dtype)


def _attn_online(i, nfull, tk, qrows, causal, pv_dot, ka_ref, qa_ref, m_ref, l_ref, acc_ref, s_ref, p_ref):
    m_ref[...] = jnp.full(m_ref.shape, -jnp.inf, F32)

    def logits(jj, slot):
        rows = pl.ds(pl.multiple_of(jj * tk, tk), tk)
        for hd in range(FOX_HEADS):
            slab = slice(hd * 128, (hd + 1) * 128)
            s_ref[slot, hd] = _dot_nt(ka_ref[rows, slab], qa_ref[qrows, slab])

    def update(jj, slot, last):
        alphas = []
        for hd in range(FOX_HEADS):
            s = s_ref[slot, hd]
            if last:
                s = causal(s, jj)
            m_old = m_ref[hd]
            m_new = jnp.maximum(m_old, jnp.max(s, axis=0, keepdims=True))
            alpha = jnp.exp2(m_old - m_new)
            p = jnp.exp2(s - m_new)
            m_ref[hd] = m_new
            l_ref[hd] = alpha * l_ref[hd] + jnp.sum(p, axis=0, keepdims=True)
            p_ref[hd] = p.astype(BF16)
            alphas.append(alpha)
        for hd in range(FOX_HEADS):
            acc_ref[hd] = alphas[hd] * acc_ref[hd] + pv_dot(jj, hd, p_ref[hd])

    def pair(kk, carry):
        logits(2 * kk + 1, 1)
        update(2 * kk, 0, False)
        logits(2 * kk + 2, 0)
        update(2 * kk + 1, 1, False)
        return carry

    logits(0, 0)
    npair = nfull // 2
    lax.fori_loop(0, npair, pair, 0)

    @pl.when(nfull % 2 == 0)
    def _():
        update(2 * npair, 0, True)

    @pl.when(nfull % 2 == 1)
    def _():
        logits(2 * npair + 1, 1)
        update(2 * npair, 0, False)
        update(2 * npair + 1, 1, True)


def _attn_placement():
    w = FOX_HEADS * 128
    e = np.zeros((FOX_WIDTH, w), np.float32)
    cols = np.arange(FOX_WIDTH)
    e[cols, (cols // FOX_HEAD_DIM) * 128 + cols % FOX_HEAD_DIM] = 1.0
    pq, pk = np.zeros((128, w), np.float32), np.zeros((128, w), np.float32)
    oq, ok = np.zeros((1, w), np.float32), np.zeros((1, w), np.float32)
    for hd in range(FOX_HEADS):
        base = hd * 128 + FOX_HEAD_DIM
        for term in range(3):
            pk[term * FOX_HEADS + hd, base + term] = -1.0
            oq[0, base + term] = 1.0
            pq[term * FOX_HEADS + hd, base + 3 + term] = 1.0
            ok[0, base + 3 + term] = 1.0
    return (jnp.asarray(e, BF16), jnp.asarray(pq, BF16), jnp.asarray(pk, BF16),
            jnp.asarray(oq), jnp.asarray(ok))


def _attn(q, k, vt, cs, q_gain, k_gain):
    b, l, w = q.shape
    nkv, tk = vt.shape[1], vt.shape[3]
    tq = tk
    wa = FOX_HEADS * 128
    e, pq, pk, oq, ok = _attn_placement()
    full = lambda shape: pl.BlockSpec(shape, lambda bi, i: tuple(0 for _ in shape))
    seq = lambda width: pl.BlockSpec((None, l, width), lambda bi, i: (bi, 0, 0))
    assert (l // tq) % ATTN_KV_BLOCKS == 0
    logit_bound = (1.01 * FOX_HEAD_DIM * LOG2E / math.sqrt(FOX_HEAD_DIM)
                   * jnp.max(jnp.abs(q_gain.astype(F32))) * jnp.max(jnp.abs(k_gain.astype(F32))))
    bounded = (logit_bound < ATTN_MAX_FREE_LOGIT).astype(jnp.int32).reshape(1)
    est = (2 * (l * w * 2 * 2 + l * 128 * 2) + 2 * l * wa * (2 + 4 + 4)
           + FOX_HEADS * ATTN_KV_BLOCKS * tq * tq * (2 * 4 + 2 * 2 + 2 * 4) + FOX_HEADS * 128 * tq * 4)
    return pl.pallas_call(
        _attn_kernel,
        grid=(b, l // tq),
        in_specs=[
            pl.BlockSpec(memory_space=pltpu.SMEM),
            seq(w), seq(w),
            pl.BlockSpec((None, nkv, w, tk), lambda bi, i: (bi, 0, 0, 0)),
            seq(128),
            full(e.shape), full(pq.shape), full(pk.shape), full(oq.shape), full(ok.shape),
        ],
        out_specs=pl.BlockSpec((None, tq, w), lambda bi, i: (bi, i, 0)),
        out_shape=jax.ShapeDtypeStruct((b, l, w), BF16),
        scratch_shapes=[pltpu.VMEM((l, wa), BF16), pltpu.VMEM((l, wa), BF16),
                        pltpu.VMEM((FOX_HEADS, 1, tq), F32), pltpu.VMEM((FOX_HEADS, 1, tq), F32),
                        pltpu.VMEM((FOX_HEADS, FOX_HEAD_DIM, tq), F32),
                        pltpu.VMEM((2, FOX_HEADS, ATTN_KV_BLOCKS * tq, tq), F32),
                        pltpu.VMEM((2, FOX_HEADS, ATTN_KV_BLOCKS * tq, tq), BF16)],
        compiler_params=pltpu.CompilerParams(
            dimension_semantics=("arbitrary", "arbitrary"), vmem_limit_bytes=_vmem_limit(est)),
        name="fox_attn",
    )(bounded, q, k, vt, cs, e, pq, pk, oq, ok)


def _hgrn_levels(c):
    return [c >> (s + 1) for s in range(int(math.log2(c)))]


def _hgrn_level_ids(c):
    t = np.arange(c)[:, None]
    s = np.arange(c)[None, :]
    lvl = np.floor(np.log2(np.maximum(t ^ s, 1))).astype(np.int32) + 1
    return jnp.asarray(np.where(s > t, -1, np.where(s == t, 0, lvl)).astype(np.int32))


def _hgrn_kernel(q_ref, lf_ref, f_ref, i_ref, gate_ref, og_ref, lvl_ref, o_ref, st_ref, b_ref):
    @pl.when(pl.program_id(1) == 0)
    def _():
        st_ref[...] = jnp.zeros_like(st_ref)

    for sub in range(q_ref.shape[0] // HGRN_CHUNK):
        rows = pl.ds(sub * HGRN_CHUNK, HGRN_CHUNK)
        _hgrn_chunk(q_ref.at[rows], lf_ref.at[rows], f_ref.at[rows], i_ref.at[rows], gate_ref.at[rows],
                    og_ref, lvl_ref, o_ref.at[rows], st_ref, b_ref.at[rows])


def _hgrn_chunk(q_ref, lf_ref, f_ref, i_ref, gate_ref, og_ref, lvl_ref, o_ref, st_ref, b_ref):
    c = q_ref.shape[0]
    f = f_ref[...]
    key = 1.0 - f
    q = q_ref[...]
    ti = lax.broadcasted_iota(jnp.int32, (c, c), 0)
    si = lax.broadcasted_iota(jnp.int32, (c, c), 1)
    lower = (si <= ti).astype(BF16)
    b = _dot_exact_lhs(lower, lf_ref[...], terms=2)
    b_ref[...] = b
    row = lax.broadcasted_iota(jnp.int32, b.shape, 0)

    def ref_rows(m):
        parts = [jnp.broadcast_to(b_ref[g0 + m - 1:g0 + m, :], (2 * m, b.shape[1]))
                 for g0 in range(0, c, 2 * m)]
        return jnp.concatenate(parts, axis=0) if len(parts) > 1 else parts[0]

    def level_decay(m):
        if m == 1:
            return jnp.where(row % 2 == 1, f, 1.0)
        if m == 2:
            off = row % 4
            nxt, prv = pltpu.roll(f, c - 1, axis=0), pltpu.roll(f, 1, axis=0)
            return jnp.where(off == 0, nxt, jnp.where(off == 1, 1.0, jnp.where(off == 2, f, f * prv)))
        return jnp.exp2(jnp.abs(b - ref_rows(m)) * (-LOG2E))

    lvl = lvl_ref[...]
    qb, kb = q.astype(BF16), key.astype(BF16)
    scores = [jnp.zeros((c, c), F32) for _ in range(HGRN_HEADS)]
    for idx, m in enumerate([0] + _hgrn_levels(c)[::-1]):
        if m == 0:
            z, y = qb, kb
        else:
            e = level_decay(m).astype(BF16)
            z, y = qb * e, kb * e
        mask = lvl == idx
        for hd in range(HGRN_HEADS):
            sl = slice(hd * HGRN_DIM, (hd + 1) * HGRN_DIM)
            scores[hd] = jnp.where(mask, _dot_nt(z[:, sl], y[:, sl]), scores[hd])

    b_last = b[c - 1:c, :]
    q_in = (q * jnp.exp(b)).astype(BF16)
    k_out = (key * jnp.exp(b_last - b)).astype(BF16)
    v = i_ref[...].astype(BF16)
    gate = gate_ref[...]
    og = og_ref[...]
    decay = jnp.exp(b_last)
    for hd in range(HGRN_HEADS):
        sl = slice(hd * HGRN_DIM, (hd + 1) * HGRN_DIM)
        st = st_ref[hd]
        o = _dot(scores[hd].astype(BF16), v[:, sl]) + _dot_nt(q_in[:, sl], st.astype(BF16))
        st_ref[hd] = st * decay[:, sl] + _dot_tn(v[:, sl], k_out[:, sl])
        o_ref[:, sl] = (_rms_norm(o, og[:, sl]) * gate[:, sl]).astype(o_ref.dtype)


def _hgrn(hg, og):
    b, l, _ = hg.shape
    c = HGRN_CHUNK
    tl = HGRN_CHUNK * HGRN_CHUNKS_PER_STEP
    assert l % tl == 0
    w = HGRN_WIDTH
    part = lambda p: pl.BlockSpec((None, tl, w), lambda bi, i, p=p: (bi, i, p))
    est = 2 * 6 * tl * w * 4 + 16 * tl * w * 4 + HGRN_HEADS * HGRN_DIM * HGRN_DIM * 4
    return pl.pallas_call(
        _hgrn_kernel,
        grid=(b, l // tl),
        in_specs=[part(0), part(1), part(2), part(3), part(4),
                  pl.BlockSpec((1, w), lambda bi, i: (0, 0)),
                  pl.BlockSpec((c, c), lambda bi, i: (0, 0))],
        out_specs=pl.BlockSpec((None, tl, w), lambda bi, i: (bi, i, 0)),
        out_shape=jax.ShapeDtypeStruct((b, l, w), BF16),
        scratch_shapes=[pltpu.VMEM((HGRN_HEADS, HGRN_DIM, HGRN_DIM), F32),
                        pltpu.VMEM((tl, w), F32)],
        compiler_params=pltpu.CompilerParams(
            dimension_semantics=("arbitrary", "arbitrary"), vmem_limit_bytes=_vmem_limit(est)),
        name="hgrn2",
    )(hg, hg, hg, hg, hg, og, _hgrn_level_ids(c))


def _s5_kernel(u_ref, kd_ref, pin_ref, pout_ref, ar_ref, ai_ref, y_ref, xr_ref, xi_ref):
    nc, nb, cs, w = u_ref.shape
    ns = ar_ref.shape[1]

    @pl.when(pl.program_id(0) == 0)
    def _():
        xr_ref[...] = jnp.zeros_like(xr_ref)
        xi_ref[...] = jnp.zeros_like(xi_ref)

    us = [u_ref[:, :, s, :].reshape(nc * nb, w).astype(BF16) for s in range(cs)]
    win = _dot(us[0], pin_ref[0])
    for s in range(1, cs):
        win = win + _dot(us[s], pin_ref[s])

    ar, ai = ar_ref[...], ai_ref[...]
    xr, xi = xr_ref[...], xi_ref[...]
    prev_r, prev_i = [], []
    for c in range(nc):
        prev_r.append(xr)
        prev_i.append(xi)
        wr = win[c * nb:(c + 1) * nb, :ns]
        wi = win[c * nb:(c + 1) * nb, ns:]
        xr, xi = ar * xr - ai * xi + wr, ar * xi + ai * xr + wi
    xr_ref[...] = xr
    xi_ref[...] = xi
    xp = jnp.concatenate([jnp.concatenate(prev_r, axis=0), jnp.concatenate(prev_i, axis=0)],
                         axis=1).astype(BF16)

    for t in range(cs):
        lags = jnp.concatenate([us[t - d] for d in range(t + 1)], axis=1) if t else us[0]
        yt = _dot(xp, pout_ref[t]) + _dot(lags, kd_ref[0:(t + 1) * w, :])
        y_ref[:, :, t, :] = yt.reshape(nc, nb, w)


def _s5(u4, kd, pin, pout, ar, ai):
    nchunk, nb, cs, w = u4.shape
    nc = min(S5_TILE // cs, nchunk)
    const = dict(pipeline_mode=pl.Buffered(1))
    full = lambda shape: pl.BlockSpec(shape, lambda i: tuple(0 for _ in shape), **const)
    rows = nc * cs * nb
    chunked = pl.BlockSpec((nc, nb, cs, w), lambda i: (i, 0, 0, 0))
    est = (4 * rows * w * 4 + (kd.size + pin.size + pout.size) * 2 + rows // cs * 2 * S5_NSTATE * (4 + 4 + 2)
           + rows * w * (2 + 2 + 4))
    return pl.pallas_call(
        _s5_kernel,
        grid=(nchunk // nc,),
        in_specs=[chunked, full(kd.shape), full(pin.shape), full(pout.shape), full(ar.shape), full(ai.shape)],
        out_specs=chunked,
        out_shape=jax.ShapeDtypeStruct(u4.shape, F32),
        scratch_shapes=[pltpu.VMEM((nb, S5_NSTATE), F32), pltpu.VMEM((nb, S5_NSTATE), F32)],
        compiler_params=pltpu.CompilerParams(
            dimension_semantics=("arbitrary",), vmem_limit_bytes=_vmem_limit(est)),
        name="s5",
    )(u4, kd, pin, pout, ar, ai)


def _s5_operators(lam_re, lam_im, log_step, b_re, b_im, c_re, c_im):
    lr, li = lam_re.astype(F32), lam_im.astype(F32)
    dt = jnp.exp(log_step.astype(F32))[:, None]
    mag = jnp.exp(lr * dt)
    lbar_re, lbar_im = mag * jnp.cos(li * dt), mag * jnp.sin(li * dt)
    nr, ni = lbar_re - 1.0, lbar_im
    den = lr * lr + li * li
    coef_re = (nr * lr + ni * li) / den
    coef_im = (ni * lr - nr * li) / den
    br, bi = b_re.astype(F32), b_im.astype(F32)
    bbar_re = coef_re[..., None] * br - coef_im[..., None] * bi
    bbar_im = coef_re[..., None] * bi + coef_im[..., None] * br
    cr, ci = c_re.astype(F32), c_im.astype(F32)

    pr, pi = [jnp.ones_like(lbar_re)], [jnp.zeros_like(lbar_re)]
    for _ in range(S5_CHUNK):
        pr, pi = (pr + [pr[-1] * lbar_re - pi[-1] * lbar_im],
                  pi + [pr[-1] * lbar_im + pi[-1] * lbar_re])
    pw_re, pw_im = jnp.stack(pr), jnp.stack(pi)
    cs = S5_CHUNK

    def block_diag(compact, rows_group, cols_group, reps):
        tiled = jnp.tile(compact, reps)
        r = lax.broadcasted_iota(jnp.int32, tiled.shape, tiled.ndim - 2)
        c = lax.broadcasted_iota(jnp.int32, tiled.shape, tiled.ndim - 1)
        return jnp.where(rows_group(r) == cols_group(c), tiled, 0.0).astype(BF16)

    lane_group = lambda c: c // S5_GROUP
    state_group = lambda c: (c % S5_NSTATE) // S5_STATE

    ab_re = pw_re[:cs, :, :, None] * bbar_re - pw_im[:cs, :, :, None] * bbar_im
    ab_im = pw_re[:cs, :, :, None] * bbar_im + pw_im[:cs, :, :, None] * bbar_re
    taps = jnp.sum(cr[None, :, :, :, None] * ab_re[:, :, None] - ci[None, :, :, :, None] * ab_im[:, :, None],
                   axis=3)
    kd = block_diag(taps.transpose(0, 1, 3, 2).reshape(cs * S5_WIDTH, S5_GROUP),
                    lambda r: (r // S5_GROUP) % S5_GROUPS, lane_group, (1, S5_GROUPS))

    rev = cs - 1 - jnp.arange(cs)
    ab = jnp.stack([ab_re[rev], ab_im[rev]], axis=1)
    pin = block_diag(ab.transpose(0, 4, 1, 2, 3).reshape(cs, S5_GROUP, 2 * S5_NSTATE),
                     lane_group, state_group, (1, S5_GROUPS, 1))

    qr, qi = pw_re[1:], pw_im[1:]
    po_re = cr[None] * qr[:, :, None, :] - ci[None] * qi[:, :, None, :]
    po_im = -(cr[None] * qi[:, :, None, :] + ci[None] * qr[:, :, None, :])
    po = jnp.stack([po_re, po_im], axis=1)
    pout = block_diag(po.transpose(0, 1, 2, 4, 3).reshape(cs, 2 * S5_NSTATE, S5_GROUP),
                      state_group, lane_group, (1, 1, S5_GROUPS))
    a_re = pw_re[cs].reshape(1, S5_NSTATE)
    a_im = pw_im[cs].reshape(1, S5_NSTATE)
    return kd, pin, pout, a_re, a_im


def _out_ffn_kernel(x_ref, oa_ref, ob_ref, y_ref, u_ref, d_ref, gw_ref, gb_ref, wa_ref, wb_ref, wc_ref,
                    g_ref, wg_ref, wu_ref, wd_ref, o_ref):
    tm = x_ref.shape[0]
    y = y_ref[...].reshape(tm, S5_WIDTH) + d_ref[...] * u_ref[...].reshape(tm, S5_WIDTH)
    z = 0.5 * y * (1.0 + jnp.tanh(math.sqrt(2.0 / math.pi) * (y + 0.044715 * (y * y * y))))
    oc = z * _sigmoid(_dot(z.astype(BF16), gw_ref[...]) + gb_ref[...])
    x = (x_ref[...] + _dot(oa_ref[...], wa_ref[...]) + _dot(ob_ref[...], wb_ref[...])
         + _dot(oc.astype(BF16), wc_ref[...]))
    o_ref[...] = _swiglu_residual(x, g_ref[...], wg_ref, wu_ref, wd_ref)


def _out_ffn(x, oa, ob, y_lb, u_lb, dvec, gw, gb, wa, wb, wc, gain, wg, wu, wd, layer):
    b, l, d = x.shape
    f = wg.shape[2]
    stacked = lambda shape: pl.BlockSpec((None,) + shape[1:], lambda bi, i: (layer, 0, 0),
                                         pipeline_mode=pl.Buffered(1))
    tm = min(OUT_ROWS, l)
    rows = lambda w: pl.BlockSpec((None, tm, w), lambda bi, i: (bi, i, 0))
    token_major = pl.BlockSpec((tm // S5_CHUNK, None, S5_CHUNK, S5_WIDTH), lambda bi, i: (i, bi, 0, 0))
    full = lambda shape: pl.BlockSpec(shape, lambda bi, i: tuple(0 for _ in shape),
                                      pipeline_mode=pl.Buffered(1))
    est = (2 * (2 * tm * d * 4 + tm * (256 * 2 + 512 * 2 + 256 * 8)) + 2 * (d * d + 256 * 256) + 3 * d * f * 2
           + tm * f * (4 + 4 + 2 + 4) + 4 * tm * d * 4)
    return pl.pallas_call(
        _out_ffn_kernel,
        grid=(b, l // tm),
        in_specs=[rows(d), rows(FOX_WIDTH), rows(HGRN_WIDTH), token_major, token_major,
                  full((1, S5_WIDTH)), full(gw.shape), full((1, S5_WIDTH)),
                  full(wa.shape), full(wb.shape), full(wc.shape),
                  full((1, d)), stacked(wg.shape), stacked(wu.shape), stacked(wd.shape)],
        out_specs=rows(d),
        out_shape=jax.ShapeDtypeStruct((b, l, d), F32),
        compiler_params=pltpu.CompilerParams(
            dimension_semantics=("arbitrary", "arbitrary"), vmem_limit_bytes=_vmem_limit(est)),
        name="out_ffn",
    )(x, oa, ob, y_lb, u_lb, dvec, gw, gb, wa, wb, wc, gain, wg, wu, wd)


def kernel(x, ffn1_norm, ffn1_w_gate, ffn1_w_up, ffn1_w_down, mix_norm, w_in, fox_f_bias, fox_q_gain, fox_k_gain, hgrn_lb, hgrn_o_gain, s5_lambda_re, s5_lambda_im, s5_log_step, s5_B_re, s5_B_im, s5_C_re, s5_C_im, s5_D, s5_glu_w, s5_glu_b, w_out, ffn2_norm, ffn2_w_gate, ffn2_w_up, ffn2_w_down):
    bsz, seq, d = x.shape
    depth = w_in.shape[0]
    t = bsz * seq

    lb_all = jnp.cumsum(jax.nn.softmax(hgrn_lb.astype(F32), axis=0), axis=0)
    lb_all = lb_all - lb_all[0:1]

    o_q, o_k, o_v = 0, FOX_WIDTH, 2 * FOX_WIDTH
    o_f = 3 * FOX_WIDTH
    o_h = o_f + FOX_HEADS
    o_u = o_h + 4 * HGRN_WIDTH

    ffn1 = [_to_bf16(w) for w in (ffn1_w_gate, ffn1_w_up, ffn1_w_down)]
    ffn2 = [_to_bf16(w) for w in (ffn2_w_gate, ffn2_w_up, ffn2_w_down)]

    for l in range(depth):
        x = _ffn(x.reshape(t, d), ffn1_norm[l][None], *ffn1, layer=l).reshape(bsz, seq, d)

        w = w_in[l]
        wf = jnp.zeros((d, 128), F32).at[:, :3 * FOX_HEADS].set(jnp.tile(w[:, o_f:o_h], (1, 3))).astype(BF16)
        fb = jnp.zeros((1, 128), F32).at[0, :3 * FOX_HEADS].set(jnp.tile(fox_f_bias[l].astype(F32), 3))
        q, k, vt, cs, hg, u_lb = _proj(
            x, mix_norm[l][None], w[:, o_q:o_v].astype(BF16), w[:, o_v:o_f].T.astype(BF16), wf,
            w[:, o_h:o_u].astype(BF16), w[:, o_u:].astype(BF16),
            jnp.tile(fox_q_gain[l].astype(F32), FOX_HEADS)[None],
            jnp.tile(fox_k_gain[l].astype(F32), FOX_HEADS)[None], fb, lb_all[l][None])

        o_a = _attn(q, k, vt, cs, fox_q_gain[l], fox_k_gain[l])
        o_b = _hgrn(hg, hgrn_o_gain[l].astype(F32)[None])
        ops = _s5_operators(s5_lambda_re[l], s5_lambda_im[l], s5_log_step[l],
                            s5_B_re[l], s5_B_im[l], s5_C_re[l], s5_C_im[l])
        y_lb = _s5(u_lb, *ops)

        wo = w_out[l].astype(BF16)
        x = _out_ffn(x, o_a, o_b, y_lb, u_lb, s5_D[l].astype(F32)[None], s5_glu_w[l].astype(BF16),
                     s5_glu_b[l].astype(F32)[None], wo[:FOX_WIDTH], wo[FOX_WIDTH:FOX_WIDTH + HGRN_WIDTH],
                     wo[FOX_WIDTH + HGRN_WIDTH:], ffn2_norm[l][None], *ffn2, layer=l)
    return x
```

```python
import math

import jax
import jax.numpy as jnp
import numpy as np
from jax import lax
from jax.experimental import pallas as pl
from jax.experimental.pallas import tpu as pltpu

F32 = jnp.float32
BF16 = jnp.bfloat16

EPS = 1e-6
FOX_HEADS = 4
FOX_HEAD_DIM = 64
FOX_WIDTH = FOX_HEADS * FOX_HEAD_DIM
HGRN_HEADS = 4
HGRN_DIM = 128
HGRN_WIDTH = HGRN_HEADS * HGRN_DIM
S5_GROUPS = 16
S5_GROUP = 16
S5_STATE = 64
S5_WIDTH = S5_GROUPS * S5_GROUP
S5_NSTATE = S5_GROUPS * S5_STATE
S5_CHUNK = 8
S5_TILE = 128
CAST_STEPS = 8
LOG2E = math.log2(math.e)

V7X_VMEM_BYTES = 64 * 1024 * 1024
V7X_SUBLANES = 8

ATTN_BLOCK = 512
ATTN_KV_BLOCKS = 1
ATTN_MAX_FREE_LOGIT = 40.0
HGRN_CHUNK = 128
HGRN_CHUNKS_PER_STEP = 8
FFN_ROWS = 512
FFN_SPLIT = 4
PROJ_ROWS = 512
PROJ_SPLIT = 2
OUT_ROWS = 512


def _vmem_limit(estimate_bytes):
    return int(min(estimate_bytes * 5 // 4 + (4 << 20), V7X_VMEM_BYTES - (4 << 20)))


def _rms_norm(x, gain):
    ms = jnp.mean(x * x, axis=-1, keepdims=True)
    return x * lax.rsqrt(ms + EPS) * gain


def _dot(a, b):
    return jnp.dot(a, b, preferred_element_type=F32)


def _dot_nt(a, b):
    return lax.dot_general(a, b, (((1,), (1,)), ((), ())), preferred_element_type=F32)


def _dot_tn(a, b):
    return lax.dot_general(a, b, (((0,), (0,)), ((), ())), preferred_element_type=F32)


def _split3(a):
    a1 = a.astype(BF16)
    r1 = a - a1.astype(F32)
    a2 = r1.astype(BF16)
    a3 = (r1 - a2.astype(F32)).astype(BF16)
    return a1, a2, a3


def _dot_exact_rhs(a, rhs, terms=3):
    parts = _split3(a)[:terms]
    out = _dot(parts[0], rhs)
    for part in parts[1:]:
        out = out + _dot(part, rhs)
    return out


def _dot_exact_lhs(lhs, a, terms=3):
    parts = _split3(a)[:terms]
    out = _dot(lhs, parts[0])
    for part in parts[1:]:
        out = out + _dot(lhs, part)
    return out


def _sigmoid(x):
    return 1.0 / (1.0 + jnp.exp(-x))


def _cumsum_rows(a):
    row = lax.broadcasted_iota(jnp.int32, a.shape, 0)
    shift = 1
    while shift < a.shape[0]:
        a = a + jnp.where(row >= shift, pltpu.roll(a, shift, axis=0), 0.0)
        shift *= 2
    return a


def _cast_kernel(*refs):
    n = len(refs) // 2
    for w_ref, o_ref in zip(refs[:n], refs[n:]):
        o_ref[...] = w_ref[...].astype(o_ref.dtype)


def _to_bf16(*ws):
    depth = ws[0].shape[0]
    specs, block_bytes = [], 0
    for w in ws:
        rows, cols = w.shape[1:]
        tr = rows // CAST_STEPS
        assert w.shape[0] == depth and tr * CAST_STEPS == rows and tr % V7X_SUBLANES == 0
        specs.append(pl.BlockSpec((None, tr, cols), lambda l, i: (l, i, 0)))
        block_bytes += tr * cols * (4 + 2)
    return pl.pallas_call(
        _cast_kernel,
        grid=(depth, CAST_STEPS),
        in_specs=specs,
        out_specs=specs,
        out_shape=[jax.ShapeDtypeStruct(w.shape, BF16) for w in ws],
        compiler_params=pltpu.CompilerParams(
            dimension_semantics=("arbitrary", "arbitrary"), vmem_limit_bytes=_vmem_limit(2 * block_bytes)),
        name="cast_bf16",
    )(*ws)


def _swiglu_residual(x, gain, wg_ref, wu_ref, wd_ref):
    h = _rms_norm(x, gain).astype(BF16)
    gate = _dot(h, wg_ref[...])
    up = _dot(h, wu_ref[...])
    act = (gate * _sigmoid(gate) * up).astype(BF16)
    return x + 0.5 * _dot(act, wd_ref[...])


def _ffn_kernel(x_ref, g_ref, wg_ref, wu_ref, wd_ref, o_ref):
    part_rows = x_ref.shape[0] // FFN_SPLIT
    for part in range(FFN_SPLIT):
        rows = slice(part * part_rows, (part + 1) * part_rows)
        o_ref[rows, :] = _swiglu_residual(x_ref[rows, :], g_ref[...], wg_ref, wu_ref, wd_ref)


def _ffn(x2d, gain, wg, wu, wd, layer):
    t, d = x2d.shape
    f = wg.shape[2]
    tm = min(FFN_ROWS, t)
    const = dict(pipeline_mode=pl.Buffered(1))
    est = 3 * d * f * 2 + 4 * tm * d * 4 + tm * f * (4 + 4 + 2 + 4) + tm * d * 4
    return pl.pallas_call(
        _ffn_kernel,
        grid=(t // tm,),
        in_specs=[
            pl.BlockSpec((tm, d), lambda i: (i, 0)),
            pl.BlockSpec((1, d), lambda i: (0, 0)),
            pl.BlockSpec((None, d, f), lambda i: (layer, 0, 0), **const),
            pl.BlockSpec((None, d, f), lambda i: (layer, 0, 0), **const),
            pl.BlockSpec((None, f, d), lambda i: (layer, 0, 0), **const),
        ],
        out_specs=pl.BlockSpec((tm, d), lambda i: (i, 0)),
        out_shape=jax.ShapeDtypeStruct((t, d), F32),
        compiler_params=pltpu.CompilerParams(
            dimension_semantics=("arbitrary",), vmem_limit_bytes=_vmem_limit(est)),
        name="ffn",
    )(x2d, gain, wg, wu, wd)


def _proj_kernel(x_ref, g_ref, wqk_ref, wvt_ref, wf_ref, wh_ref, wu_ref, qg_ref, kg_ref,
                 fb_ref, lb_ref, q_ref, k_ref, vt_ref, cs_ref, hg_ref, u_ref, carry_ref):
    @pl.when(pl.program_id(1) == 0)
    def _():
        carry_ref[...] = jnp.zeros_like(carry_ref)

    tk = vt_ref.shape[2]
    pr = x_ref.shape[0] // PROJ_SPLIT
    assert tk % pr == 0
    for part in range(PROJ_SPLIT):
        rows = pl.ds(part * pr, pr)
        blk, off = divmod(part * pr, tk)
        _proj_rows(x_ref.at[rows], g_ref, wqk_ref, wvt_ref, wf_ref, wh_ref, wu_ref, qg_ref, kg_ref, fb_ref,
                   lb_ref, q_ref.at[rows], k_ref.at[rows], vt_ref.at[blk, :, pl.ds(off, pr)], cs_ref.at[rows],
                   hg_ref.at[rows], u_ref.at[pl.ds(part * pr // S5_CHUNK, pr // S5_CHUNK)], carry_ref)


def _proj_rows(x_ref, g_ref, wqk_ref, wvt_ref, wf_ref, wh_ref, wu_ref, qg_ref, kg_ref,
               fb_ref, lb_ref, q_ref, k_ref, vt_ref, cs_ref, hg_ref, u_ref, carry_ref):
    tm = x_ref.shape[0]
    h = _rms_norm(x_ref[...], g_ref[...]).astype(BF16)

    li = lax.broadcasted_iota(jnp.int32, (FOX_WIDTH, FOX_WIDTH), 0) // FOX_HEAD_DIM
    lj = lax.broadcasted_iota(jnp.int32, (FOX_WIDTH, FOX_WIDTH), 1) // FOX_HEAD_DIM
    group = (li == lj).astype(BF16)

    def head_norm(a, gain):
        ms = _dot_exact_rhs(a * a, group, terms=1) * (1.0 / FOX_HEAD_DIM)
        return a * lax.rsqrt(ms + EPS) * gain

    qk = _dot(h, wqk_ref[...])
    q_ref[...] = (head_norm(qk[:, :FOX_WIDTH], qg_ref[...]) * (LOG2E / math.sqrt(FOX_HEAD_DIM))).astype(BF16)
    k_ref[...] = head_norm(qk[:, FOX_WIDTH:], kg_ref[...]).astype(BF16)
    vt_ref[...] = _dot_nt(wvt_ref[...], h).astype(BF16)

    fl = _dot(h, wf_ref[...]) + fb_ref[...]
    ls = jnp.minimum(fl, 0.0) - jnp.log1p(jnp.exp(-jnp.abs(fl)))
    c = _cumsum_rows(ls) + carry_ref[0:1, :]
    carry_ref[...] = jnp.broadcast_to(c[tm - 1:tm, :], carry_ref.shape)
    c1, c2, c3 = _split3(c * LOG2E)
    lane = lax.broadcasted_iota(jnp.int32, c.shape, 1)
    cs_ref[...] = jnp.where(lane < FOX_HEADS, c1, jnp.where(lane < 2 * FOX_HEADS, c2, c3))

    hg = _dot(h, wh_ref[...])
    w = HGRN_WIDTH
    lb = lb_ref[...]
    f = lb + (1.0 - lb) * _sigmoid(hg[:, w:2 * w])
    g = hg[:, 3 * w:4 * w]
    hg_ref[:, 0:w] = hg[:, 0:w]
    hg_ref[:, w:2 * w] = jnp.log(f)
    hg_ref[:, 2 * w:3 * w] = f
    hg_ref[:, 3 * w:4 * w] = hg[:, 2 * w:3 * w]
    hg_ref[:, 4 * w:5 * w] = g * (0.5 * (jnp.tanh(0.5 * g) + 1.0))
    u_ref[...] = _dot(h, wu_ref[...]).reshape(u_ref.shape)


def _proj(x, gain, wqk, wvt, wf, wh, wu, qg, kg, fb, lb):
    b, l, d = x.shape
    tm = min(PROJ_ROWS, l)
    tk = min(ATTN_BLOCK, l)
    hw = 5 * HGRN_WIDTH
    full = lambda shape: pl.BlockSpec(shape, lambda bi, i: tuple(0 for _ in shape))
    est = 2 * (d * (512 + 256 + 128 + hw + 256) * 2) + 2 * tm * d * 4 + 2 * tm * (hw + 256 + 512) * 4 + 2 * tm * hw * 4
    return pl.pallas_call(
        _proj_kernel,
        grid=(b, l // tm),
        in_specs=[
            pl.BlockSpec((None, tm, d), lambda bi, i: (bi, i, 0)),
            full((1, d)), full(wqk.shape), full(wvt.shape), full(wf.shape), full(wh.shape),
            full(wu.shape), full((1, FOX_WIDTH)), full((1, FOX_WIDTH)), full((1, 128)),
            full((1, HGRN_WIDTH)),
        ],
        out_specs=[
            pl.BlockSpec((None, tm, FOX_WIDTH), lambda bi, i: (bi, i, 0)),
            pl.BlockSpec((None, tm, FOX_WIDTH), lambda bi, i: (bi, i, 0)),
            pl.BlockSpec((None, tm // tk, FOX_WIDTH, tk), lambda bi, i: (bi, i, 0, 0)),
            pl.BlockSpec((None, tm, 128), lambda bi, i: (bi, i, 0)),
            pl.BlockSpec((None, tm, hw), lambda bi, i: (bi, i, 0)),
            pl.BlockSpec((tm // S5_CHUNK, None, S5_CHUNK, S5_WIDTH), lambda bi, i: (i, bi, 0, 0)),
        ],
        out_shape=[
            jax.ShapeDtypeStruct((b, l, FOX_WIDTH), BF16),
            jax.ShapeDtypeStruct((b, l, FOX_WIDTH), BF16),
            jax.ShapeDtypeStruct((b, l // tk, FOX_WIDTH, tk), BF16),
            jax.ShapeDtypeStruct((b, l, 128), BF16),
            jax.ShapeDtypeStruct((b, l, hw), F32),
            jax.ShapeDtypeStruct((l // S5_CHUNK, b, S5_CHUNK, S5_WIDTH), F32),
        ],
        scratch_shapes=[pltpu.VMEM((V7X_SUBLANES, 128), F32)],
        compiler_params=pltpu.CompilerParams(
            dimension_semantics=("arbitrary", "arbitrary"), vmem_limit_bytes=_vmem_limit(est)),
        name="proj",
    )(x, gain, wqk, wvt, wf, wh, wu, qg, kg, fb, lb)


def _attn_kernel(bounded_ref, q_ref, k_ref, vt_ref, cs_ref, e_ref, pq_ref, pk_ref, oq_ref, ok_ref,
                 o_ref, ka_ref, qa_ref, m_ref, l_ref, acc_ref, s_ref, p_ref):
    tq = o_ref.shape[0]
    i = pl.program_id(1)

    @pl.when(i == 0)
    def _():
        cs = cs_ref[...]
        ka_ref[...] = (_dot(k_ref[...], e_ref[...]) + _dot(cs, pk_ref[...]) + ok_ref[...]).astype(BF16)
        qa_ref[...] = (_dot(q_ref[...], e_ref[...]) + _dot(cs, pq_ref[...]) + oq_ref[...]).astype(BF16)

    l_ref[...] = jnp.zeros_like(l_ref)
    acc_ref[...] = jnp.zeros_like(acc_ref)
    nblk = ATTN_KV_BLOCKS
    tk = nblk * tq
    nfull = i // nblk
    qrows = pl.ds(pl.multiple_of(i * tq, tq), tq)
    key_minus_qry = (lax.broadcasted_iota(jnp.int32, (tk, tq), 0)
                     - lax.broadcasted_iota(jnp.int32, (tk, tq), 1))

    def causal(s, jj):
        return jnp.where(key_minus_qry <= (i - jj * nblk) * tq, s, -jnp.inf)

    def pv_dot(jj, hd, p):
        head_rows = slice(hd * FOX_HEAD_DIM, (hd + 1) * FOX_HEAD_DIM)
        pv = _dot(vt_ref[jj * nblk, head_rows, :], p[0:tq, :])
        for bk in range(1, nblk):
            pv = pv + _dot(vt_ref[jj * nblk + bk, head_rows, :], p[bk * tq:(bk + 1) * tq, :])
        return pv

    def bounded_steps(steps):
        for slot, jj in enumerate(steps):
            rows = pl.ds(pl.multiple_of(jj * tk, tk), tk)
            for hd in range(FOX_HEADS):
                slab = slice(hd * 128, (hd + 1) * 128)
                p = jnp.exp2(causal(_dot_nt(ka_ref[rows, slab], qa_ref[qrows, slab]), jj))
                l_ref[hd] = l_ref[hd] + jnp.sum(p, axis=0, keepdims=True)
                p_ref[slot, hd] = p.astype(BF16)
        for hd in range(FOX_HEADS):
            pv = pv_dot(steps[0], hd, p_ref[0, hd])
            for slot, jj in enumerate(steps[1:], 1):
                pv = pv + pv_dot(jj, hd, p_ref[slot, hd])
            acc_ref[hd] = acc_ref[hd] + pv

    @pl.when(bounded_ref[0] == 1)
    def _():
        def pair(kk, carry):
            bounded_steps([2 * kk, 2 * kk + 1])
            return carry

        lax.fori_loop(0, (nfull + 1) // 2, pair, 0)

        @pl.when(nfull % 2 == 0)
        def _():
            bounded_steps([nfull])

    @pl.when(bounded_ref[0] == 0)
    def _():
        _attn_online(i, nfull, tk, qrows, causal, pv_dot, ka_ref, qa_ref, m_ref, l_ref, acc_ref,
                     s_ref, p_ref.at[0])

    outs = [acc_ref[hd] / l_ref[hd] for hd in range(FOX_HEADS)]
    o_ref[...] = jnp.concatenate(outs, axis=0).T.astype(o_ref.dtype)


def _attn_online(i, nfull, tk, qrows, causal, pv_dot, ka_ref, qa_ref, m_ref, l_ref, acc_ref, s_ref, p_ref):
    m_ref[...] = jnp.full(m_ref.shape, -jnp.inf, F32)

    def logits(jj, slot):
        rows = pl.ds(pl.multiple_of(jj * tk, tk), tk)
        for hd in range(FOX_HEADS):
            slab = slice(hd * 128, (hd + 1) * 128)
            s_ref[slot, hd] = _dot_nt(ka_ref[rows, slab], qa_ref[qrows, slab])

    def update(jj, slot, last):
        alphas = []
        for hd in range(FOX_HEADS):
            s = s_ref[slot, hd]
            if last:
                s = causal(s, jj)
            m_old = m_ref[hd]
            m_new = jnp.maximum(m_old, jnp.max(s, axis=0, keepdims=True))
            alpha = jnp.exp2(m_old - m_new)
            p = jnp.exp2(s - m_new)
            m_ref[hd] = m_new
            l_ref[hd] = alpha * l_ref[hd] + jnp.sum(p, axis=0, keepdims=True)
            p_ref[hd] = p.astype(BF16)
            alphas.append(alpha)
        for hd in range(FOX_HEADS):
            acc_ref[hd] = alphas[hd] * acc_ref[hd] + pv_dot(jj, hd, p_ref[hd])

    def pair(kk, carry):
        logits(2 * kk + 1, 1)
        update(2 * kk, 0, False)
        logits(2 * kk + 2, 0)
        update(2 * kk + 1, 1, False)
        return carry

    logits(0, 0)
    npair = nfull // 2
    lax.fori_loop(0, npair, pair, 0)

    @pl.when(nfull % 2 == 0)
    def _():
        update(2 * npair, 0, True)

    @pl.when(nfull % 2 == 1)
    def _():
        logits(2 * npair + 1, 1)
        update(2 * npair, 0, False)
        update(2 * npair + 1, 1, True)


def _attn_placement():
    w = FOX_HEADS * 128
    e = np.zeros((FOX_WIDTH, w), np.float32)
    cols = np.arange(FOX_WIDTH)
    e[cols, (cols // FOX_HEAD_DIM) * 128 + cols % FOX_HEAD_DIM] = 1.0
    pq, pk = np.zeros((128, w), np.float32), np.zeros((128, w), np.float32)
    oq, ok = np.zeros((1, w), np.float32), np.zeros((1, w), np.float32)
    for hd in range(FOX_HEADS):
        base = hd * 128 + FOX_HEAD_DIM
        for term in range(3):
            pk[term * FOX_HEADS + hd, base + term] = -1.0
            oq[0, base + term] = 1.0
            pq[term * FOX_HEADS + hd, base + 3 + term] = 1.0
            ok[0, base + 3 + term] = 1.0
    return (jnp.asarray(e, BF16), jnp.asarray(pq, BF16), jnp.asarray(pk, BF16),
            jnp.asarray(oq), jnp.asarray(ok))


def _attn(q, k, vt, cs, q_gain, k_gain):
    b, l, w = q.shape
    nkv, tk = vt.shape[1], vt.shape[3]
    tq = tk
    wa = FOX_HEADS * 128
    e, pq, pk, oq, ok = _attn_placement()
    full = lambda shape: pl.BlockSpec(shape, lambda bi, i: tuple(0 for _ in shape))
    seq = lambda width: pl.BlockSpec((None, l, width), lambda bi, i: (bi, 0, 0))
    assert (l // tq) % ATTN_KV_BLOCKS == 0
    logit_bound = (1.01 * FOX_HEAD_DIM * LOG2E / math.sqrt(FOX_HEAD_DIM)
                   * jnp.max(jnp.abs(q_gain.astype(F32))) * jnp.max(jnp.abs(k_gain.astype(F32))))
    bounded = (logit_bound < ATTN_MAX_FREE_LOGIT).astype(jnp.int32).reshape(1)
    est = (2 * (l * w * 2 * 2 + l * 128 * 2) + 2 * l * wa * (2 + 4 + 4)
           + FOX_HEADS * ATTN_KV_BLOCKS * tq * tq * (2 * 4 + 2 * 2 + 2 * 4) + FOX_HEADS * 128 * tq * 4)
    return pl.pallas_call(
        _attn_kernel,
        grid=(b, l // tq),
        in_specs=[
            pl.BlockSpec(memory_space=pltpu.SMEM),
            seq(w), seq(w),
            pl.BlockSpec((None, nkv, w, tk), lambda bi, i: (bi, 0, 0, 0)),
            seq(128),
            full(e.shape), full(pq.shape), full(pk.shape), full(oq.shape), full(ok.shape),
        ],
        out_specs=pl.BlockSpec((None, tq, w), lambda bi, i: (bi, i, 0)),
        out_shape=jax.ShapeDtypeStruct((b, l, w), BF16),
        scratch_shapes=[pltpu.VMEM((l, wa), BF16), pltpu.VMEM((l, wa), BF16),
                        pltpu.VMEM((FOX_HEADS, 1, tq), F32), pltpu.VMEM((FOX_HEADS, 1, tq), F32),
                        pltpu.VMEM((FOX_HEADS, FOX_HEAD_DIM, tq), F32),
                        pltpu.VMEM((2, FOX_HEADS, ATTN_KV_BLOCKS * tq, tq), F32),
                        pltpu.VMEM((2, FOX_HEADS, ATTN_KV_BLOCKS * tq, tq), BF16)],
        compiler_params=pltpu.CompilerParams(
            dimension_semantics=("arbitrary", "arbitrary"), vmem_limit_bytes=_vmem_limit(est)),
        name="fox_attn",
    )(bounded, q, k, vt, cs, e, pq, pk, oq, ok)


def _hgrn_levels(c):
    return [c >> (s + 1) for s in range(int(math.log2(c)))]


def _hgrn_level_ids(c):
    t = np.arange(c)[:, None]
    s = np.arange(c)[None, :]
    lvl = np.floor(np.log2(np.maximum(t ^ s, 1))).astype(np.int32) + 1
    return jnp.asarray(np.where(s > t, -1, np.where(s == t, 0, lvl)).astype(np.int32))


def _hgrn_kernel(q_ref, lf_ref, f_ref, i_ref, gate_ref, og_ref, lvl_ref, o_ref, st_ref, b_ref):
    @pl.when(pl.program_id(1) == 0)
    def _():
        st_ref[...] = jnp.zeros_like(st_ref)

    for sub in range(q_ref.shape[0] // HGRN_CHUNK):
        rows = pl.ds(sub * HGRN_CHUNK, HGRN_CHUNK)
        _hgrn_chunk(q_ref.at[rows], lf_ref.at[rows], f_ref.at[rows], i_ref.at[rows], gate_ref.at[rows],
                    og_ref, lvl_ref, o_ref.at[rows], st_ref, b_ref.at[rows])


def _hgrn_chunk(q_ref, lf_ref, f_ref, i_ref, gate_ref, og_ref, lvl_ref, o_ref, st_ref, b_ref):
    c = q_ref.shape[0]
    f = f_ref[...]
    key = 1.0 - f
    q = q_ref[...]
    ti = lax.broadcasted_iota(jnp.int32, (c, c), 0)
    si = lax.broadcasted_iota(jnp.int32, (c, c), 1)
    lower = (si <= ti).astype(BF16)
    b = _dot_exact_lhs(lower, lf_ref[...], terms=2)
    b_ref[...] = b
    row = lax.broadcasted_iota(jnp.int32, b.shape, 0)

    def ref_rows(m):
        parts = [jnp.broadcast_to(b_ref[g0 + m - 1:g0 + m, :], (2 * m, b.shape[1]))
                 for g0 in range(0, c, 2 * m)]
        return jnp.concatenate(parts, axis=0) if len(parts) > 1 else parts[0]

    def level_decay(m):
        if m == 1:
            return jnp.where(row % 2 == 1, f, 1.0)
        if m == 2:
            off = row % 4
            nxt, prv = pltpu.roll(f, c - 1, axis=0), pltpu.roll(f, 1, axis=0)
            return jnp.where(off == 0, nxt, jnp.where(off == 1, 1.0, jnp.where(off == 2, f, f * prv)))
        return jnp.exp2(jnp.abs(b - ref_rows(m)) * (-LOG2E))

    lvl = lvl_ref[...]
    qb, kb = q.astype(BF16), key.astype(BF16)
    scores = [jnp.zeros((c, c), F32) for _ in range(HGRN_HEADS)]
    for idx, m in enumerate([0] + _hgrn_levels(c)[::-1]):
        if m == 0:
            z, y = qb, kb
        else:
            e = level_decay(m).astype(BF16)
            z, y = qb * e, kb * e
        mask = lvl == idx
        for hd in range(HGRN_HEADS):
            sl = slice(hd * HGRN_DIM, (hd + 1) * HGRN_DIM)
            scores[hd] = jnp.where(mask, _dot_nt(z[:, sl], y[:, sl]), scores[hd])

    b_last = b[c - 1:c, :]
    q_in = (q * jnp.exp(b)).astype(BF16)
    k_out = (key * jnp.exp(b_last - b)).astype(BF16)
    v = i_ref[...].astype(BF16)
    gate = gate_ref[...]
    og = og_ref[...]
    decay = jnp.exp(b_last)
    for hd in range(HGRN_HEADS):
        sl = slice(hd * HGRN_DIM, (hd + 1) * HGRN_DIM)
        st = st_ref[hd]
        o = _dot(scores[hd].astype(BF16), v[:, sl]) + _dot_nt(q_in[:, sl], st.astype(BF16))
        st_ref[hd] = st * decay[:, sl] + _dot_tn(v[:, sl], k_out[:, sl])
        o_ref[:, sl] = (_rms_norm(o, og[:, sl]) * gate[:, sl]).astype(o_ref.dtype)


def _hgrn(hg, og):
    b, l, _ = hg.shape
    c = HGRN_CHUNK
    tl = HGRN_CHUNK * HGRN_CHUNKS_PER_STEP
    assert l % tl == 0
    w = HGRN_WIDTH
    part = lambda p: pl.BlockSpec((None, tl, w), lambda bi, i, p=p: (bi, i, p))
    est = 2 * 6 * tl * w * 4 + 16 * tl * w * 4 + HGRN_HEADS * HGRN_DIM * HGRN_DIM * 4
    return pl.pallas_call(
        _hgrn_kernel,
        grid=(b, l // tl),
        in_specs=[part(0), part(1), part(2), part(3), part(4),
                  pl.BlockSpec((1, w), lambda bi, i: (0, 0)),
                  pl.BlockSpec((c, c), lambda bi, i: (0, 0))],
        out_specs=pl.BlockSpec((None, tl, w), lambda bi, i: (bi, i, 0)),
        out_shape=jax.ShapeDtypeStruct((b, l, w), BF16),
        scratch_shapes=[pltpu.VMEM((HGRN_HEADS, HGRN_DIM, HGRN_DIM), F32),
                        pltpu.VMEM((tl, w), F32)],
        compiler_params=pltpu.CompilerParams(
            dimension_semantics=("arbitrary", "arbitrary"), vmem_limit_bytes=_vmem_limit(est)),
        name="hgrn2",
    )(hg, hg, hg, hg, hg, og, _hgrn_level_ids(c))


def _s5_kernel(u_ref, kd_ref, pin_ref, pout_ref, ar_ref, ai_ref, y_ref, xr_ref, xi_ref):
    nc, nb, cs, w = u_ref.shape
    ns = ar_ref.shape[1]

    @pl.when(pl.program_id(0) == 0)
    def _():
        xr_ref[...] = jnp.zeros_like(xr_ref)
        xi_ref[...] = jnp.zeros_like(xi_ref)

    us = [u_ref[:, :, s, :].reshape(nc * nb, w).astype(BF16) for s in range(cs)]
    win = _dot(us[0], pin_ref[0])
    for s in range(1, cs):
        win = win + _dot(us[s], pin_ref[s])

    ar, ai = ar_ref[...], ai_ref[...]
    xr, xi = xr_ref[...], xi_ref[...]
    prev_r, prev_i = [], []
    for c in range(nc):
        prev_r.append(xr)
        prev_i.append(xi)
        wr = win[c * nb:(c + 1) * nb, :ns]
        wi = win[c * nb:(c + 1) * nb, ns:]
        xr, xi = ar * xr - ai * xi + wr, ar * xi + ai * xr + wi
    xr_ref[...] = xr
    xi_ref[...] = xi
    xp = jnp.concatenate([jnp.concatenate(prev_r, axis=0), jnp.concatenate(prev_i, axis=0)],
                         axis=1).astype(BF16)

    for t in range(cs):
        lags = jnp.concatenate([us[t - d] for d in range(t + 1)], axis=1) if t else us[0]
        yt = _dot(xp, pout_ref[t]) + _dot(lags, kd_ref[0:(t + 1) * w, :])
        y_ref[:, :, t, :] = yt.reshape(nc, nb, w)


def _s5(u4, kd, pin, pout, ar, ai):
    nchunk, nb, cs, w = u4.shape
    nc = min(S5_TILE // cs, nchunk)
    const = dict(pipeline_mode=pl.Buffered(1))
    full = lambda shape: pl.BlockSpec(shape, lambda i: tuple(0 for _ in shape), **const)
    rows = nc * cs * nb
    chunked = pl.BlockSpec((nc, nb, cs, w), lambda i: (i, 0, 0, 0))
    est = (4 * rows * w * 4 + (kd.size + pin.size + pout.size) * 2 + rows // cs * 2 * S5_NSTATE * (4 + 4 + 2)
           + rows * w * (2 + 2 + 4))
    return pl.pallas_call(
        _s5_kernel,
        grid=(nchunk // nc,),
        in_specs=[chunked, full(kd.shape), full(pin.shape), full(pout.shape), full(ar.shape), full(ai.shape)],
        out_specs=chunked,
        out_shape=jax.ShapeDtypeStruct(u4.shape, F32),
        scratch_shapes=[pltpu.VMEM((nb, S5_NSTATE), F32), pltpu.VMEM((nb, S5_NSTATE), F32)],
        compiler_params=pltpu.CompilerParams(
            dimension_semantics=("arbitrary",), vmem_limit_bytes=_vmem_limit(est)),
        name="s5",
    )(u4, kd, pin, pout, ar, ai)


def _s5_operators(lam_re, lam_im, log_step, b_re, b_im, c_re, c_im):
    lr, li = lam_re.astype(F32), lam_im.astype(F32)
    dt = jnp.exp(log_step.astype(F32))[:, None]
    mag = jnp.exp(lr * dt)
    lbar_re, lbar_im = mag * jnp.cos(li * dt), mag * jnp.sin(li * dt)
    nr, ni = lbar_re - 1.0, lbar_im
    den = lr * lr + li * li
    coef_re = (nr * lr + ni * li) / den
    coef_im = (ni * lr - nr * li) / den
    br, bi = b_re.astype(F32), b_im.astype(F32)
    bbar_re = coef_re[..., None] * br - coef_im[..., None] * bi
    bbar_im = coef_re[..., None] * bi + coef_im[..., None] * br
    cr, ci = c_re.astype(F32), c_im.astype(F32)

    pr, pi = [jnp.ones_like(lbar_re)], [jnp.zeros_like(lbar_re)]
    for _ in range(S5_CHUNK):
        pr, pi = (pr + [pr[-1] * lbar_re - pi[-1] * lbar_im],
                  pi + [pr[-1] * lbar_im + pi[-1] * lbar_re])
    pw_re, pw_im = jnp.stack(pr), jnp.stack(pi)
    cs = S5_CHUNK

    def block_diag(compact, rows_group, cols_group, reps):
        tiled = jnp.tile(compact, reps)
        r = lax.broadcasted_iota(jnp.int32, tiled.shape, tiled.ndim - 2)
        c = lax.broadcasted_iota(jnp.int32, tiled.shape, tiled.ndim - 1)
        return jnp.where(rows_group(r) == cols_group(c), tiled, 0.0).astype(BF16)

    lane_group = lambda c: c // S5_GROUP
    state_group = lambda c: (c % S5_NSTATE) // S5_STATE

    ab_re = pw_re[:cs, :, :, None] * bbar_re - pw_im[:cs, :, :, None] * bbar_im
    ab_im = pw_re[:cs, :, :, None] * bbar_im + pw_im[:cs, :, :, None] * bbar_re
    taps = jnp.sum(cr[None, :, :, :, None] * ab_re[:, :, None] - ci[None, :, :, :, None] * ab_im[:, :, None],
                   axis=3)
    kd = block_diag(taps.transpose(0, 1, 3, 2).reshape(cs * S5_WIDTH, S5_GROUP),
                    lambda r: (r // S5_GROUP) % S5_GROUPS, lane_group, (1, S5_GROUPS))

    rev = cs - 1 - jnp.arange(cs)
    ab = jnp.stack([ab_re[rev], ab_im[rev]], axis=1)
    pin = block_diag(ab.transpose(0, 4, 1, 2, 3).reshape(cs, S5_GROUP, 2 * S5_NSTATE),
                     lane_group, state_group, (1, S5_GROUPS, 1))

    qr, qi = pw_re[1:], pw_im[1:]
    po_re = cr[None] * qr[:, :, None, :] - ci[None] * qi[:, :, None, :]
    po_im = -(cr[None] * qi[:, :, None, :] + ci[None] * qr[:, :, None, :])
    po = jnp.stack([po_re, po_im], axis=1)
    pout = block_diag(po.transpose(0, 1, 2, 4, 3).reshape(cs, 2 * S5_NSTATE, S5_GROUP),
                      state_group, lane_group, (1, 1, S5_GROUPS))
    a_re = pw_re[cs].reshape(1, S5_NSTATE)
    a_im = pw_im[cs].reshape(1, S5_NSTATE)
    return kd, pin, pout, a_re, a_im


def _out_ffn_kernel(x_ref, oa_ref, ob_ref, y_ref, u_ref, d_ref, gw_ref, gb_ref, wa_ref, wb_ref, wc_ref,
                    g_ref, wg_ref, wu_ref, wd_ref, o_ref):
    tm = x_ref.shape[0]
    y = y_ref[...].reshape(tm, S5_WIDTH) + d_ref[...] * u_ref[...].reshape(tm, S5_WIDTH)
    z = 0.5 * y * (1.0 + jnp.tanh(math.sqrt(2.0 / math.pi) * (y + 0.044715 * (y * y * y))))
    oc = z * _sigmoid(_dot(z.astype(BF16), gw_ref[...]) + gb_ref[...])
    x = (x_ref[...] + _dot(oa_ref[...], wa_ref[...]) + _dot(ob_ref[...], wb_ref[...])
         + _dot(oc.astype(BF16), wc_ref[...]))
    o_ref[...] = _swiglu_residual(x, g_ref[...], wg_ref, wu_ref, wd_ref)


def _out_ffn(x, oa, ob, y_lb, u_lb, dvec, gw, gb, wa, wb, wc, gain, wg, wu, wd, layer):
    b, l, d = x.shape
    f = wg.shape[2]
    stacked = lambda shape: pl.BlockSpec((None,) + shape[1:], lambda bi, i: (layer, 0, 0),
                                         pipeline_mode=pl.Buffered(1))
    tm = min(OUT_ROWS, l)
    rows = lambda w: pl.BlockSpec((None, tm, w), lambda bi, i: (bi, i, 0))
    token_major = pl.BlockSpec((tm // S5_CHUNK, None, S5_CHUNK, S5_WIDTH), lambda bi, i: (i, bi, 0, 0))
    full = lambda shape: pl.BlockSpec(shape, lambda bi, i: tuple(0 for _ in shape),
                                      pipeline_mode=pl.Buffered(1))
    est = (2 * (2 * tm * d * 4 + tm * (256 * 2 + 512 * 2 + 256 * 8)) + 2 * (d * d + 256 * 256) + 3 * d * f * 2
           + tm * f * (4 + 4 + 2 + 4) + 4 * tm * d * 4)
    return pl.pallas_call(
        _out_ffn_kernel,
        grid=(b, l // tm),
        in_specs=[rows(d), rows(FOX_WIDTH), rows(HGRN_WIDTH), token_major, token_major,
                  full((1, S5_WIDTH)), full(gw.shape), full((1, S5_WIDTH)),
                  full(wa.shape), full(wb.shape), full(wc.shape),
                  full((1, d)), stacked(wg.shape), stacked(wu.shape), stacked(wd.shape)],
        out_specs=rows(d),
        out_shape=jax.ShapeDtypeStruct((b, l, d), F32),
        compiler_params=pltpu.CompilerParams(
            dimension_semantics=("arbitrary", "arbitrary"), vmem_limit_bytes=_vmem_limit(est)),
        name="out_ffn",
    )(x, oa, ob, y_lb, u_lb, dvec, gw, gb, wa, wb, wc, gain, wg, wu, wd)


def kernel(x, ffn1_norm, ffn1_w_gate, ffn1_w_up, ffn1_w_down, mix_norm, w_in, fox_f_bias, fox_q_gain, fox_k_gain, hgrn_lb, hgrn_o_gain, s5_lambda_re, s5_lambda_im, s5_log_step, s5_B_re, s5_B_im, s5_C_re, s5_C_im, s5_D, s5_glu_w, s5_glu_b, w_out, ffn2_norm, ffn2_w_gate, ffn2_w_up, ffn2_w_down):
    bsz, seq, d = x.shape
    depth = w_in.shape[0]
    t = bsz * seq

    lb_all = jnp.cumsum(jax.nn.softmax(hgrn_lb.astype(F32), axis=0), axis=0)
    lb_all = lb_all - lb_all[0:1]

    o_q, o_k, o_v = 0, FOX_WIDTH, 2 * FOX_WIDTH
    o_f = 3 * FOX_WIDTH
    o_h = o_f + FOX_HEADS
    o_u = o_h + 4 * HGRN_WIDTH

    cast = _to_bf16(ffn1_w_gate, ffn1_w_up, ffn1_w_down, ffn2_w_gate, ffn2_w_up, ffn2_w_down)
    ffn1, ffn2 = cast[:3], cast[3:]

    for l in range(depth):
        x = _ffn(x.reshape(t, d), ffn1_norm[l][None], *ffn1, layer=l).reshape(bsz, seq, d)

        w = w_in[l]
        wf = jnp.zeros((d, 128), F32).at[:, :3 * FOX_HEADS].set(jnp.tile(w[:, o_f:o_h], (1, 3))).astype(BF16)
        fb = jnp.zeros((1, 128), F32).at[0, :3 * FOX_HEADS].set(jnp.tile(fox_f_bias[l].astype(F32), 3))
        q, k, vt, cs, hg, u_lb = _proj(
            x, mix_norm[l][None], w[:, o_q:o_v].astype(BF16), w[:, o_v:o_f].T.astype(BF16), wf,
            w[:, o_h:o_u].astype(BF16), w[:, o_u:].astype(BF16),
            jnp.tile(fox_q_gain[l].astype(F32), FOX_HEADS)[None],
            jnp.tile(fox_k_gain[l].astype(F32), FOX_HEADS)[None], fb, lb_all[l][None])

        o_a = _attn(q, k, vt, cs, fox_q_gain[l], fox_k_gain[l])
        o_b = _hgrn(hg, hgrn_o_gain[l].astype(F32)[None])
        ops = _s5_operators(s5_lambda_re[l], s5_lambda_im[l], s5_log_step[l],
                            s5_B_re[l], s5_B_im[l], s5_C_re[l], s5_C_im[l])
        y_lb = _s5(u_lb, *ops)

        wo = w_out[l].astype(BF16)
        x = _out_ffn(x, o_a, o_b, y_lb, u_lb, s5_D[l].astype(F32)[None], s5_glu_w[l].astype(BF16),
                     s5_glu_b[l].astype(F32)[None], wo[:FOX_WIDTH], wo[FOX_WIDTH:FOX_WIDTH + HGRN_WIDTH],
                     wo[FOX_WIDTH + HGRN_WIDTH:], ffn2_norm[l][None], *ffn2, layer=l)
    return x
```
